```python
import math
import jax, jax.numpy as jnp
from jax import lax
import numpy as np

D_MODEL = 1024
BATCH = 4
SEQ = 4096
DEPTH = 2
DEC_BATCH = 128
DEC_SEQ = 1
PAST_LEN = 2048
PAGE_SIZE = 128

SC_WIDTH = D_MODEL
SC_KERNEL = 3
SSD_INNER = 2 * D_MODEL
SSD_HEAD_DIM = 64
SSD_HEADS = SSD_INNER // SSD_HEAD_DIM
SSD_GROUPS = 4
SSD_STATE = 128
SSD_CONV = 4
SSD_CHUNK = 256
SSD_CONV_DIM = SSD_INNER + 2 * SSD_GROUPS * SSD_STATE
IN_E_DIM = 3 * SC_WIDTH + SSD_INNER + SSD_CONV_DIM + SSD_HEADS
ATT_HEADS = 16
ATT_KV_HEADS = 4
ATT_HEAD_DIM = D_MODEL // ATT_HEADS
ROT_DIM = ATT_HEAD_DIM // 4
ROPE_THETA = 500000.0
MOBA_BLOCK = 256
MOBA_TOPK = 3
MOBA_Q_CHUNK = 32
QKV_DIM = (ATT_HEADS + 2 * ATT_KV_HEADS) * ATT_HEAD_DIM
D_FF = 4 * D_MODEL
EPS = 1e-6

kernel_name = 'hybrid_shortconv_ssd_moba_adaln_step'


def rms_norm(x, w):
    x32 = x.astype(jnp.float32)
    y = x32 * lax.rsqrt(jnp.mean(x32 * x32, axis=-1, keepdims=True) + EPS) * w.astype(jnp.float32)
    return y.astype(x.dtype)


def adaln(c, w, b):
    m = jax.nn.silu(c) @ w + b
    return [t[:, None, :] for t in jnp.split(m, 6, axis=-1)]


def modulate(h, shift, scale):
    return h * (1.0 + scale) + shift


def causal_dwconv(u, buf, w):
    full = jnp.concatenate([buf.astype(u.dtype), u], axis=1)
    out = lax.conv_general_dilated(full, w[:, None, :].astype(u.dtype), window_strides=(1,), padding='VALID',
                                   dimension_numbers=('NWC', 'WIO', 'NWC'), feature_group_count=u.shape[-1])
    return out, full[:, -(w.shape[0] - 1):]


def ssd_scan(x, dt, a_neg, bm, cm, h0):
    b, L, nh, p = x.shape
    g, n = bm.shape[2], bm.shape[3]
    e = nh // g
    q = min(SSD_CHUNK, L)
    lp = -(-L // q) * q

    def pad(t):
        return jnp.pad(t, [(0, 0), (0, lp - L)] + [(0, 0)] * (t.ndim - 2))

    x, dt, bm, cm = pad(x.astype(jnp.float32)), pad(dt), pad(bm.astype(jnp.float32)), pad(cm.astype(jnp.float32))
    nc = lp // q
    xr = x.reshape(b, nc, q, g, e, p)
    dtr = dt.reshape(b, nc, q, g, e)
    br = bm.reshape(b, nc, q, g, n)
    cr = cm.reshape(b, nc, q, g, n)
    a = dtr * a_neg.reshape(g, e)
    a_cs = jnp.cumsum(a, axis=2)
    xdt = xr * dtr[..., None]
    seg = a_cs[:, :, :, None] - a_cs[:, :, None, :]
    causal = jnp.tril(jnp.ones((q, q), dtype=bool))[None, None, :, :, None, None]
    decay = jnp.exp(jnp.where(causal, seg, -jnp.inf))
    cb = jnp.einsum('bclgn,bcsgn->bclsg', cr, br)
    y_diag = jnp.einsum('bclsge,bcsgep->bclgep', cb[..., None] * decay, xdt)
    decay_to_end = jnp.exp(a_cs[:, :, -1:] - a_cs)
    chunk_states = jnp.einsum('bclgn,bclge,bclgep->bcgepn', br, decay_to_end, xdt)
    chunk_decay = jnp.exp(a_cs[:, :, -1])

    def step(h, inp):
        s_c, d_c = inp
        return h * d_c[..., None, None] + s_c, h

    h_final, h_enter = lax.scan(step, h0.astype(jnp.float32).reshape(b, g, e, p, n),
                                (jnp.moveaxis(chunk_states, 1, 0), jnp.moveaxis(chunk_decay, 1, 0)))
    h_enter = jnp.moveaxis(h_enter, 0, 1)
    y_off = jnp.einsum('bclgn,bcgepn,bclge->bclgep', cr, h_enter, jnp.exp(a_cs))
    y = (y_diag + y_off).reshape(b, lp, nh, p)[:, :L]
    return y, h_final.reshape(b, nh, p, n)


def even_mixer(h, sc_buf, ssd_buf, ssm_h0, w_in, sc_w, conv_w, conv_b, dt_bias, a_log, d_skip, norm_w, w_out):
    b, L, _ = h.shape
    proj = h @ w_in
    s1 = SC_WIDTH
    s2 = 2 * SC_WIDTH
    s3 = 3 * SC_WIDTH
    s4 = s3 + SSD_INNER
    s5 = s4 + SSD_CONV_DIM
    sc_b, sc_c, sc_x, z, xbc, dt_raw = jnp.split(proj, [s1, s2, s3, s4, s5], axis=-1)
    conv_u, sc_buf_new = causal_dwconv(sc_c * sc_x, sc_buf, sc_w)
    y_sc = sc_b * conv_u
    xbc_c, ssd_buf_new = causal_dwconv(xbc, ssd_buf, conv_w)
    xbc_c = jax.nn.silu(xbc_c + conv_b)
    xs, bm, cm = jnp.split(xbc_c, [SSD_INNER, SSD_INNER + SSD_GROUPS * SSD_STATE], axis=-1)
    xs = xs.reshape(b, L, SSD_HEADS, SSD_HEAD_DIM)
    bm = bm.reshape(b, L, SSD_GROUPS, SSD_STATE)
    cm = cm.reshape(b, L, SSD_GROUPS, SSD_STATE)
    dt = jax.nn.softplus(dt_raw.astype(jnp.float32) + dt_bias.astype(jnp.float32))
    a_neg = -jnp.exp(a_log.astype(jnp.float32))
    y, h_new = ssd_scan(xs, dt, a_neg, bm, cm, ssm_h0)
    y = y + d_skip.astype(jnp.float32)[:, None] * xs.astype(jnp.float32)
    y = y.reshape(b, L, SSD_INNER) * jax.nn.silu(z.astype(jnp.float32))
    yg = y.reshape(b, L, SSD_GROUPS, SSD_INNER // SSD_GROUPS)
    yg = yg * lax.rsqrt(jnp.mean(yg * yg, axis=-1, keepdims=True) + EPS)
    y_ssd = (yg.reshape(b, L, SSD_INNER) * norm_w.astype(jnp.float32)).astype(h.dtype)
    out = jnp.concatenate([y_sc, y_ssd], axis=-1) @ w_out
    return out, sc_buf_new, ssd_buf_new, h_new


def partial_rope(t, pos):
    half = ROT_DIM // 2
    inv = ROPE_THETA ** (-jnp.arange(half, dtype=jnp.float32) * 2.0 / ROT_DIM)
    ang = pos.astype(jnp.float32)[:, None] * inv[None, :]
    cos = jnp.cos(ang)[None, :, None, :]
    sin = jnp.sin(ang)[None, :, None, :]
    t32 = t.astype(jnp.float32)
    t1, t2, rest = t32[..., :half], t32[..., half:ROT_DIM], t32[..., ROT_DIM:]
    return jnp.concatenate([t1 * cos - t2 * sin, t2 * cos + t1 * sin, rest], axis=-1).astype(t.dtype)


def moba_attention(q, k_all, v_all, q_start):
    bsz, s_len, n_h, hd = q.shape
    t_len, n_kv = k_all.shape[1], k_all.shape[2]
    grp = n_h // n_kv
    n_blk = -(-t_len // MOBA_BLOCK)
    pad = n_blk * MOBA_BLOCK - t_len

    def blocks(t):
        t = jnp.pad(t, ((0, 0), (0, pad), (0, 0), (0, 0)))
        return t.reshape(bsz, n_blk, MOBA_BLOCK, n_kv, hd).transpose(0, 3, 1, 2, 4)

    k_blk, v_blk = blocks(k_all), blocks(v_all)
    k_mean = jnp.mean(k_blk.astype(jnp.float32), axis=3)
    q_pos = q_start + jnp.arange(s_len, dtype=jnp.int32)
    cur = q_pos // MOBA_BLOCK
    qg = q.reshape(bsz, s_len, n_kv, grp, hd)
    score = jnp.einsum('bskgd,bknd->bskgn', qg.astype(jnp.float32), k_mean)
    past = jnp.arange(n_blk)[None, :] < cur[:, None]
    score = jnp.where(past[None, :, None, None, :], score, -jnp.inf)
    n_sel = min(MOBA_TOPK, n_blk)
    _, top_idx = lax.top_k(score, n_sel)
    own = jnp.broadcast_to(cur[None, :, None, None, None], (bsz, s_len, n_kv, grp, 1)).astype(top_idx.dtype)
    blk_idx = jnp.concatenate([top_idx, own], axis=-1)
    slot_ok = jnp.concatenate([jnp.arange(n_sel)[None, :] < cur[:, None],
                               jnp.ones((s_len, 1), dtype=bool)], axis=-1)
    ns = n_sel + 1
    qc = math.gcd(s_len, MOBA_Q_CHUNK)
    n_c = s_len // qc
    q_ch = jnp.moveaxis(qg.reshape(bsz, n_c, qc, n_kv, grp, hd), 1, 0)
    i_ch = jnp.moveaxis(blk_idx.reshape(bsz, n_c, qc, n_kv, grp, ns), 1, 0)
    p_ch = q_pos.reshape(n_c, qc)
    o_ch = slot_ok.reshape(n_c, qc, ns)
    bi = jnp.arange(bsz)[:, None, None, None, None]
    hi = jnp.arange(n_kv)[None, None, :, None, None]
    offs = jnp.arange(MOBA_BLOCK, dtype=jnp.int32)
    scale = hd ** -0.5

    def attend_chunk(args):
        qx, ix, px, ox = args
        kg = k_blk[bi, hi, ix]
        vg = v_blk[bi, hi, ix]
        logits = jnp.einsum('bqkgd,bqkgsnd->bqkgsn', qx, kg, preferred_element_type=jnp.float32) * scale
        kpos = ix[..., None] * MOBA_BLOCK + offs
        ok = ox[None, :, None, None, :, None] & (kpos <= px[None, :, None, None, None, None])
        logits = jnp.where(ok, logits, -jnp.inf)
        sh = logits.shape
        prob = jax.nn.softmax(logits.reshape(sh[0], sh[1], sh[2], sh[3], -1), axis=-1).reshape(sh)
        return jnp.einsum('bqkgsn,bqkgsnd->bqkgd', prob, vg.astype(jnp.float32)).astype(q.dtype)

    outs = lax.map(attend_chunk, (q_ch, i_ch, p_ch, o_ch))
    return jnp.moveaxis(outs, 0, 1).reshape(bsz, s_len, n_h * hd)


def odd_mixer(h, k_past, v_past, q_start, w_qkv, w_o):
    b, L, _ = h.shape
    qkv = h @ w_qkv
    q, k, v = jnp.split(qkv, [ATT_HEADS * ATT_HEAD_DIM, (ATT_HEADS + ATT_KV_HEADS) * ATT_HEAD_DIM], axis=-1)
    q = q.reshape(b, L, ATT_HEADS, ATT_HEAD_DIM)
    k = k.reshape(b, L, ATT_KV_HEADS, ATT_HEAD_DIM)
    v = v.reshape(b, L, ATT_KV_HEADS, ATT_HEAD_DIM)
    pos = q_start + jnp.arange(L, dtype=jnp.int32)
    q = partial_rope(q, pos)
    k = partial_rope(k, pos)
    if k_past is None:
        k_all, v_all = k, v
    else:
        k_all = jnp.concatenate([k_past.astype(k.dtype), k], axis=1)
        v_all = jnp.concatenate([v_past.astype(v.dtype), v], axis=1)
    o = moba_attention(q, k_all, v_all, q_start)
    return o @ w_o, k, v


def sq_relu_mlp(h, w_up, w_down):
    a = jax.nn.relu(h @ w_up)
    return (a * a) @ w_down


def setup_inputs(seed: int = 0) -> dict:
    key = jax.random.key(seed)
    ks = list(jax.random.split(key, 40))

    def nrm(i, shape, s=1.0):
        return jax.random.normal(ks[i], shape, jnp.float32) * s

    n_even = (DEPTH + 1) // 2
    n_odd = DEPTH // 2
    n_pages = PAST_LEN // PAGE_SIZE
    n_used = DEC_BATCH * n_pages
    n_pool = n_used + max(1, n_used // 4)
    page_table = jax.random.permutation(ks[0], n_pool)[:n_used].reshape(DEC_BATCH, n_pages).astype(jnp.int32)
    dt0 = jnp.exp(jax.random.uniform(ks[1], (n_even, SSD_HEADS), jnp.float32, math.log(1e-3), math.log(1e-1)))
    dt_bias = dt0 + jnp.log(-jnp.expm1(-dt0))
    a_log = jnp.log(jax.random.uniform(ks[2], (n_even, SSD_HEADS), jnp.float32, 1.0, 16.0))
    return {
        'x_prompt': nrm(3, (BATCH, SEQ, D_MODEL)),
        'x_sample': nrm(4, (DEC_BATCH, DEC_SEQ, D_MODEL)),
        'state_sc_conv': nrm(5, (n_even, DEC_BATCH, SC_KERNEL - 1, SC_WIDTH)),
        'state_ssd_conv': nrm(6, (n_even, DEC_BATCH, SSD_CONV - 1, SSD_CONV_DIM)),
        'state_ssm': nrm(7, (n_even, DEC_BATCH, SSD_HEADS, SSD_HEAD_DIM, SSD_STATE), 0.5),
        'cache_k': nrm(8, (n_odd, n_pool, PAGE_SIZE, ATT_KV_HEADS, ATT_HEAD_DIM)),
        'cache_v': nrm(9, (n_odd, n_pool, PAGE_SIZE, ATT_KV_HEADS, ATT_HEAD_DIM)),
        'page_table': page_table,
        'c_prompt': nrm(10, (BATCH, D_MODEL)),
        'c_sample': nrm(11, (DEC_BATCH, D_MODEL)),
        'norm_mix_w': 1.0 + nrm(12, (DEPTH, D_MODEL), 0.02),
        'norm_mlp_w': 1.0 + nrm(13, (DEPTH, D_MODEL), 0.02),
        'norm_final_w': 1.0 + nrm(14, (D_MODEL,), 0.02),
        'w_ada': nrm(15, (DEPTH, D_MODEL, 6 * D_MODEL), 0.5 * D_MODEL ** -0.5),
        'b_ada': nrm(16, (DEPTH, 6 * D_MODEL), 0.02),
        'w_in_e': nrm(17, (n_even, D_MODEL, IN_E_DIM), D_MODEL ** -0.5),
        'sc_conv_w': nrm(18, (n_even, SC_KERNEL, SC_WIDTH), SC_KERNEL ** -0.5),
        'ssd_conv_w': nrm(19, (n_even, SSD_CONV, SSD_CONV_DIM), SSD_CONV ** -0.5),
        'ssd_conv_b': nrm(20, (n_even, SSD_CONV_DIM), 0.02),
        'ssd_dt_bias': dt_bias,
        'ssd_a_log': a_log,
        'ssd_d': 1.0 + nrm(21, (n_even, SSD_HEADS), 0.1),
        'ssd_norm_w': 1.0 + nrm(22, (n_even, SSD_INNER), 0.02),
        'w_out_e': nrm(23, (n_even, SC_WIDTH + SSD_INNER, D_MODEL), (SC_WIDTH + SSD_INNER) ** -0.5),
        'w_qkv': nrm(24, (n_odd, D_MODEL, QKV_DIM), D_MODEL ** -0.5),
        'w_o': nrm(25, (n_odd, ATT_HEADS * ATT_HEAD_DIM, D_MODEL), (ATT_HEADS * ATT_HEAD_DIM) ** -0.5),
        'w_up': nrm(26, (DEPTH, D_MODEL, D_FF), D_MODEL ** -0.5),
        'w_down': nrm(27, (DEPTH, D_FF, D_MODEL), D_FF ** -0.5),
    }


def reference(x_prompt, x_sample, state_sc_conv, state_ssd_conv, state_ssm, cache_k, cache_v, page_table,
              c_prompt, c_sample, norm_mix_w, norm_mlp_w, norm_final_w, w_ada, b_ada, w_in_e, sc_conv_w,
              ssd_conv_w, ssd_conv_b, ssd_dt_bias, ssd_a_log, ssd_d, ssd_norm_w, w_out_e, w_qkv, w_o, w_up, w_down):
    bp = x_prompt.shape[0]
    bs = x_sample.shape[0]
    past_len = page_table.shape[1] * cache_k.shape[2]
    sc_p, sc_s, cv_p, cv_s, h_p, h_s = [], [], [], [], [], []
    k_p, k_s, v_p, v_s = [], [], [], []
    xp, xs = x_prompt, x_sample
    for layer in range(DEPTH):
        i = layer // 2
        mp = adaln(c_prompt, w_ada[layer], b_ada[layer])
        ms = adaln(c_sample, w_ada[layer], b_ada[layer])
        hp = modulate(rms_norm(xp, norm_mix_w[layer]), mp[0], mp[1])
        hs = modulate(rms_norm(xs, norm_mix_w[layer]), ms[0], ms[1])
        if layer % 2 == 0:
            ew = (w_in_e[i], sc_conv_w[i], ssd_conv_w[i], ssd_conv_b[i], ssd_dt_bias[i], ssd_a_log[i],
                  ssd_d[i], ssd_norm_w[i], w_out_e[i])
            zero_sc = jnp.zeros((bp, SC_KERNEL - 1, SC_WIDTH), xp.dtype)
            zero_cv = jnp.zeros((bp, SSD_CONV - 1, SSD_CONV_DIM), xp.dtype)
            zero_h = jnp.zeros((bp, SSD_HEADS, SSD_HEAD_DIM, SSD_STATE), jnp.float32)
            op, a1, a2, a3 = even_mixer(hp, zero_sc, zero_cv, zero_h, *ew)
            osm, b1, b2, b3 = even_mixer(hs, state_sc_conv[i], state_ssd_conv[i], state_ssm[i], *ew)
            sc_p.append(a1); cv_p.append(a2); h_p.append(a3)
            sc_s.append(b1); cv_s.append(b2); h_s.append(b3)
        else:
            k_past = cache_k[i][page_table].reshape(bs, past_len, ATT_KV_HEADS, ATT_HEAD_DIM)
            v_past = cache_v[i][page_table].reshape(bs, past_len, ATT_KV_HEADS, ATT_HEAD_DIM)
            op, kp, vp = odd_mixer(hp, None, None, 0, w_qkv[i], w_o[i])
            osm, ksm, vsm = odd_mixer(hs, k_past, v_past, past_len, w_qkv[i], w_o[i])
            k_p.append(kp); v_p.append(vp); k_s.append(ksm); v_s.append(vsm)
        xp = xp + mp[2] * op.astype(xp.dtype)
        xs = xs + ms[2] * osm.astype(xs.dtype)
        hp = modulate(rms_norm(xp, norm_mlp_w[layer]), mp[3], mp[4])
        hs = modulate(rms_norm(xs, norm_mlp_w[layer]), ms[3], ms[4])
        xp = xp + mp[5] * sq_relu_mlp(hp, w_up[layer], w_down[layer])
        xs = xs + ms[5] * sq_relu_mlp(hs, w_up[layer], w_down[layer])
    y_prompt = rms_norm(xp, norm_final_w)
    y_sample = rms_norm(xs, norm_final_w)
    return (y_prompt, y_sample, jnp.stack(sc_p), jnp.stack(sc_s), jnp.stack(cv_p), jnp.stack(cv_s),
            jnp.stack(h_p), jnp.stack(h_s), jnp.stack(k_p), jnp.stack(k_s), jnp.stack(v_p), jnp.stack(v_s))
```

```python
import functools
import math

import jax
import jax.numpy as jnp
from jax import lax
from jax.experimental import pallas as pl
from jax.experimental.pallas import tpu as pltpu

F32 = jnp.float32
BF16 = jnp.bfloat16
HIGHEST = lax.Precision.HIGHEST

EPS = 1e-6
LANE = 128
SUBLANE = 8
SSD_HEAD_DIM = 64
SSD_GROUPS = 4
SSD_STATE = 128
SSD_CHUNK = 256
ATT_HEADS = 16
ATT_KV_HEADS = 4
ATT_GROUP = ATT_HEADS // ATT_KV_HEADS
ATT_HEAD_DIM = 64
ROT_DIM = ATT_HEAD_DIM // 4
ROPE_THETA = 500000.0
MOBA_BLOCK = 256
MOBA_TOPK = 3
SEL_LANES = 16
NEG_BIG = -1e30
VMEM_LIMIT = 56 * 1024 * 1024

NT_DIMS = (((1,), (1,)), ((), ()))


def _cparams(*sem):
    return pltpu.CompilerParams(dimension_semantics=sem, vmem_limit_bytes=VMEM_LIMIT)


def _silu(x):
    return x * jax.nn.sigmoid(x)


def _softplus(x):
    return jnp.maximum(x, 0.0) + jnp.log1p(jnp.exp(-jnp.abs(x)))


def _norm_mod(x, nw, shift, scale):
    ms = jnp.mean(x * x, axis=-1, keepdims=True)
    xn = x * lax.rsqrt(ms + EPS) * nw
    return xn * (1.0 + scale) + shift


def _resident(shape):
    nd = len(shape)
    return pl.BlockSpec(shape, lambda *_: (0,) * nd, pipeline_mode=pl.Buffered(1))


def _ada_kernel(c_ref, w_ref, b_ref, o_ref):
    s = _silu(c_ref[...]).astype(BF16)
    o_ref[0] = jnp.dot(s, w_ref[0].astype(BF16), preferred_element_type=F32) + b_ref[0]


def _ada(c_all, w_ada, b_ada):
    rows, d = c_all.shape
    n_layers, _, n6 = w_ada.shape
    tn = 1024
    return pl.pallas_call(
        _ada_kernel,
        grid=(n_layers, n6 // tn),
        in_specs=[pl.BlockSpec((rows, d), lambda l, j: (0, 0)),
                  pl.BlockSpec((1, d, tn), lambda l, j: (l, 0, j)),
                  pl.BlockSpec((1, 1, tn), lambda l, j: (l, 0, j))],
        out_specs=pl.BlockSpec((1, rows, tn), lambda l, j: (l, 0, j)),
        out_shape=jax.ShapeDtypeStruct((n_layers, rows, n6), F32),
        compiler_params=_cparams("parallel", "parallel"),
        name="adaln",
    )(c_all, w_ada, b_ada.reshape(n_layers, 1, n6))


def _mod_spec(mod, k, rows_per_group, tm):
    _, r, n6 = mod.shape
    d = n6 // 6
    tiles_per_group = max(rows_per_group // tm, 1)
    return pl.BlockSpec((1, r, d), lambda i, *_: (i // tiles_per_group, 0, k))


def _nm_matmul_kernel(x_ref, nw_ref, sh_ref, sc_ref, w_ref, w2_ref, o_ref, o2_ref, h_scr):
    @pl.when(pl.program_id(1) == 0)
    def _():
        hb = _norm_mod(x_ref[...], nw_ref[...], sh_ref[0], sc_ref[0]).astype(BF16)
        h_scr[...] = hb
        o2_ref[...] = jnp.dot(hb, w2_ref[...], preferred_element_type=F32)

    o_ref[...] = jnp.dot(h_scr[...], w_ref[...], preferred_element_type=F32)


def _nm_matmul(x, nw, mod, k_shift, k_scale, w, w2, rows_per_group, tm, tn):
    t, d = x.shape
    n = w.shape[1]
    n2 = w2.shape[1]
    return pl.pallas_call(
        _nm_matmul_kernel,
        grid=(t // tm, n // tn),
        in_specs=[pl.BlockSpec((tm, d), lambda i, j: (i, 0)),
                  pl.BlockSpec((1, d), lambda i, j: (0, 0)),
                  _mod_spec(mod, k_shift, rows_per_group, tm),
                  _mod_spec(mod, k_scale, rows_per_group, tm),
                  pl.BlockSpec((d, tn), lambda i, j: (0, j)),
                  pl.BlockSpec((d, n2), lambda i, j: (0, 0))],
        out_specs=[pl.BlockSpec((tm, tn), lambda i, j: (i, j)),
                   pl.BlockSpec((tm, n2), lambda i, j: (i, 0))],
        out_shape=[jax.ShapeDtypeStruct((t, n), F32), jax.ShapeDtypeStruct((t, n2), F32)],
        scratch_shapes=[pltpu.VMEM((tm, d), BF16)],
        compiler_params=_cparams("parallel", "arbitrary"),
        name="in_proj",
    )(x, nw, mod, mod, w, w2)


def _mm_res_kernel(a_ref, w_ref, res_ref, g_ref, o_ref):
    o_ref[...] = res_ref[...] + g_ref[0] * jnp.dot(a_ref[...], w_ref[...], preferred_element_type=F32)


def _mm_res(a, w, res, mod, k_gate, rows_per_group, tm):
    t, kdim = a.shape
    d = w.shape[1]
    return pl.pallas_call(
        _mm_res_kernel,
        grid=(t // tm,),
        in_specs=[pl.BlockSpec((tm, kdim), lambda i: (i, 0)),
                  _resident((kdim, d)),
                  pl.BlockSpec((tm, d), lambda i: (i, 0)),
                  _mod_spec(mod, k_gate, rows_per_group, tm)],
        out_specs=pl.BlockSpec((tm, d), lambda i: (i, 0)),
        out_shape=jax.ShapeDtypeStruct((t, d), F32),
        compiler_params=_cparams("parallel"),
        name="proj_residual",
    )(a, w, res, mod)


def _mlp_kernel(x_ref, nw_ref, sh_ref, sc_ref, g_ref, wu_ref, wd_ref, fw_ref, o_ref, *, final_norm, tf):
    x = x_ref[...]
    hb = _norm_mod(x, nw_ref[...], sh_ref[0], sc_ref[0]).astype(BF16)
    acc = jnp.zeros(x.shape, F32)
    for f in range(wu_ref.shape[1] // tf):
        a = jnp.maximum(jnp.dot(hb, wu_ref[:, f * tf:(f + 1) * tf], preferred_element_type=F32), 0.0)
        acc = acc + jnp.dot((a * a).astype(BF16), wd_ref[f * tf:(f + 1) * tf, :], preferred_element_type=F32)
    y = x + g_ref[0] * acc
    if final_norm:
        ms = jnp.mean(y * y, axis=-1, keepdims=True)
        y = y * lax.rsqrt(ms + EPS) * fw_ref[...]
    o_ref[...] = y


def _mlp(x, nw, mod, w_up, w_down, fw, rows_per_group, tm, final_norm):
    t, d = x.shape
    dff = w_up.shape[1]
    return pl.pallas_call(
        functools.partial(_mlp_kernel, final_norm=final_norm, tf=1024),
        grid=(t // tm,),
        in_specs=[pl.BlockSpec((tm, d), lambda i: (i, 0)),
                  pl.BlockSpec((1, d), lambda i: (0, 0)),
                  _mod_spec(mod, 3, rows_per_group, tm),
                  _mod_spec(mod, 4, rows_per_group, tm),
                  _mod_spec(mod, 5, rows_per_group, tm),
                  _resident((d, dff)),
                  _resident((dff, d)),
                  pl.BlockSpec((1, d), lambda i: (0, 0))],
        out_specs=pl.BlockSpec((tm, d), lambda i: (i, 0)),
        out_shape=jax.ShapeDtypeStruct((t, d), F32),
        compiler_params=_cparams("parallel"),
        name="mlp",
    )(x, nw, mod, mod, mod, w_up, w_down, fw)


def _shift_rows(cur, tail, k):
    rolled = pltpu.roll(cur, k, axis=0)
    row = lax.broadcasted_iota(jnp.int32, tail.shape, 0)
    top = jnp.where(row < k, pltpu.roll(tail, k, axis=0), rolled[:SUBLANE])
    return jnp.concatenate([top, rolled[SUBLANE:]], axis=0)


def _causal_conv(cur, tail, w):
    kk = w.shape[0]
    out = cur * w[kk - 1:kk, :]
    for s in range(1, kk):
        out = out + _shift_rows(cur, tail, s) * w[kk - 1 - s:kk - s, :]
    return out


def _expand_heads(m, h0, n_cols):
    rows = m.shape[0]
    lane = lax.broadcasted_iota(jnp.int32, (rows, LANE), 1)
    cols = []
    for c in range(n_cols):
        lo = jnp.broadcast_to(m[:, h0 + 2 * c:h0 + 2 * c + 1], (rows, LANE))
        hi = jnp.broadcast_to(m[:, h0 + 2 * c + 1:h0 + 2 * c + 2], (rows, LANE))
        cols.append(jnp.where(lane < SSD_HEAD_DIM, lo, hi))
    return jnp.concatenate(cols, axis=1)


def _ssd_prompt_kernel(scb_ref, scc_ref, scx_ref, z0_ref, z1_ref, x0_ref, x1_ref, bc_ref, dt_ref,
                       scw_ref, cw_ref, cb_ref, dtb_ref, alog_ref, dexp_ref, nw_ref,
                       y_ref, scst_ref, cvst_ref, ssm_ref,
                       utail, xtail, s_scr):
    c = pl.program_id(1)
    q = scb_ref.shape[0]
    dsc = scb_ref.shape[1]
    gw = 2 * x0_ref.shape[1] // SSD_GROUPS
    hpg = gw // SSD_HEAD_DIM

    @pl.when(c == 0)
    def _():
        utail[...] = jnp.zeros(utail.shape, F32)
        xtail[...] = jnp.zeros(xtail.shape, F32)
        s_scr[...] = jnp.zeros(s_scr.shape, F32)

    u = scc_ref[...] * scx_ref[...]
    y_sc = scb_ref[...] * _causal_conv(u, utail[...], scw_ref[...])
    utail[...] = u[q - SUBLANE:, :]
    y_ref[:, :dsc] = y_sc.astype(BF16)

    bc = bc_ref[...]
    n_bc = bc.shape[1]
    o_bc = 2 * x0_ref.shape[1]
    bc_c = _silu(_causal_conv(bc, xtail[:, o_bc:o_bc + n_bc], cw_ref[:, o_bc:o_bc + n_bc]) + cb_ref[:, o_bc:o_bc + n_bc])
    xtail[:, o_bc:o_bc + n_bc] = bc[q - SUBLANE:, :]
    half = n_bc // 2

    dt = _softplus(dt_ref[...] + dtb_ref[...])
    a = dt * (-jnp.exp(alog_ref[...]))
    ri = lax.broadcasted_iota(jnp.int32, (q, q), 0)
    ci = lax.broadcasted_iota(jnp.int32, (q, q), 1)
    tril = ri >= ci
    a_cs = jnp.dot(tril.astype(F32), a, precision=HIGHEST, preferred_element_type=F32)
    a_cs_t = a_cs.T
    quad = 4 * SSD_HEAD_DIM
    lane_head = lax.broadcasted_iota(jnp.int32, (q, quad), 1) // SSD_HEAD_DIM

    for g in range(SSD_GROUPS):
        x_ref = x0_ref if g < SSD_GROUPS // 2 else x1_ref
        z_ref = z0_ref if g < SSD_GROUPS // 2 else z1_ref
        lo = (g % (SSD_GROUPS // 2)) * gw
        go = g * gw
        xg = x_ref[:, lo:lo + gw]
        xs = _silu(_causal_conv(xg, xtail[:, go:go + gw], cw_ref[:, go:go + gw]) + cb_ref[:, go:go + gw])
        xtail[:, go:go + gw] = xg[q - SUBLANE:, :]

        n_cols = gw // LANE
        dt_x = _expand_heads(dt, g * hpg, n_cols)
        a_x = _expand_heads(a_cs, g * hpg, n_cols)
        ea = jnp.exp(a_x)
        dte = jnp.exp(a_x[q - 1:q, :] - a_x)
        xdt = xs * dt_x
        xdt_b = xdt.astype(BF16)

        bg = bc_c[:, g * SSD_STATE:(g + 1) * SSD_STATE]
        cg_b = bc_c[:, half + g * SSD_STATE:half + (g + 1) * SSD_STATE].astype(BF16)
        cb_mat = lax.dot_general(cg_b, bg.astype(BF16), NT_DIMS, preferred_element_type=F32)
        s_g = s_scr[:, go:go + gw]
        y_off = jnp.dot(cg_b, s_g.astype(BF16), preferred_element_type=F32) * ea

        y_quads = []
        for qd in range(gw // quad):
            xq = xdt[:, qd * quad:(qd + 1) * quad]
            acc = None
            for i in range(4):
                h = g * hpg + qd * 4 + i
                seg = a_cs[:, h:h + 1] - a_cs_t[h:h + 1, :]
                dec = jnp.exp(jnp.where(tril, seg, -jnp.inf))
                w_h = (cb_mat * dec).astype(BF16)
                xm = jnp.where(lane_head == i, xq, 0.0).astype(BF16)
                d = jnp.dot(w_h, xm, preferred_element_type=F32)
                acc = d if acc is None else acc + d
            y_quads.append(acc)
        y = jnp.concatenate(y_quads, axis=1) + y_off + dexp_ref[:, go:go + gw] * xs

        zg = z_ref[:, lo:lo + gw]
        y = y * _silu(zg)
        ms = jnp.mean(y * y, axis=-1, keepdims=True)
        y = y * lax.rsqrt(ms + EPS) * nw_ref[:, go:go + gw]
        y_ref[:, dsc + go:dsc + go + gw] = y.astype(BF16)

        s_new = s_g * ea[q - 1:q, :] + jnp.dot(bg.T.astype(BF16), (xdt * dte).astype(BF16),
                                               preferred_element_type=F32)
        s_scr[:, go:go + gw] = s_new

    @pl.when(c == pl.num_programs(1) - 1)
    def _():
        scst_ref[0] = utail[...]
        cvst_ref[0] = xtail[...]
        ssm_ref[0] = s_scr[...].T


def _ssd_prompt(proj, dt_raw, scw, cw, cb, dtb, alog, dexp, nw, bp, seq):
    q = min(SSD_CHUNK, seq)
    nc = seq // q
    dsc = scw.shape[1]
    d_inner = dexp.shape[1]
    conv_dim = cw.shape[1]
    bw = dsc

    def col(kb):
        return pl.BlockSpec((q, bw), lambda b, c: (b * nc + c, kb))

    n_z = d_inner // bw
    specs = [col(0), col(1), col(2)]
    specs += [col(3 + i) for i in range(n_z)]
    specs += [col(3 + n_z + i) for i in range(n_z)]
    specs += [col(3 + 2 * n_z)]
    specs += [pl.BlockSpec((q, LANE), lambda b, c: (b * nc + c, 0))]
    for arr in (scw, cw, cb, dtb, alog, dexp, nw):
        specs.append(pl.BlockSpec(arr.shape, lambda b, c: (0, 0)))
    return pl.pallas_call(
        _ssd_prompt_kernel,
        grid=(bp, nc),
        in_specs=specs,
        out_specs=[pl.BlockSpec((q, dsc + d_inner), lambda b, c: (b * nc + c, 0)),
                   pl.BlockSpec((1, SUBLANE, dsc), lambda b, c: (b, 0, 0)),
                   pl.BlockSpec((1, SUBLANE, conv_dim), lambda b, c: (b, 0, 0)),
                   pl.BlockSpec((1, d_inner, SSD_STATE), lambda b, c: (b, 0, 0))],
        out_shape=[jax.ShapeDtypeStruct((bp * seq, dsc + d_inner), BF16),
                   jax.ShapeDtypeStruct((bp, SUBLANE, dsc), F32),
                   jax.ShapeDtypeStruct((bp, SUBLANE, conv_dim), F32),
                   jax.ShapeDtypeStruct((bp, d_inner, SSD_STATE), F32)],
        scratch_shapes=[pltpu.VMEM((SUBLANE, dsc), F32),
                        pltpu.VMEM((SUBLANE, conv_dim), F32),
                        pltpu.VMEM((SSD_STATE, d_inner), F32)],
        compiler_params=_cparams("parallel", "arbitrary"),
        name="ssd_prompt",
    )(*([proj] * (4 + 2 * n_z)), dt_raw, scw, cw, cb, dtb, alog, dexp, nw)


def _even_sample_prep_kernel(proj_ref, dt_ref, scst_ref, cvst_ref, scw_ref, cw_ref, cb_ref, dtb_ref, alog_ref,
                             ysc_ref, scn_ref, cvn_ref, xs_ref, b_ref, c_ref, xdt_t_ref, dec_t_ref):
    dsc = scw_ref.shape[1]
    conv_dim = cw_ref.shape[1]
    d_inner = xs_ref.shape[1]
    n_heads = d_inner // SSD_HEAD_DIM
    scb = proj_ref[:, :dsc]
    u = proj_ref[:, dsc:2 * dsc] * proj_ref[:, 2 * dsc:3 * dsc]
    xbc = proj_ref[:, 3 * dsc + d_inner:3 * dsc + d_inner + conv_dim]
    b0 = scst_ref[:, :dsc]
    b1 = scst_ref[:, dsc:]
    ysc_ref[...] = scb * (b0 * scw_ref[0:1, :] + b1 * scw_ref[1:2, :] + u * scw_ref[2:3, :])
    scn_ref[:, :dsc] = b1
    scn_ref[:, dsc:] = u

    c0 = cvst_ref[:, :conv_dim]
    c1 = cvst_ref[:, conv_dim:2 * conv_dim]
    c2 = cvst_ref[:, 2 * conv_dim:]
    act = _silu(c0 * cw_ref[0:1, :] + c1 * cw_ref[1:2, :] + c2 * cw_ref[2:3, :] + xbc * cw_ref[3:4, :] + cb_ref[...])
    cvn_ref[:, :conv_dim] = c1
    cvn_ref[:, conv_dim:2 * conv_dim] = c2
    cvn_ref[:, 2 * conv_dim:] = xbc
    xs = act[:, :d_inner]
    n_bc = (conv_dim - d_inner) // 2
    xs_ref[...] = xs
    b_ref[...] = act[:, d_inner:d_inner + n_bc]
    c_ref[...] = act[:, d_inner + n_bc:]

    dt = _softplus(dt_ref[...] + dtb_ref[...])
    dec = jnp.exp(dt * (-jnp.exp(alog_ref[...])))
    n_cols = n_heads // 2
    xdt_t_ref[...] = (xs * _expand_heads(dt, 0, n_cols)).T
    dec_t_ref[...] = _expand_heads(dec, 0, n_cols).T


def _even_sample_prep(proj, dt_raw, scst, cvst, scw, cw, cb, dtb, alog, d_inner):
    bs = proj.shape[0]
    dsc = scw.shape[1]
    conv_dim = cw.shape[1]
    n_bc = (conv_dim - d_inner) // 2
    shapes = [(bs, dsc), (bs, 2 * dsc), (bs, 3 * conv_dim), (bs, d_inner), (bs, n_bc), (bs, n_bc),
              (d_inner, bs), (d_inner, bs)]
    return pl.pallas_call(
        _even_sample_prep_kernel,
        out_shape=[jax.ShapeDtypeStruct(s, F32) for s in shapes],
        compiler_params=pltpu.CompilerParams(vmem_limit_bytes=VMEM_LIMIT),
        name="even_sample_prep",
    )(proj, dt_raw, scst, cvst, scw, cw, cb, dtb, alog)


def _ssm_sample_kernel(h0_ref, xt_ref, dt_ref, b_ref, c_ref, hn_ref, yt_ref):
    bs = h0_ref.shape[0]
    p = h0_ref.shape[2]
    n = h0_ref.shape[3]
    lane = lax.broadcasted_iota(jnp.int32, (p, bs), 1)
    xt = xt_ref[...]
    dct = dt_ref[...]
    yt = jnp.zeros((p, bs), F32)
    for b in range(bs):
        xcol = jnp.broadcast_to(xt[:, b:b + 1], (p, n))
        dcol = jnp.broadcast_to(dct[:, b:b + 1], (p, n))
        hn = h0_ref[b, 0] * dcol + xcol * b_ref[b:b + 1, :]
        hn_ref[b, 0] = hn
        ycol = jnp.sum(hn * c_ref[b:b + 1, :], axis=1, keepdims=True)
        yt = jnp.where(lane == b, ycol, yt)
    yt_ref[...] = yt


def _ssm_sample(h0, xdt_t, dec_t, bm, cm):
    bs, n_heads, p, n = h0.shape
    hpg = n_heads // SSD_GROUPS
    return pl.pallas_call(
        _ssm_sample_kernel,
        grid=(n_heads,),
        in_specs=[pl.BlockSpec((bs, 1, p, n), lambda h: (0, h, 0, 0)),
                  pl.BlockSpec((p, bs), lambda h: (h, 0)),
                  pl.BlockSpec((p, bs), lambda h: (h, 0)),
                  pl.BlockSpec((bs, n), lambda h: (0, h // hpg)),
                  pl.BlockSpec((bs, n), lambda h: (0, h // hpg))],
        out_specs=[pl.BlockSpec((bs, 1, p, n), lambda h: (0, h, 0, 0)),
                   pl.BlockSpec((p, bs), lambda h: (h, 0))],
        out_shape=[jax.ShapeDtypeStruct(h0.shape, F32), jax.ShapeDtypeStruct((n_heads * p, bs), F32)],
        compiler_params=_cparams("parallel"),
        name="ssm_sample",
    )(h0, xdt_t, dec_t, bm, cm)


def _even_out_sample_kernel(yt_ref, xs_ref, proj_ref, ysc_ref, dexp_ref, nw_ref, w_ref, x_ref, g_ref, o_ref):
    d_inner = xs_ref.shape[1]
    dsc = ysc_ref.shape[1]
    gw = d_inner // SSD_GROUPS
    y = yt_ref[...].T + dexp_ref[...] * xs_ref[...]
    y = y * _silu(proj_ref[:, 3 * dsc:3 * dsc + d_inner])
    parts = []
    for g in range(SSD_GROUPS):
        yg = y[:, g * gw:(g + 1) * gw]
        ms = jnp.mean(yg * yg, axis=-1, keepdims=True)
        parts.append(yg * lax.rsqrt(ms + EPS))
    y_ssd = jnp.concatenate(parts, axis=1) * nw_ref[...]
    cat = jnp.concatenate([ysc_ref[...], y_ssd], axis=1).astype(BF16)
    o_ref[...] = x_ref[...] + g_ref[0] * jnp.dot(cat, w_ref[...], preferred_element_type=F32)


def _even_out_sample(y_t, xs, proj, ysc, dexp, nw, w_out, x, mod):
    bs, d = x.shape
    full = lambda a: pl.BlockSpec(a.shape, lambda i: (0,) * a.ndim)
    return pl.pallas_call(
        _even_out_sample_kernel,
        grid=(1,),
        in_specs=[full(y_t), full(xs), full(proj), full(ysc), full(dexp), full(nw), full(w_out), full(x),
                  pl.BlockSpec((1, bs, d), lambda i: (0, 0, 2))],
        out_specs=pl.BlockSpec((bs, d), lambda i: (0, 0)),
        out_shape=jax.ShapeDtypeStruct((bs, d), F32),
        compiler_params=_cparams("arbitrary"),
        name="even_out_sample",
    )(y_t, xs, proj, ysc, dexp, nw, w_out, x, mod)


def _rope_table_kernel(ca_ref, cb_ref, cc_ref, *, pos0, stride):
    rows = ca_ref.shape[0]
    row = lax.broadcasted_iota(jnp.int32, (rows, LANE), 0) + pl.program_id(0) * rows
    d = lax.broadcasted_iota(jnp.int32, (rows, LANE), 1) % ATT_HEAD_DIM
    half = ROT_DIM // 2
    pos = (pos0 + row * stride).astype(F32)
    inv = jnp.exp((d % half).astype(F32) * (-2.0 / ROT_DIM * math.log(ROPE_THETA)))
    ang = pos * inv
    cs = jnp.cos(ang)
    sn = jnp.sin(ang)
    ca_ref[...] = jnp.where(d < ROT_DIM, cs, 1.0)
    cb_ref[...] = jnp.where(d < half, -sn, 0.0)
    cc_ref[...] = jnp.where(d < half, 0.0, jnp.where(d < ROT_DIM, sn, 0.0))


def _rope_tables(rows, pos0, stride):
    tr = min(rows, 512)
    spec = pl.BlockSpec((tr, LANE), lambda i: (i, 0))
    return pl.pallas_call(
        functools.partial(_rope_table_kernel, pos0=pos0, stride=stride),
        grid=(rows // tr,),
        out_specs=[spec, spec, spec],
        out_shape=[jax.ShapeDtypeStruct((rows, LANE), F32)] * 3,
        compiler_params=_cparams("parallel"),
        name="rope_tables",
    )()


def _rope(t, ca, cb, cc):
    cols = []
    for c in range(t.shape[1] // LANE):
        s = t[:, c * LANE:(c + 1) * LANE]
        cols.append(s * ca + pltpu.roll(s, LANE - ROT_DIM // 2, axis=1) * cb + pltpu.roll(s, ROT_DIM // 2, axis=1) * cc)
    return jnp.concatenate(cols, axis=1)


def _qkv_kernel(x_ref, nw_ref, sh_ref, sc_ref, w_ref, ca_ref, cb_ref, cc_ref, q_ref, k_ref, v_ref, km_ref):
    nq = q_ref.shape[1]
    nk = k_ref.shape[1]
    hb = _norm_mod(x_ref[...], nw_ref[...], sh_ref[0], sc_ref[0]).astype(BF16)
    qkv = jnp.dot(hb, w_ref[...], preferred_element_type=F32)
    ca, cb, cc = ca_ref[...], cb_ref[...], cc_ref[...]
    q_ref[...] = _rope(qkv[:, :nq], ca, cb, cc)
    k = _rope(qkv[:, nq:nq + nk], ca, cb, cc)
    k_ref[...] = k
    v_ref[...] = qkv[:, nq + nk:]
    km_ref[0] = jnp.mean(k, axis=0, keepdims=True)


def _qkv(x, nw, mod, w, tabs, rows_per_group, tm):
    t, d = x.shape
    nq = ATT_HEADS * ATT_HEAD_DIM
    nk = ATT_KV_HEADS * ATT_HEAD_DIM
    tab_tiles = tabs[0].shape[0] // tm
    tab = pl.BlockSpec((tm, LANE), lambda i: (i % tab_tiles, 0))
    row = lambda n: pl.BlockSpec((tm, n), lambda i: (i, 0))
    return pl.pallas_call(
        _qkv_kernel,
        grid=(t // tm,),
        in_specs=[row(d), pl.BlockSpec((1, d), lambda i: (0, 0)),
                  _mod_spec(mod, 0, rows_per_group, tm), _mod_spec(mod, 1, rows_per_group, tm),
                  _resident(w.shape), tab, tab, tab],
        out_specs=[row(nq), row(nk), row(nk), pl.BlockSpec((1, 1, nk), lambda i: (i, 0, 0))],
        out_shape=[jax.ShapeDtypeStruct((t, nq), F32), jax.ShapeDtypeStruct((t, nk), F32),
                   jax.ShapeDtypeStruct((t, nk), F32), jax.ShapeDtypeStruct((t // tm, 1, nk), F32)],
        compiler_params=_cparams("parallel"),
        name="qkv_rope",
    )(x, nw, mod, mod, w, *tabs)


def _select_bias(scores, n_past, n_rounds):
    lane = lax.broadcasted_iota(jnp.int32, scores.shape, 1)
    lanef = lane.astype(F32)
    s = jnp.where(lane < n_past, scores, -jnp.inf)
    bias = jnp.full(scores.shape, NEG_BIG, F32)
    for r in range(n_rounds):
        mx = jnp.max(s, axis=1, keepdims=True)
        idx = jnp.min(jnp.where(s == mx, lanef, float(LANE)), axis=1, keepdims=True)
        bias = jnp.where(lanef == jnp.where(n_past > r, idx, -1.0), 0.0, bias)
        s = jnp.where(lanef == idx, -jnp.inf, s)
    return bias


def _moba_prompt_kernel(q_ref, k_ref, v_ref, km_ref, o_ref, qm_scr, qa_scr, m_scr, l_scr, acc_scr):
    qi = pl.program_id(1)
    blk = q_ref.shape[0]
    kvw = k_ref.shape[1]
    rows = ATT_GROUP * blk
    scale = ATT_HEAD_DIM ** -0.5
    lane_h = lax.broadcasted_iota(jnp.int32, (blk, kvw), 1) // ATT_HEAD_DIM
    lane_r = lax.broadcasted_iota(jnp.int32, (rows, LANE), 1)
    kmean = km_ref[0]

    for h in range(ATT_KV_HEADS):
        qm = jnp.concatenate([jnp.where(lane_h == h, q_ref[:, g * kvw:(g + 1) * kvw], 0.0)
                              for g in range(ATT_GROUP)], axis=0)
        scores = lax.dot_general(qm, kmean, NT_DIMS, precision=HIGHEST, preferred_element_type=F32)
        bias = _select_bias(scores, qi, min(MOBA_TOPK, k_ref.shape[0] // blk))
        qs = qm * scale
        qm_scr[h] = qs.astype(BF16)
        if h == 0:
            qa = jnp.concatenate([qs[:, :LANE], jnp.where(lane_r < SEL_LANES, bias, qs[:, LANE:])], axis=1)
        else:
            qa = jnp.concatenate([jnp.where(lane_r < SEL_LANES, bias, qs[:, :LANE]), qs[:, LANE:]], axis=1)
        qa_scr[h] = qa.astype(BF16)

    def v_masked(vb):
        return [jnp.where(lane_h == h, vb, 0.0).astype(BF16) for h in range(ATT_KV_HEADS)]

    def softmax_step(s, m_old):
        s0, s1 = s[:, :LANE], s[:, LANE:]
        m_new = jnp.maximum(m_old, jnp.broadcast_to(jnp.max(jnp.maximum(s0, s1), axis=1, keepdims=True), (rows, LANE)))
        p0 = jnp.exp(s0 - m_new)
        p1 = jnp.exp(s1 - m_new)
        psum = jnp.broadcast_to(jnp.sum(p0 + p1, axis=1, keepdims=True), (rows, LANE))
        return m_new, psum, jnp.concatenate([p0, p1], axis=1).astype(BF16)

    start = pl.multiple_of(qi * blk, blk)
    kd = k_ref[pl.ds(start, blk), :].astype(BF16)
    vms = v_masked(v_ref[pl.ds(start, blk), :])
    qrow = lax.broadcasted_iota(jnp.int32, (rows, kvw), 0) % blk
    kcol = lax.broadcasted_iota(jnp.int32, (rows, kvw), 1)
    causal = kcol <= qrow
    acc = None
    for h in range(ATT_KV_HEADS):
        s = lax.dot_general(qm_scr[h], kd, NT_DIMS, preferred_element_type=F32)
        s = jnp.where(causal, s, NEG_BIG)
        m_new, psum, p = softmax_step(s, jnp.full((rows, LANE), NEG_BIG, F32))
        m_scr[h] = m_new
        l_scr[h] = psum
        d = jnp.dot(p, vms[h], preferred_element_type=F32)
        acc = d if acc is None else acc + d
    acc_scr[...] = acc

    def body(j, carry):
        off = pl.multiple_of(j * blk, blk)
        kb = k_ref[pl.ds(off, blk), :]
        lane_k = lax.broadcasted_iota(jnp.int32, (blk, LANE), 1)
        onehot = jnp.where(lane_k == j, 1.0, 0.0)
        in_sel = lane_k < SEL_LANES
        k_lo = jnp.concatenate([jnp.where(in_sel, onehot, kb[:, :LANE]), kb[:, LANE:]], axis=1).astype(BF16)
        k_hi = jnp.concatenate([kb[:, :LANE], jnp.where(in_sel, onehot, kb[:, LANE:])], axis=1).astype(BF16)
        vmj = v_masked(v_ref[pl.ds(off, blk), :])
        ps, alphas = [], []
        for h in range(ATT_KV_HEADS):
            s = lax.dot_general(qa_scr[h], k_hi if h == 0 else k_lo, NT_DIMS, preferred_element_type=F32)
            m_old = m_scr[h]
            m_new, psum, p = softmax_step(s, m_old)
            alpha = jnp.exp(m_old - m_new)
            l_scr[h] = alpha * l_scr[h] + psum
            m_scr[h] = m_new
            ps.append(p)
            alphas.append(alpha)
        a_lo = jnp.where(lane_r < ATT_HEAD_DIM, alphas[0], alphas[1])
        a_hi = jnp.where(lane_r < ATT_HEAD_DIM, alphas[2], alphas[3])
        upd = None
        for h in range(ATT_KV_HEADS):
            d = jnp.dot(ps[h], vmj[h], preferred_element_type=F32)
            upd = d if upd is None else upd + d
        acc_scr[...] = acc_scr[...] * jnp.concatenate([a_lo, a_hi], axis=1) + upd
        return carry

    lax.fori_loop(0, qi, body, 0)

    l_lo = jnp.where(lane_r < ATT_HEAD_DIM, l_scr[0], l_scr[1])
    l_hi = jnp.where(lane_r < ATT_HEAD_DIM, l_scr[2], l_scr[3])
    out = acc_scr[...] / jnp.concatenate([l_lo, l_hi], axis=1)
    for g in range(ATT_GROUP):
        o_ref[:, g * kvw:(g + 1) * kvw] = out[g * blk:(g + 1) * blk, :].astype(BF16)


def _moba_prompt(q, k, v, kmean, bp, seq):
    blk = MOBA_BLOCK
    nq = seq // blk
    kvw = k.shape[1]
    rows = ATT_GROUP * blk
    return pl.pallas_call(
        _moba_prompt_kernel,
        grid=(bp, nq),
        in_specs=[pl.BlockSpec((blk, q.shape[1]), lambda b, i: (b * nq + i, 0)),
                  pl.BlockSpec((seq, kvw), lambda b, i: (b, 0)),
                  pl.BlockSpec((seq, kvw), lambda b, i: (b, 0)),
                  pl.BlockSpec((1, LANE, kvw), lambda b, i: (b, 0, 0))],
        out_specs=pl.BlockSpec((blk, q.shape[1]), lambda b, i: (b * nq + i, 0)),
        out_shape=jax.ShapeDtypeStruct(q.shape, BF16),
        scratch_shapes=[pltpu.VMEM((ATT_KV_HEADS, rows, kvw), BF16),
                        pltpu.VMEM((ATT_KV_HEADS, rows, kvw), BF16),
                        pltpu.VMEM((ATT_KV_HEADS, rows, LANE), F32),
                        pltpu.VMEM((ATT_KV_HEADS, rows, LANE), F32),
                        pltpu.VMEM((rows, kvw), F32)],
        compiler_params=_cparams("parallel", "arbitrary"),
        name="moba_prompt",
    )(q, k, v, kmean)


def _moba_sample_kernel(pt_ref, q_ref, kn_ref, vn_ref, *rest, n_pages):
    k_pages = rest[:n_pages]
    v_pages = rest[n_pages:2 * n_pages]
    o_ref = rest[2 * n_pages]
    kvw = kn_ref.shape[2]
    nh = ATT_HEADS
    scale = ATT_HEAD_DIM ** -0.5
    q = q_ref[0]
    kn = kn_ref[0]
    vn = vn_ref[0]
    row = lax.broadcasted_iota(jnp.int32, (nh, kvw), 0)
    own = (lax.broadcasted_iota(jnp.int32, (nh, kvw), 1) // ATT_HEAD_DIM) == (row % ATT_KV_HEADS)
    q16 = jnp.zeros((nh, kvw), F32)
    for g in range(ATT_GROUP):
        qg = jnp.broadcast_to(q[:, g * kvw:(g + 1) * kvw], (nh, kvw))
        q16 = jnp.where(own & (row // ATT_KV_HEADS == g), qg, q16)

    k_all = jnp.concatenate([r[0] for r in k_pages], axis=0)
    v_all = jnp.concatenate([r[0] for r in v_pages], axis=0)
    past = k_all.shape[0]
    n_past = past // MOBA_BLOCK
    r128 = lax.broadcasted_iota(jnp.int32, (LANE, kvw), 0)
    kmean = jnp.zeros((LANE, kvw), F32)
    for j in range(n_past):
        mj = jnp.mean(k_all[j * MOBA_BLOCK:(j + 1) * MOBA_BLOCK], axis=0, keepdims=True)
        kmean = jnp.where(r128 == j, jnp.broadcast_to(mj, (LANE, kvw)), kmean)
    scores = lax.dot_general(q16, kmean, NT_DIMS, precision=HIGHEST, preferred_element_type=F32)
    bias = _select_bias(scores, n_past, min(MOBA_TOPK, n_past + 1))

    blk_of_key = lax.broadcasted_iota(jnp.int32, (LANE, past), 1) // MOBA_BLOCK
    expand = jnp.where(blk_of_key == lax.broadcasted_iota(jnp.int32, (LANE, past), 0), 1.0, 0.0).astype(BF16)
    lane_b = lax.broadcasted_iota(jnp.int32, bias.shape, 1)
    bias_b = jnp.where(lane_b < n_past, bias, 0.0).astype(BF16)
    qs = q16 * scale
    logits = (lax.dot_general(qs.astype(BF16), k_all.astype(BF16), NT_DIMS, preferred_element_type=F32)
              + jnp.dot(bias_b, expand, preferred_element_type=F32))
    l_new = jnp.sum(qs * kn, axis=1, keepdims=True)
    m = jnp.maximum(jnp.max(logits, axis=1, keepdims=True), l_new)
    p = jnp.exp(logits - m)
    pn = jnp.exp(l_new - m)
    denom = jnp.sum(p, axis=1, keepdims=True) + pn
    o = (jnp.dot(p.astype(BF16), v_all.astype(BF16), preferred_element_type=F32) + pn * vn) / denom
    o = jnp.where(own, o, 0.0)
    for g in range(ATT_GROUP):
        o_ref[0, :, g * kvw:(g + 1) * kvw] = jnp.sum(o[g * ATT_KV_HEADS:(g + 1) * ATT_KV_HEADS], axis=0, keepdims=True)


def _moba_sample(q, kn, vn, cache_k, cache_v, page_table):
    bs = q.shape[0]
    n_pages = page_table.shape[1]
    page, kvw = cache_k.shape[1], cache_k.shape[2]
    one = lambda n: pl.BlockSpec((1, 1, n), lambda b, pt: (b, 0, 0))

    def page_spec(p):
        return pl.BlockSpec((1, page, kvw), lambda b, pt: (pt[b, p], 0, 0))

    grid_spec = pltpu.PrefetchScalarGridSpec(
        num_scalar_prefetch=1,
        grid=(bs,),
        in_specs=[one(q.shape[1]), one(kvw), one(kvw)] + [page_spec(p) for p in range(n_pages)] * 2,
        out_specs=one(q.shape[1]),
    )
    return pl.pallas_call(
        functools.partial(_moba_sample_kernel, n_pages=n_pages),
        grid_spec=grid_spec,
        out_shape=jax.ShapeDtypeStruct((bs, 1, q.shape[1]), F32),
        compiler_params=_cparams("parallel"),
        name="moba_sample",
    )(page_table, q.reshape(bs, 1, -1), kn.reshape(bs, 1, kvw), vn.reshape(bs, 1, kvw),
      *([cache_k] * n_pages), *([cache_v] * n_pages))


def _pad_lanes(v, fill=0.0):
    return jnp.pad(v.reshape(1, -1), ((0, 0), (0, LANE - v.shape[0])), constant_values=fill)


def kernel(x_prompt, x_sample, state_sc_conv, state_ssd_conv, state_ssm, cache_k, cache_v, page_table,
           c_prompt, c_sample, norm_mix_w, norm_mlp_w, norm_final_w, w_ada, b_ada, w_in_e, sc_conv_w,
           ssd_conv_w, ssd_conv_b, ssd_dt_bias, ssd_a_log, ssd_d, ssd_norm_w, w_out_e, w_qkv, w_o, w_up, w_down):
    bp, seq, d = x_prompt.shape
    bs, dec_seq, _ = x_sample.shape
    assert dec_seq == 1 and seq % MOBA_BLOCK == 0 and seq % SSD_CHUNK == 0 and seq // MOBA_BLOCK <= SEL_LANES
    n_pages, page = page_table.shape[1], cache_k.shape[2]
    past_len = n_pages * page
    assert past_len % MOBA_BLOCK == 0 and MOBA_BLOCK % page == 0
    dsc = sc_conv_w.shape[2]
    conv_dim = ssd_conv_w.shape[2]
    n_heads = ssd_d.shape[1]
    d_inner = n_heads * SSD_HEAD_DIM
    main = 3 * dsc + d_inner + conv_dim
    kvw = ATT_KV_HEADS * ATT_HEAD_DIM
    nq = ATT_HEADS * ATT_HEAD_DIM

    xp = x_prompt.reshape(bp * seq, d)
    xs = x_sample.reshape(bs, d)

    rows = -(-(bs + bp) // SUBLANE) * SUBLANE
    c_all = jnp.pad(jnp.concatenate([c_sample, c_prompt], axis=0), ((0, rows - bs - bp), (0, 0)))
    ada = _ada(c_all, w_ada, b_ada)

    w_in = w_in_e[0][:, :main].astype(BF16)
    w_dt = jnp.pad(w_in_e[0][:, main:], ((0, 0), (0, LANE - n_heads))).astype(BF16)
    w_out = w_out_e[0].astype(BF16)
    wq = w_qkv[0][:, :nq].reshape(d, ATT_KV_HEADS, ATT_GROUP, ATT_HEAD_DIM).transpose(0, 2, 1, 3).reshape(d, nq)
    w_qkv_p = jnp.concatenate([wq, w_qkv[0][:, nq:]], axis=1).astype(BF16)
    w_o_p = w_o[0].reshape(ATT_KV_HEADS, ATT_GROUP, ATT_HEAD_DIM, d).transpose(1, 0, 2, 3).reshape(nq, d).astype(BF16)
    w_up_b = w_up.astype(BF16)
    w_down_b = w_down.astype(BF16)

    scw = sc_conv_w[0]
    cw = ssd_conv_w[0]
    cb = ssd_conv_b[0].reshape(1, conv_dim)
    dtb = _pad_lanes(ssd_dt_bias[0])
    alog = _pad_lanes(ssd_a_log[0])
    dexp = jnp.repeat(ssd_d[0], SSD_HEAD_DIM).reshape(1, d_inner)
    ssd_nw = ssd_norm_w[0].reshape(1, d_inner)

    mod_p = ada[0, bs:bs + bp].reshape(bp, 1, 6 * d)
    mod_s = ada[0, :bs].reshape(1, bs, 6 * d)
    nw_mix = norm_mix_w[0].reshape(1, d)
    nw_mlp = norm_mlp_w[0].reshape(1, d)
    fw = norm_final_w.reshape(1, d)

    proj_p, dt_p = _nm_matmul(xp, nw_mix, mod_p, 0, 1, w_in, w_dt, seq, 512, 1024)
    ycat_p, scst_p, cvst_p, ssm_p = _ssd_prompt(proj_p, dt_p, scw, cw, cb, dtb, alog, dexp, ssd_nw, bp, seq)
    xp = _mm_res(ycat_p, w_out, xp, mod_p, 2, seq, 512)
    xp = _mlp(xp, nw_mlp, mod_p, w_up_b[0], w_down_b[0], fw, seq, 512, False)

    proj_s, dt_s = _nm_matmul(xs, nw_mix, mod_s, 0, 1, w_in, w_dt, bs, bs, 1024)
    ysc_s, scn_s, cvn_s, xs_s, b_s, c_s, xdt_t, dec_t = _even_sample_prep(
        proj_s, dt_s, state_sc_conv[0].reshape(bs, -1), state_ssd_conv[0].reshape(bs, -1),
        scw, cw, cb, dtb, alog, d_inner)
    ssm_s, y_t = _ssm_sample(state_ssm[0], xdt_t, dec_t, b_s, c_s)
    xs = _even_out_sample(y_t, xs_s, proj_s, ysc_s, dexp, ssd_nw, w_out, xs, mod_s)
    xs = _mlp(xs, nw_mlp, mod_s, w_up_b[0], w_down_b[0], fw, bs, bs, False)

    mod_p = ada[1, bs:bs + bp].reshape(bp, 1, 6 * d)
    mod_s = ada[1, :bs].reshape(1, bs, 6 * d)
    nw_mix = norm_mix_w[1].reshape(1, d)
    nw_mlp = norm_mlp_w[1].reshape(1, d)

    tabs_p = _rope_tables(seq, 0, 1)
    q_p, k_p, v_p, km_p = _qkv(xp, nw_mix, mod_p, w_qkv_p, tabs_p, seq, MOBA_BLOCK)
    n_blk = seq // MOBA_BLOCK
    kmean_p = jnp.pad(km_p.reshape(bp, n_blk, kvw), ((0, 0), (0, LANE - n_blk), (0, 0)))
    o_p = _moba_prompt(q_p, k_p, v_p, kmean_p, bp, seq)
    xp = _mm_res(o_p, w_o_p, xp, mod_p, 2, seq, 512)
    y_prompt = _mlp(xp, nw_mlp, mod_p, w_up_b[1], w_down_b[1], fw, seq, 512, True)

    tabs_s = _rope_tables(bs, past_len, 0)
    q_s, k_s, v_s, _ = _qkv(xs, nw_mix, mod_s, w_qkv_p, tabs_s, bs, bs)
    o_s = _moba_sample(q_s, k_s, v_s, cache_k[0].reshape(-1, page, kvw), cache_v[0].reshape(-1, page, kvw), page_table)
    xs = _mm_res(o_s.reshape(bs, nq).astype(BF16), w_o_p, xs, mod_s, 2, bs, bs)
    y_sample = _mlp(xs, nw_mlp, mod_s, w_up_b[1], w_down_b[1], fw, bs, bs, True)

    return (y_prompt.reshape(bp, seq, d), y_sample.reshape(bs, 1, d),
            scst_p[:, SUBLANE - 2:, :][None], scn_s.reshape(1, bs, 2, dsc),
            cvst_p[:, SUBLANE - 3:, :][None], cvn_s.reshape(1, bs, 3, conv_dim),
            ssm_p.reshape(1, bp, n_heads, SSD_HEAD_DIM, SSD_STATE), ssm_s[None],
            k_p.reshape(1, bp, seq, ATT_KV_HEADS, ATT_HEAD_DIM), k_s.reshape(1, bs, 1, ATT_KV_HEADS, ATT_HEAD_DIM),
            v_p.reshape(1, bp, seq, ATT_KV_HEADS, ATT_HEAD_DIM), v_s.reshape(1, bs, 1, ATT_KV_HEADS, ATT_HEAD_DIM))
```

```python
import functools
import math

import jax
import jax.numpy as jnp
from jax import lax
from jax.experimental import pallas as pl
from jax.experimental.pallas import tpu as pltpu

F32 = jnp.float32
BF16 = jnp.bfloat16
HIGHEST = lax.Precision.HIGHEST

EPS = 1e-6
LANE = 128
SUBLANE = 8
SSD_HEAD_DIM = 64
SSD_GROUPS = 4
SSD_STATE = 128
SSD_CHUNK = 256
ATT_HEADS = 16
ATT_KV_HEADS = 4
ATT_GROUP = ATT_HEADS // ATT_KV_HEADS
ATT_HEAD_DIM = 64
ROT_DIM = ATT_HEAD_DIM // 4
ROPE_THETA = 500000.0
MOBA_BLOCK = 256
MOBA_TOPK = 3
SEL_LANES = 16
NEG_BIG = -1e30
VMEM_LIMIT = 56 * 1024 * 1024

NT_DIMS = (((1,), (1,)), ((), ()))


def _cparams(*sem):
    return pltpu.CompilerParams(dimension_semantics=sem, vmem_limit_bytes=VMEM_LIMIT)


def _silu(x):
    return x * jax.nn.sigmoid(x)


def _softplus(x):
    return jnp.maximum(x, 0.0) + jnp.log1p(jnp.exp(-jnp.abs(x)))


def _norm_mod(x, nw, shift, scale):
    ms = jnp.mean(x * x, axis=-1, keepdims=True)
    xn = x * lax.rsqrt(ms + EPS) * nw
    return xn * (1.0 + scale) + shift


def _resident(shape):
    nd = len(shape)
    return pl.BlockSpec(shape, lambda *_: (0,) * nd, pipeline_mode=pl.Buffered(1))


def _ada_kernel(c_ref, w_ref, b_ref, o_ref):
    s = _silu(c_ref[...]).astype(BF16)
    o_ref[0] = jnp.dot(s, w_ref[0].astype(BF16), preferred_element_type=F32) + b_ref[0]


def _ada(c_all, w_ada, b_ada):
    rows, d = c_all.shape
    n_layers, _, n6 = w_ada.shape
    tn = 1024
    return pl.pallas_call(
        _ada_kernel,
        grid=(n_layers, n6 // tn),
        in_specs=[pl.BlockSpec((rows, d), lambda l, j: (0, 0)),
                  pl.BlockSpec((1, d, tn), lambda l, j: (l, 0, j)),
                  pl.BlockSpec((1, 1, tn), lambda l, j: (l, 0, j))],
        out_specs=pl.BlockSpec((1, rows, tn), lambda l, j: (l, 0, j)),
        out_shape=jax.ShapeDtypeStruct((n_layers, rows, n6), F32),
        compiler_params=_cparams("parallel", "parallel"),
        name="adaln",
    )(c_all, w_ada, b_ada.reshape(n_layers, 1, n6))


def _mod_spec(mod, k, rows_per_group, tm):
    _, r, n6 = mod.shape
    d = n6 // 6
    tiles_per_group = max(rows_per_group // tm, 1)
    return pl.BlockSpec((1, r, d), lambda i, *_: (i // tiles_per_group, 0, k))


def _nm_matmul_kernel(x_ref, nw_ref, sh_ref, sc_ref, w_ref, w2_ref, o_ref, o2_ref, h_scr):
    @pl.when(pl.program_id(1) == 0)
    def _():
        hb = _norm_mod(x_ref[...], nw_ref[...], sh_ref[0], sc_ref[0]).astype(BF16)
        h_scr[...] = hb
        o2_ref[...] = jnp.dot(hb, w2_ref[...], preferred_element_type=F32)

    o_ref[...] = jnp.dot(h_scr[...], w_ref[...], preferred_element_type=F32)


def _nm_matmul(x, nw, mod, k_shift, k_scale, w, w2, rows_per_group, tm, tn):
    t, d = x.shape
    n = w.shape[1]
    n2 = w2.shape[1]
    return pl.pallas_call(
        _nm_matmul_kernel,
        grid=(t // tm, n // tn),
        in_specs=[pl.BlockSpec((tm, d), lambda i, j: (i, 0)),
                  pl.BlockSpec((1, d), lambda i, j: (0, 0)),
                  _mod_spec(mod, k_shift, rows_per_group, tm),
                  _mod_spec(mod, k_scale, rows_per_group, tm),
                  pl.BlockSpec((d, tn), lambda i, j: (0, j)),
                  pl.BlockSpec((d, n2), lambda i, j: (0, 0))],
        out_specs=[pl.BlockSpec((tm, tn), lambda i, j: (i, j)),
                   pl.BlockSpec((tm, n2), lambda i, j: (i, 0))],
        out_shape=[jax.ShapeDtypeStruct((t, n), F32), jax.ShapeDtypeStruct((t, n2), F32)],
        scratch_shapes=[pltpu.VMEM((tm, d), BF16)],
        compiler_params=_cparams("parallel", "arbitrary"),
        name="in_proj",
    )(x, nw, mod, mod, w, w2)


def _mm_res_kernel(a_ref, w_ref, res_ref, g_ref, o_ref):
    o_ref[...] = res_ref[...] + g_ref[0] * jnp.dot(a_ref[...], w_ref[...], preferred_element_type=F32)


def _mm_res(a, w, res, mod, k_gate, rows_per_group, tm):
    t, kdim = a.shape
    d = w.shape[1]
    return pl.pallas_call(
        _mm_res_kernel,
        grid=(t // tm,),
        in_specs=[pl.BlockSpec((tm, kdim), lambda i: (i, 0)),
                  _resident((kdim, d)),
                  pl.BlockSpec((tm, d), lambda i: (i, 0)),
                  _mod_spec(mod, k_gate, rows_per_group, tm)],
        out_specs=pl.BlockSpec((tm, d), lambda i: (i, 0)),
        out_shape=jax.ShapeDtypeStruct((t, d), F32),
        compiler_params=_cparams("parallel"),
        name="proj_residual",
    )(a, w, res, mod)


def _mlp_kernel(x_ref, nw_ref, sh_ref, sc_ref, g_ref, wu_ref, wd_ref, fw_ref, o_ref, *, final_norm, tf):
    x = x_ref[...]
    hb = _norm_mod(x, nw_ref[...], sh_ref[0], sc_ref[0]).astype(BF16)
    acc = jnp.zeros(x.shape, F32)
    for f in range(wu_ref.shape[1] // tf):
        a = jnp.maximum(jnp.dot(hb, wu_ref[:, f * tf:(f + 1) * tf], preferred_element_type=F32), 0.0)
        acc = acc + jnp.dot((a * a).astype(BF16), wd_ref[f * tf:(f + 1) * tf, :], preferred_element_type=F32)
    y = x + g_ref[0] * acc
    if final_norm:
        ms = jnp.mean(y * y, axis=-1, keepdims=True)
        y = y * lax.rsqrt(ms + EPS) * fw_ref[...]
    o_ref[...] = y


def _mlp(x, nw, mod, w_up, w_down, fw, rows_per_group, tm, final_norm):
    t, d = x.shape
    dff = w_up.shape[1]
    return pl.pallas_call(
        functools.partial(_mlp_kernel, final_norm=final_norm, tf=1024),
        grid=(t // tm,),
        in_specs=[pl.BlockSpec((tm, d), lambda i: (i, 0)),
                  pl.BlockSpec((1, d), lambda i: (0, 0)),
                  _mod_spec(mod, 3, rows_per_group, tm),
                  _mod_spec(mod, 4, rows_per_group, tm),
                  _mod_spec(mod, 5, rows_per_group, tm),
                  _resident((d, dff)),
                  _resident((dff, d)),
                  pl.BlockSpec((1, d), lambda i: (0, 0))],
        out_specs=pl.BlockSpec((tm, d), lambda i: (i, 0)),
        out_shape=jax.ShapeDtypeStruct((t, d), F32),
        compiler_params=_cparams("parallel"),
        name="mlp",
    )(x, nw, mod, mod, mod, w_up, w_down, fw)


def _shift_rows(cur, tail, k):
    rolled = pltpu.roll(cur, k, axis=0)
    row = lax.broadcasted_iota(jnp.int32, tail.shape, 0)
    top = jnp.where(row < k, pltpu.roll(tail, k, axis=0), rolled[:SUBLANE])
    return jnp.concatenate([top, rolled[SUBLANE:]], axis=0)


def _causal_conv(cur, tail, w):
    kk = w.shape[0]
    out = cur * w[kk - 1:kk, :]
    for s in range(1, kk):
        out = out + _shift_rows(cur, tail, s) * w[kk - 1 - s:kk - s, :]
    return out


def _expand_heads(m, h0, n_cols):
    rows = m.shape[0]
    lane = lax.broadcasted_iota(jnp.int32, (rows, LANE), 1)
    cols = []
    for c in range(n_cols):
        lo = jnp.broadcast_to(m[:, h0 + 2 * c:h0 + 2 * c + 1], (rows, LANE))
        hi = jnp.broadcast_to(m[:, h0 + 2 * c + 1:h0 + 2 * c + 2], (rows, LANE))
        cols.append(jnp.where(lane < SSD_HEAD_DIM, lo, hi))
    return jnp.concatenate(cols, axis=1)


def _ssd_prompt_kernel(scb_ref, scc_ref, scx_ref, z0_ref, z1_ref, x0_ref, x1_ref, bc_ref, dt_ref,
                       scw_ref, cw_ref, cb_ref, dtb_ref, alog_ref, dexp_ref, nw_ref,
                       y_ref, scst_ref, cvst_ref, ssm_ref,
                       utail, xtail, s_scr):
    c = pl.program_id(1)
    q = scb_ref.shape[0]
    dsc = scb_ref.shape[1]
    gw = 2 * x0_ref.shape[1] // SSD_GROUPS
    hpg = gw // SSD_HEAD_DIM

    @pl.when(c == 0)
    def _():
        utail[...] = jnp.zeros(utail.shape, F32)
        xtail[...] = jnp.zeros(xtail.shape, F32)
        s_scr[...] = jnp.zeros(s_scr.shape, F32)

    u = scc_ref[...] * scx_ref[...]
    y_sc = scb_ref[...] * _causal_conv(u, utail[...], scw_ref[...])
    utail[...] = u[q - SUBLANE:, :]
    y_ref[:, :dsc] = y_sc.astype(BF16)

    bc = bc_ref[...]
    n_bc = bc.shape[1]
    o_bc = 2 * x0_ref.shape[1]
    bc_c = _silu(_causal_conv(bc, xtail[:, o_bc:o_bc + n_bc], cw_ref[:, o_bc:o_bc + n_bc]) + cb_ref[:, o_bc:o_bc + n_bc])
    xtail[:, o_bc:o_bc + n_bc] = bc[q - SUBLANE:, :]
    half = n_bc // 2

    dt = _softplus(dt_ref[...] + dtb_ref[...])
    a = dt * (-jnp.exp(alog_ref[...]))
    ri = lax.broadcasted_iota(jnp.int32, (q, q), 0)
    ci = lax.broadcasted_iota(jnp.int32, (q, q), 1)
    tril = ri >= ci
    a_cs = jnp.dot(tril.astype(F32), a, precision=HIGHEST, preferred_element_type=F32)
    a_cs_t = a_cs.T
    quad = 4 * SSD_HEAD_DIM
    lane_head = lax.broadcasted_iota(jnp.int32, (q, quad), 1) // SSD_HEAD_DIM

    for g in range(SSD_GROUPS):
        x_ref = x0_ref if g < SSD_GROUPS // 2 else x1_ref
        z_ref = z0_ref if g < SSD_GROUPS // 2 else z1_ref
        lo = (g % (SSD_GROUPS // 2)) * gw
        go = g * gw
        xg = x_ref[:, lo:lo + gw]
        xs = _silu(_causal_conv(xg, xtail[:, go:go + gw], cw_ref[:, go:go + gw]) + cb_ref[:, go:go + gw])
        xtail[:, go:go + gw] = xg[q - SUBLANE:, :]

        n_cols = gw // LANE
        dt_x = _expand_heads(dt, g * hpg, n_cols)
        a_x = _expand_heads(a_cs, g * hpg, n_cols)
        ea = jnp.exp(a_x)
        dte = jnp.exp(a_x[q - 1:q, :] - a_x)
        xdt = xs * dt_x
        xdt_b = xdt.astype(BF16)

        bg = bc_c[:, g * SSD_STATE:(g + 1) * SSD_STATE]
        cg_b = bc_c[:, half + g * SSD_STATE:half + (g + 1) * SSD_STATE].astype(BF16)
        cb_mat = lax.dot_general(cg_b, bg.astype(BF16), NT_DIMS, preferred_element_type=F32)
        s_g = s_scr[:, go:go + gw]
        y_off = jnp.dot(cg_b, s_g.astype(BF16), preferred_element_type=F32) * ea

        y_quads = []
        for qd in range(gw // quad):
            xq = xdt[:, qd * quad:(qd + 1) * quad]
            acc = None
            for i in range(4):
                h = g * hpg + qd * 4 + i
                seg = a_cs[:, h:h + 1] - a_cs_t[h:h + 1, :]
                dec = jnp.exp(jnp.where(tril, seg, -jnp.inf))
                w_h = (cb_mat * dec).astype(BF16)
                xm = jnp.where(lane_head == i, xq, 0.0).astype(BF16)
                d = jnp.dot(w_h, xm, preferred_element_type=F32)
                acc = d if acc is None else acc + d
            y_quads.append(acc)
        y = jnp.concatenate(y_quads, axis=1) + y_off + dexp_ref[:, go:go + gw] * xs

        zg = z_ref[:, lo:lo + gw]
        y = y * _silu(zg)
        ms = jnp.mean(y * y, axis=-1, keepdims=True)
        y = y * lax.rsqrt(ms + EPS) * nw_ref[:, go:go + gw]
        y_ref[:, dsc + go:dsc + go + gw] = y.astype(BF16)

        s_new = s_g * ea[q - 1:q, :] + jnp.dot(bg.T.astype(BF16), (xdt * dte).astype(BF16),
                                               preferred_element_type=F32)
        s_scr[:, go:go + gw] = s_new

    @pl.when(c == pl.num_programs(1) - 1)
    def _():
        scst_ref[0] = utail[...]
        cvst_ref[0] = xtail[...]
        ssm_ref[0] = s_scr[...].T


def _ssd_prompt(proj, dt_raw, scw, cw, cb, dtb, alog, dexp, nw, bp, seq):
    q = min(SSD_CHUNK, seq)
    nc = seq // q
    dsc = scw.shape[1]
    d_inner = dexp.shape[1]
    conv_dim = cw.shape[1]
    bw = dsc

    def col(kb):
        return pl.BlockSpec((q, bw), lambda b, c: (b * nc + c, kb))

    n_z = d_inner // bw
    specs = [col(0), col(1), col(2)]
    specs += [col(3 + i) for i in range(n_z)]
    specs += [col(3 + n_z + i) for i in range(n_z)]
    specs += [col(3 + 2 * n_z)]
    specs += [pl.BlockSpec((q, LANE), lambda b, c: (b * nc + c, 0))]
    for arr in (scw, cw, cb, dtb, alog, dexp, nw):
        specs.append(pl.BlockSpec(arr.shape, lambda b, c: (0, 0)))
    return pl.pallas_call(
        _ssd_prompt_kernel,
        grid=(bp, nc),
        in_specs=specs,
        out_specs=[pl.BlockSpec((q, dsc + d_inner), lambda b, c: (b * nc + c, 0)),
                   pl.BlockSpec((1, SUBLANE, dsc), lambda b, c: (b, 0, 0)),
                   pl.BlockSpec((1, SUBLANE, conv_dim), lambda b, c: (b, 0, 0)),
                   pl.BlockSpec((1, d_inner, SSD_STATE), lambda b, c: (b, 0, 0))],
        out_shape=[jax.ShapeDtypeStruct((bp * seq, dsc + d_inner), BF16),
                   jax.ShapeDtypeStruct((bp, SUBLANE, dsc), F32),
                   jax.ShapeDtypeStruct((bp, SUBLANE, conv_dim), F32),
                   jax.ShapeDtypeStruct((bp, d_inner, SSD_STATE), F32)],
        scratch_shapes=[pltpu.VMEM((SUBLANE, dsc), F32),
                        pltpu.VMEM((SUBLANE, conv_dim), F32),
                        pltpu.VMEM((SSD_STATE, d_inner), F32)],
        compiler_params=_cparams("parallel", "arbitrary"),
        name="ssd_prompt",
    )(*([proj] * (4 + 2 * n_z)), dt_raw, scw, cw, cb, dtb, alog, dexp, nw)


def _even_sample_prep_kernel(proj_ref, dt_ref, scst_ref, cvst_ref, scw_ref, cw_ref, cb_ref, dtb_ref, alog_ref,
                             ysc_ref, scn_ref, cvn_ref, xs_ref, b_ref, c_ref, xdt_t_ref, dec_ref):
    dsc = scw_ref.shape[1]
    conv_dim = cw_ref.shape[1]
    d_inner = xs_ref.shape[1]
    n_heads = d_inner // SSD_HEAD_DIM
    scb = proj_ref[:, :dsc]
    u = proj_ref[:, dsc:2 * dsc] * proj_ref[:, 2 * dsc:3 * dsc]
    xbc = proj_ref[:, 3 * dsc + d_inner:3 * dsc + d_inner + conv_dim]
    b0 = scst_ref[:, :dsc]
    b1 = scst_ref[:, dsc:]
    ysc_ref[...] = scb * (b0 * scw_ref[0:1, :] + b1 * scw_ref[1:2, :] + u * scw_ref[2:3, :])
    scn_ref[:, :dsc] = b1
    scn_ref[:, dsc:] = u

    c0 = cvst_ref[:, :conv_dim]
    c1 = cvst_ref[:, conv_dim:2 * conv_dim]
    c2 = cvst_ref[:, 2 * conv_dim:]
    act = _silu(c0 * cw_ref[0:1, :] + c1 * cw_ref[1:2, :] + c2 * cw_ref[2:3, :] + xbc * cw_ref[3:4, :] + cb_ref[...])
    cvn_ref[:, :conv_dim] = c1
    cvn_ref[:, conv_dim:2 * conv_dim] = c2
    cvn_ref[:, 2 * conv_dim:] = xbc
    xs = act[:, :d_inner]
    n_bc = (conv_dim - d_inner) // 2
    xs_ref[...] = xs
    b_ref[...] = act[:, d_inner:d_inner + n_bc]
    c_ref[...] = act[:, d_inner + n_bc:]

    dt = _softplus(dt_ref[...] + dtb_ref[...])
    dec_ref[...] = jnp.exp(dt * (-jnp.exp(alog_ref[...])))
    xdt_t_ref[...] = (xs * _expand_heads(dt, 0, n_heads // 2)).T


def _even_sample_prep(proj, dt_raw, scst, cvst, scw, cw, cb, dtb, alog, d_inner):
    bs = proj.shape[0]
    dsc = scw.shape[1]
    conv_dim = cw.shape[1]
    n_bc = (conv_dim - d_inner) // 2
    shapes = [(bs, dsc), (bs, 2 * dsc), (bs, 3 * conv_dim), (bs, d_inner), (bs, n_bc), (bs, n_bc),
              (d_inner, bs), (bs, LANE)]
    return pl.pallas_call(
        _even_sample_prep_kernel,
        out_shape=[jax.ShapeDtypeStruct(s, F32) for s in shapes],
        compiler_params=pltpu.CompilerParams(vmem_limit_bytes=VMEM_LIMIT),
        name="even_sample_prep",
    )(proj, dt_raw, scst, cvst, scw, cw, cb, dtb, alog)


def _ssm_sample_kernel(dec_ref, h0_ref, xt_ref, b_ref, c_ref, hn_ref, yt_ref):
    h = pl.program_id(0)
    bs = h0_ref.shape[0]
    p = h0_ref.shape[2]
    n = h0_ref.shape[3]
    lane = lax.broadcasted_iota(jnp.int32, (p, bs), 1)
    xt = xt_ref[...]
    ones = jnp.ones((n, bs), BF16)
    yt = jnp.zeros((p, bs), F32)
    for b in range(bs):
        xcol = jnp.broadcast_to(xt[:, b:b + 1], (p, n))
        hn = h0_ref[b, 0] * dec_ref[b, h] + xcol * b_ref[b:b + 1, :]
        hn_ref[b, 0] = hn
        ysum = jnp.zeros((p, bs), F32)
        rest = hn * c_ref[b:b + 1, :]
        for _ in range(3):
            piece = rest.astype(BF16)
            ysum = ysum + jnp.dot(piece, ones, preferred_element_type=F32)
            rest = rest - piece.astype(F32)
        yt = jnp.where(lane == b, ysum, yt)
    yt_ref[...] = yt


def _ssm_sample(h0, xdt_t, dec, bm, cm):
    bs, n_heads, p, n = h0.shape
    hpg = n_heads // SSD_GROUPS
    return pl.pallas_call(
        _ssm_sample_kernel,
        grid=(n_heads,),
        in_specs=[pl.BlockSpec(memory_space=pltpu.SMEM),
                  pl.BlockSpec((bs, 1, p, n), lambda h: (0, h, 0, 0)),
                  pl.BlockSpec((p, bs), lambda h: (h, 0)),
                  pl.BlockSpec((bs, n), lambda h: (0, h // hpg)),
                  pl.BlockSpec((bs, n), lambda h: (0, h // hpg))],
        out_specs=[pl.BlockSpec((bs, 1, p, n), lambda h: (0, h, 0, 0)),
                   pl.BlockSpec((p, bs), lambda h: (h, 0))],
        out_shape=[jax.ShapeDtypeStruct(h0.shape, F32), jax.ShapeDtypeStruct((n_heads * p, bs), F32)],
        compiler_params=_cparams("parallel"),
        name="ssm_sample",
    )(dec, h0, xdt_t, bm, cm)


def _even_out_sample_kernel(yt_ref, xs_ref, proj_ref, ysc_ref, dexp_ref, nw_ref, w_ref, x_ref, g_ref, o_ref):
    d_inner = xs_ref.shape[1]
    dsc = ysc_ref.shape[1]
    gw = d_inner // SSD_GROUPS
    y = yt_ref[...].T + dexp_ref[...] * xs_ref[...]
    y = y * _silu(proj_ref[:, 3 * dsc:3 * dsc + d_inner])
    parts = []
    for g in range(SSD_GROUPS):
        yg = y[:, g * gw:(g + 1) * gw]
        ms = jnp.mean(yg * yg, axis=-1, keepdims=True)
        parts.append(yg * lax.rsqrt(ms + EPS))
    y_ssd = jnp.concatenate(parts, axis=1) * nw_ref[...]
    cat = jnp.concatenate([ysc_ref[...], y_ssd], axis=1).astype(BF16)
    o_ref[...] = x_ref[...] + g_ref[0] * jnp.dot(cat, w_ref[...], preferred_element_type=F32)


def _even_out_sample(y_t, xs, proj, ysc, dexp, nw, w_out, x, mod):
    bs, d = x.shape
    full = lambda a: pl.BlockSpec(a.shape, lambda i: (0,) * a.ndim)
    return pl.pallas_call(
        _even_out_sample_kernel,
        grid=(1,),
        in_specs=[full(y_t), full(xs), full(proj), full(ysc), full(dexp), full(nw), full(w_out), full(x),
                  pl.BlockSpec((1, bs, d), lambda i: (0, 0, 2))],
        out_specs=pl.BlockSpec((bs, d), lambda i: (0, 0)),
        out_shape=jax.ShapeDtypeStruct((bs, d), F32),
        compiler_params=_cparams("arbitrary"),
        name="even_out_sample",
    )(y_t, xs, proj, ysc, dexp, nw, w_out, x, mod)


def _rope_table_kernel(ca_ref, cb_ref, cc_ref, *, pos0, stride):
    rows = ca_ref.shape[0]
    row = lax.broadcasted_iota(jnp.int32, (rows, LANE), 0) + pl.program_id(0) * rows
    d = lax.broadcasted_iota(jnp.int32, (rows, LANE), 1) % ATT_HEAD_DIM
    half = ROT_DIM // 2
    pos = (pos0 + row * stride).astype(F32)
    inv = jnp.power(jnp.full((rows, LANE), ROPE_THETA, F32), -((d % half).astype(F32) * 2.0 / ROT_DIM))
    ang = pos * inv
    cs = jnp.cos(ang)
    sn = jnp.sin(ang)
    ca_ref[...] = jnp.where(d < ROT_DIM, cs, 1.0)
    cb_ref[...] = jnp.where(d < half, -sn, 0.0)
    cc_ref[...] = jnp.where(d < half, 0.0, jnp.where(d < ROT_DIM, sn, 0.0))


def _rope_tables(rows, pos0, stride):
    tr = min(rows, 512)
    spec = pl.BlockSpec((tr, LANE), lambda i: (i, 0))
    return pl.pallas_call(
        functools.partial(_rope_table_kernel, pos0=pos0, stride=stride),
        grid=(rows // tr,),
        out_specs=[spec, spec, spec],
        out_shape=[jax.ShapeDtypeStruct((rows, LANE), F32)] * 3,
        compiler_params=_cparams("parallel"),
        name="rope_tables",
    )()


def _rope(t, ca, cb, cc):
    cols = []
    for c in range(t.shape[1] // LANE):
        s = t[:, c * LANE:(c + 1) * LANE]
        cols.append(s * ca + pltpu.roll(s, LANE - ROT_DIM // 2, axis=1) * cb + pltpu.roll(s, ROT_DIM // 2, axis=1) * cc)
    return jnp.concatenate(cols, axis=1)


def _sel_keys(k, j):
    lane = lax.broadcasted_iota(jnp.int32, (k.shape[0], LANE), 1)
    onehot = jnp.where(lane == j, 1.0, 0.0)
    in_sel = lane < SEL_LANES
    k_lo = jnp.concatenate([jnp.where(in_sel, onehot, k[:, :LANE]), k[:, LANE:]], axis=1)
    k_hi = jnp.concatenate([k[:, :LANE], jnp.where(in_sel, onehot, k[:, LANE:])], axis=1)
    return k_lo.astype(BF16), k_hi.astype(BF16)


def _qkv_kernel(x_ref, nw_ref, sh_ref, sc_ref, w_ref, ca_ref, cb_ref, cc_ref, q_ref, k_ref, v_ref, *blk_refs,
                blocks_per_seq):
    nq = q_ref.shape[1]
    nk = (w_ref.shape[1] - nq) // 2
    hb = _norm_mod(x_ref[...], nw_ref[...], sh_ref[0], sc_ref[0]).astype(BF16)
    qkv = jnp.dot(hb, w_ref[...], preferred_element_type=F32)
    ca, cb, cc = ca_ref[...], cb_ref[...], cc_ref[...]
    q_ref[...] = _rope(qkv[:, :nq], ca, cb, cc)
    k = _rope(qkv[:, nq:nq + nk], ca, cb, cc)
    v = qkv[:, nq + nk:]
    if blocks_per_seq:
        km_ref, klo_ref, khi_ref, vt_ref = blk_refs
        vt = v.T
        k_ref[0] = k.T
        v_ref[0] = vt
        km_ref[0] = jnp.mean(k, axis=0, keepdims=True)
        klo_ref[...], khi_ref[...] = _sel_keys(k, pl.program_id(0) % blocks_per_seq)
        vt_ref[0] = vt.astype(BF16)
    else:
        k_ref[...] = k
        v_ref[...] = v


def _qkv(x, nw, mod, w, tabs, rows_per_group, tm, per_block):
    t, d = x.shape
    nq = ATT_HEADS * ATT_HEAD_DIM
    nk = ATT_KV_HEADS * ATT_HEAD_DIM
    tab_tiles = tabs[0].shape[0] // tm
    tab = pl.BlockSpec((tm, LANE), lambda i: (i % tab_tiles, 0))
    row = lambda n: pl.BlockSpec((tm, n), lambda i: (i, 0))
    if per_block:
        bps = rows_per_group // tm
        kv_t = pl.BlockSpec((1, nk, tm), lambda i: (i // bps, 0, i % bps))
        out_specs = [row(nq), kv_t, kv_t, pl.BlockSpec((1, 1, nk), lambda i: (i, 0, 0)), row(nk), row(nk),
                     pl.BlockSpec((1, nk, tm), lambda i: (i, 0, 0))]
        out_shape = [jax.ShapeDtypeStruct((t, nq), F32),
                     jax.ShapeDtypeStruct((t // rows_per_group, nk, rows_per_group), F32),
                     jax.ShapeDtypeStruct((t // rows_per_group, nk, rows_per_group), F32),
                     jax.ShapeDtypeStruct((t // tm, 1, nk), F32), jax.ShapeDtypeStruct((t, nk), BF16),
                     jax.ShapeDtypeStruct((t, nk), BF16), jax.ShapeDtypeStruct((t // tm, nk, tm), BF16)]
    else:
        out_specs = [row(nq), row(nk), row(nk)]
        out_shape = [jax.ShapeDtypeStruct((t, nq), F32), jax.ShapeDtypeStruct((t, nk), F32),
                     jax.ShapeDtypeStruct((t, nk), F32)]
    return pl.pallas_call(
        functools.partial(_qkv_kernel, blocks_per_seq=rows_per_group // tm if per_block else 0),
        grid=(t // tm,),
        in_specs=[row(d), pl.BlockSpec((1, d), lambda i: (0, 0)),
                  _mod_spec(mod, 0, rows_per_group, tm), _mod_spec(mod, 1, rows_per_group, tm),
                  _resident(w.shape), tab, tab, tab],
        out_specs=out_specs,
        out_shape=out_shape,
        compiler_params=_cparams("parallel"),
        name="qkv_rope",
    )(x, nw, mod, mod, w, *tabs)


def _select_bias(scores, n_past, n_rounds):
    lane = lax.broadcasted_iota(jnp.int32, scores.shape, 1)
    lanef = lane.astype(F32)
    s = jnp.where(lane < n_past, scores, -jnp.inf)
    bias = jnp.full(scores.shape, NEG_BIG, F32)
    for r in range(n_rounds):
        mx = jnp.max(s, axis=1, keepdims=True)
        idx = jnp.min(jnp.where(s == mx, lanef, float(LANE)), axis=1, keepdims=True)
        bias = jnp.where(lanef == jnp.where(n_past > r, idx, -1.0), 0.0, bias)
        s = jnp.where(lanef == idx, -jnp.inf, s)
    return bias


def _select_bias_t(scores, n_past, n_rounds):
    row = lax.broadcasted_iota(jnp.int32, scores.shape, 0)
    rowf = row.astype(F32)
    s = jnp.where(row < n_past, scores, -jnp.inf)
    bias = jnp.full(scores.shape, NEG_BIG, F32)
    for r in range(n_rounds):
        mx = jnp.max(s, axis=0, keepdims=True)
        idx = jnp.min(jnp.where(s == mx, rowf, float(SEL_LANES)), axis=0, keepdims=True)
        bias = jnp.where(rowf == jnp.where(n_past > r, idx, -1.0), 0.0, bias)
        s = jnp.where(rowf == idx, -jnp.inf, s)
    return jnp.where(row == n_past, 0.0, bias)


def _moba_prompt_kernel(q_ref, klo_ref, khi_ref, vt_ref, km_ref, o_ref, qa_scr, m_scr, l_scr, acc_scr):
    qi = pl.program_id(1)
    blk = q_ref.shape[0]
    kvw = klo_ref.shape[1]
    n_cols = ATT_GROUP * blk // LANE
    scale = ATT_HEAD_DIM ** -0.5
    lane_h = lax.broadcasted_iota(jnp.int32, (blk, kvw), 1) // ATT_HEAD_DIM
    lane_c = lax.broadcasted_iota(jnp.int32, (LANE, LANE), 1)
    kmean = km_ref[0]
    n_rounds = min(MOBA_TOPK, klo_ref.shape[0] // blk)
    pad_rows = jnp.zeros((LANE - SEL_LANES, LANE), F32)

    for h in range(ATT_KV_HEADS):
        qm = jnp.concatenate([jnp.where(lane_h == h, q_ref[:, g * kvw:(g + 1) * kvw], 0.0)
                              for g in range(ATT_GROUP)], axis=0)
        scores_t = lax.dot_general(kmean, qm, NT_DIMS, precision=HIGHEST, preferred_element_type=F32)
        bias_t = _select_bias_t(scores_t, qi, n_rounds)
        qs = qm * scale
        for c in range(n_cols):
            bias_c = jnp.concatenate([bias_t[:, c * LANE:(c + 1) * LANE], pad_rows], axis=0).T
            rows = qs[c * LANE:(c + 1) * LANE, :]
            lo, hi = rows[:, :LANE], rows[:, LANE:]
            if h == 0:
                hi = jnp.where(lane_c < SEL_LANES, bias_c, hi)
            else:
                lo = jnp.where(lane_c < SEL_LANES, bias_c, lo)
            qa_scr[h, c * LANE:(c + 1) * LANE, :] = jnp.concatenate([lo, hi], axis=1).astype(BF16)

    n_q = ATT_GROUP * blk
    key_row = lax.broadcasted_iota(jnp.int32, (blk, n_q), 0)
    q_pos = lax.broadcasted_iota(jnp.int32, (blk, n_q), 1) % blk

    def scores(h, j):
        off = pl.multiple_of(j * blk, blk)
        ka = (khi_ref if h == 0 else klo_ref)[pl.ds(off, blk), :]
        return lax.dot_general(ka, qa_scr[h], NT_DIMS, preferred_element_type=F32)

    def softmax(h, s, first):
        if first:
            s = jnp.where(key_row <= q_pos, s, NEG_BIG)
        mx = jnp.max(s, axis=0, keepdims=True)
        if first:
            m_new, alpha = mx, None
        else:
            m_old = m_scr[h, 0:1, :]
            m_new = jnp.maximum(m_old, mx)
            alpha = jnp.exp(m_old - m_new)
        p = jnp.exp(s - m_new)
        psum = jnp.sum(p, axis=0, keepdims=True)
        l_new = psum if first else alpha * l_scr[h, 0:1, :] + psum
        l_scr[h] = jnp.broadcast_to(l_new, (SUBLANE, n_q))
        m_scr[h] = jnp.broadcast_to(m_new, (SUBLANE, n_q))
        return p.astype(BF16), alpha

    def values(h, j, p, alpha):
        vt_h = vt_ref[j, h * ATT_HEAD_DIM:(h + 1) * ATT_HEAD_DIM, :]
        pv = jnp.dot(vt_h, p, preferred_element_type=F32)
        acc_scr[h] = pv if alpha is None else acc_scr[h] * alpha + pv

    def block_step(j, first):
        s_next = scores(0, j)
        for h in range(ATT_KV_HEADS):
            s = s_next
            if h + 1 < ATT_KV_HEADS:
                s_next = scores(h + 1, j)
            p, alpha = softmax(h, s, first)
            values(h, j, p, alpha)

    block_step(qi, True)

    def body(j, carry):
        block_step(j, False)
        return carry

    lax.fori_loop(0, qi, body, 0)

    for g in range(ATT_GROUP):
        o_t = jnp.concatenate([acc_scr[h, :, g * blk:(g + 1) * blk] / l_scr[h, 0:1, g * blk:(g + 1) * blk]
                               for h in range(ATT_KV_HEADS)], axis=0)
        o_ref[:, g * kvw:(g + 1) * kvw] = o_t.T.astype(BF16)


def _moba_prompt(q, k_lo, k_hi, v_t, kmean, bp, seq):
    blk = MOBA_BLOCK
    nq = seq // blk
    kvw = k_lo.shape[1]
    n_q = ATT_GROUP * blk
    return pl.pallas_call(
        _moba_prompt_kernel,
        grid=(bp, nq),
        in_specs=[pl.BlockSpec((blk, q.shape[1]), lambda b, i: (b * nq + i, 0)),
                  pl.BlockSpec((seq, kvw), lambda b, i: (b, 0)),
                  pl.BlockSpec((seq, kvw), lambda b, i: (b, 0)),
                  pl.BlockSpec((nq, kvw, blk), lambda b, i: (b, 0, 0)),
                  pl.BlockSpec((1, SEL_LANES, kvw), lambda b, i: (b, 0, 0))],
        out_specs=pl.BlockSpec((blk, q.shape[1]), lambda b, i: (b * nq + i, 0)),
        out_shape=jax.ShapeDtypeStruct(q.shape, BF16),
        scratch_shapes=[pltpu.VMEM((ATT_KV_HEADS, n_q, kvw), BF16),
                        pltpu.VMEM((ATT_KV_HEADS, SUBLANE, n_q), F32),
                        pltpu.VMEM((ATT_KV_HEADS, SUBLANE, n_q), F32),
                        pltpu.VMEM((ATT_KV_HEADS, ATT_HEAD_DIM, n_q), F32)],
        compiler_params=_cparams("parallel", "arbitrary"),
        name="moba_prompt",
    )(q, k_lo, k_hi, v_t, kmean)


def _moba_sample_kernel(pt_ref, q_ref, kn_ref, vn_ref, ex_ref, avg_ref, *rest, n_pages):
    k_pages = rest[:n_pages]
    v_pages = rest[n_pages:2 * n_pages]
    o_ref = rest[2 * n_pages]
    hd = ATT_HEAD_DIM
    kvw = kn_ref.shape[2]
    scale = hd ** -0.5
    page = k_pages[0].shape[3]
    n_past = n_pages * page // MOBA_BLOCK
    n_rows = ATT_KV_HEADS * SUBLANE
    q = q_ref[0]
    kn = kn_ref[0]
    vn = vn_ref[0]
    row = lax.broadcasted_iota(jnp.int32, (n_rows, hd), 0)
    q_rows = jnp.zeros((n_rows, hd), F32)
    kn_rows = jnp.zeros((n_rows, hd), F32)
    vn_rows = jnp.zeros((n_rows, hd), F32)
    for h in range(ATT_KV_HEADS):
        in_h = row // SUBLANE == h
        kn_rows = jnp.where(in_h, jnp.broadcast_to(kn[:, h * hd:(h + 1) * hd], (n_rows, hd)), kn_rows)
        vn_rows = jnp.where(in_h, jnp.broadcast_to(vn[:, h * hd:(h + 1) * hd], (n_rows, hd)), vn_rows)
        for g in range(ATT_GROUP):
            lo = g * kvw + h * hd
            q_rows = jnp.where(row == h * SUBLANE + g, jnp.broadcast_to(q[:, lo:lo + hd], (n_rows, hd)), q_rows)
    qs = q_rows * scale
    qs_b = qs.astype(BF16)

    score_parts, logit_parts = [], []
    for h in range(ATT_KV_HEADS):
        rows = slice(h * SUBLANE, (h + 1) * SUBLANE)
        kt = jnp.concatenate([r[0, h] for r in k_pages], axis=1)
        kt_b = kt.astype(BF16)
        kt_lo = (kt - kt_b.astype(F32)).astype(BF16)
        kmt = (jnp.dot(kt_b, avg_ref[...], preferred_element_type=F32)
               + jnp.dot(kt_lo, avg_ref[...], preferred_element_type=F32))
        score_parts.append(jnp.dot(q_rows[rows], kmt, precision=HIGHEST, preferred_element_type=F32))
        logit_parts.append(jnp.dot(qs_b[rows], kt_b, preferred_element_type=F32))
    bias = _select_bias(jnp.concatenate(score_parts, axis=0), n_past, min(MOBA_TOPK, n_past + 1))
    lane_b = lax.broadcasted_iota(jnp.int32, bias.shape, 1)
    bias_b = jnp.where(lane_b < n_past, bias, 0.0).astype(BF16)
    logits = jnp.concatenate(logit_parts, axis=0) + jnp.dot(bias_b, ex_ref[...], preferred_element_type=F32)
    l_new = jnp.sum(qs * kn_rows, axis=1, keepdims=True)
    m = jnp.maximum(jnp.max(logits, axis=1, keepdims=True), l_new)
    p = jnp.exp(logits - m)
    pn = jnp.exp(l_new - m)
    denom = jnp.sum(p, axis=1, keepdims=True) + pn
    p_b = p.astype(BF16)
    o_parts = []
    for h in range(ATT_KV_HEADS):
        vt_b = jnp.concatenate([r[0, h] for r in v_pages], axis=1).astype(BF16)
        o_parts.append(lax.dot_general(p_b[h * SUBLANE:(h + 1) * SUBLANE], vt_b, NT_DIMS, preferred_element_type=F32))
    o = (jnp.concatenate(o_parts, axis=0) + pn * vn_rows) / denom
    for h in range(ATT_KV_HEADS):
        for g in range(ATT_GROUP):
            lo = g * kvw + h * hd
            o_ref[0, :, lo:lo + hd] = o[h * SUBLANE + g:h * SUBLANE + g + 1, :]


def _moba_sample(q, kn, vn, cache_kt, cache_vt, page_table):
    bs = q.shape[0]
    n_pages = page_table.shape[1]
    _, n_kv, hd, page = cache_kt.shape
    kvw = n_kv * hd
    past = n_pages * page
    one = lambda n: pl.BlockSpec((1, 1, n), lambda b, pt: (b, 0, 0))
    expand = (jnp.arange(past)[None, :] // MOBA_BLOCK == jnp.arange(LANE)[:, None]).astype(BF16)
    avg = (expand.T.astype(F32) * (1.0 / MOBA_BLOCK)).astype(BF16)

    def page_spec(p):
        return pl.BlockSpec((1, n_kv, hd, page), lambda b, pt: (pt[b, p], 0, 0, 0))

    grid_spec = pltpu.PrefetchScalarGridSpec(
        num_scalar_prefetch=1,
        grid=(bs,),
        in_specs=[one(q.shape[1]), one(kvw), one(kvw), pl.BlockSpec((LANE, past), lambda b, pt: (0, 0)),
                  pl.BlockSpec((past, LANE), lambda b, pt: (0, 0))]
        + [page_spec(p) for p in range(n_pages)] * 2,
        out_specs=one(q.shape[1]),
    )
    return pl.pallas_call(
        functools.partial(_moba_sample_kernel, n_pages=n_pages),
        grid_spec=grid_spec,
        out_shape=jax.ShapeDtypeStruct((bs, 1, q.shape[1]), F32),
        compiler_params=_cparams("parallel"),
        name="moba_sample",
    )(page_table, q.reshape(bs, 1, -1), kn.reshape(bs, 1, kvw), vn.reshape(bs, 1, kvw), expand, avg,
      *([cache_kt] * n_pages), *([cache_vt] * n_pages))


def _pad_lanes(v, fill=0.0):
    return jnp.pad(v.reshape(1, -1), ((0, 0), (0, LANE - v.shape[0])), constant_values=fill)


def kernel(x_prompt, x_sample, state_sc_conv, state_ssd_conv, state_ssm, cache_k, cache_v, page_table,
           c_prompt, c_sample, norm_mix_w, norm_mlp_w, norm_final_w, w_ada, b_ada, w_in_e, sc_conv_w,
           ssd_conv_w, ssd_conv_b, ssd_dt_bias, ssd_a_log, ssd_d, ssd_norm_w, w_out_e, w_qkv, w_o, w_up, w_down):
    bp, seq, d = x_prompt.shape
    bs, dec_seq, _ = x_sample.shape
    assert dec_seq == 1 and seq % MOBA_BLOCK == 0 and seq % SSD_CHUNK == 0 and seq // MOBA_BLOCK <= SEL_LANES
    n_pages, page = page_table.shape[1], cache_k.shape[2]
    past_len = n_pages * page
    assert past_len % MOBA_BLOCK == 0 and MOBA_BLOCK % page == 0
    dsc = sc_conv_w.shape[2]
    conv_dim = ssd_conv_w.shape[2]
    n_heads = ssd_d.shape[1]
    d_inner = n_heads * SSD_HEAD_DIM
    main = 3 * dsc + d_inner + conv_dim
    kvw = ATT_KV_HEADS * ATT_HEAD_DIM
    nq = ATT_HEADS * ATT_HEAD_DIM

    xp = x_prompt.reshape(bp * seq, d)
    xs = x_sample.reshape(bs, d)

    rows = -(-(bs + bp) // SUBLANE) * SUBLANE
    c_all = jnp.pad(jnp.concatenate([c_sample, c_prompt], axis=0), ((0, rows - bs - bp), (0, 0)))
    ada = _ada(c_all, w_ada, b_ada)

    w_in = w_in_e[0][:, :main].astype(BF16)
    w_dt = jnp.pad(w_in_e[0][:, main:], ((0, 0), (0, LANE - n_heads))).astype(BF16)
    w_out = w_out_e[0].astype(BF16)
    wq = w_qkv[0][:, :nq].reshape(d, ATT_KV_HEADS, ATT_GROUP, ATT_HEAD_DIM).transpose(0, 2, 1, 3).reshape(d, nq)
    w_qkv_p = jnp.concatenate([wq, w_qkv[0][:, nq:]], axis=1).astype(BF16)
    w_o_p = w_o[0].reshape(ATT_KV_HEADS, ATT_GROUP, ATT_HEAD_DIM, d).transpose(1, 0, 2, 3).reshape(nq, d).astype(BF16)
    w_up_b = w_up.astype(BF16)
    w_down_b = w_down.astype(BF16)

    scw = sc_conv_w[0]
    cw = ssd_conv_w[0]
    cb = ssd_conv_b[0].reshape(1, conv_dim)
    dtb = _pad_lanes(ssd_dt_bias[0])
    alog = _pad_lanes(ssd_a_log[0])
    dexp = jnp.repeat(ssd_d[0], SSD_HEAD_DIM).reshape(1, d_inner)
    ssd_nw = ssd_norm_w[0].reshape(1, d_inner)

    mod_p = ada[0, bs:bs + bp].reshape(bp, 1, 6 * d)
    mod_s = ada[0, :bs].reshape(1, bs, 6 * d)
    nw_mix = norm_mix_w[0].reshape(1, d)
    nw_mlp = norm_mlp_w[0].reshape(1, d)
    fw = norm_final_w.reshape(1, d)

    proj_p, dt_p = _nm_matmul(xp, nw_mix, mod_p, 0, 1, w_in, w_dt, seq, min(seq, 1024), 2048)
    ycat_p, scst_p, cvst_p, ssm_p = _ssd_prompt(proj_p, dt_p, scw, cw, cb, dtb, alog, dexp, ssd_nw, bp, seq)
    xp = _mm_res(ycat_p, w_out, xp, mod_p, 2, seq, 512)
    xp = _mlp(xp, nw_mlp, mod_p, w_up_b[0], w_down_b[0], fw, seq, 512, False)

    proj_s, dt_s = _nm_matmul(xs, nw_mix, mod_s, 0, 1, w_in, w_dt, bs, bs, 1024)
    ysc_s, scn_s, cvn_s, xs_s, b_s, c_s, xdt_t, dec_s = _even_sample_prep(
        proj_s, dt_s, state_sc_conv[0].reshape(bs, -1), state_ssd_conv[0].reshape(bs, -1),
        scw, cw, cb, dtb, alog, d_inner)
    ssm_s, y_t = _ssm_sample(state_ssm[0], xdt_t, dec_s, b_s, c_s)
    xs = _even_out_sample(y_t, xs_s, proj_s, ysc_s, dexp, ssd_nw, w_out, xs, mod_s)
    xs = _mlp(xs, nw_mlp, mod_s, w_up_b[0], w_down_b[0], fw, bs, bs, False)

    mod_p = ada[1, bs:bs + bp].reshape(bp, 1, 6 * d)
    mod_s = ada[1, :bs].reshape(1, bs, 6 * d)
    nw_mix = norm_mix_w[1].reshape(1, d)
    nw_mlp = norm_mlp_w[1].reshape(1, d)

    tabs_p = _rope_tables(seq, 0, 1)
    q_p, kt_p, vt_p, km_p, klo_p, khi_p, vtb_p = _qkv(xp, nw_mix, mod_p, w_qkv_p, tabs_p, seq, MOBA_BLOCK, True)
    n_blk = seq // MOBA_BLOCK
    kmean_p = jnp.pad(km_p.reshape(bp, n_blk, kvw), ((0, 0), (0, SEL_LANES - n_blk), (0, 0)))
    o_p = _moba_prompt(q_p, klo_p, khi_p, vtb_p, kmean_p, bp, seq)
    xp = _mm_res(o_p, w_o_p, xp, mod_p, 2, seq, 512)
    y_prompt = _mlp(xp, nw_mlp, mod_p, w_up_b[1], w_down_b[1], fw, seq, 512, True)

    tabs_s = _rope_tables(bs, past_len, 0)
    q_s, k_s, v_s = _qkv(xs, nw_mix, mod_s, w_qkv_p, tabs_s, bs, bs, False)
    o_s = _moba_sample(q_s, k_s, v_s, cache_k[0].transpose(0, 2, 3, 1), cache_v[0].transpose(0, 2, 3, 1), page_table)
    xs = _mm_res(o_s.reshape(bs, nq).astype(BF16), w_o_p, xs, mod_s, 2, bs, bs)
    y_sample = _mlp(xs, nw_mlp, mod_s, w_up_b[1], w_down_b[1], fw, bs, bs, True)

    def kv_out(t):
        return t.reshape(bp, ATT_KV_HEADS, ATT_HEAD_DIM, seq).transpose(0, 3, 1, 2)[None]

    return (y_prompt.reshape(bp, seq, d), y_sample.reshape(bs, 1, d),
            scst_p[:, SUBLANE - 2:, :][None], scn_s.reshape(1, bs, 2, dsc),
            cvst_p[:, SUBLANE - 3:, :][None], cvn_s.reshape(1, bs, 3, conv_dim),
            ssm_p.reshape(1, bp, n_heads, SSD_HEAD_DIM, SSD_STATE), ssm_s[None],
            kv_out(kt_p), k_s.reshape(1, bs, 1, ATT_KV_HEADS, ATT_HEAD_DIM),
            kv_out(vt_p), v_s.reshape(1, bs, 1, ATT_KV_HEADS, ATT_HEAD_DIM))
```

```python
import functools
import math

import jax
import jax.numpy as jnp
from jax import lax
from jax.experimental import pallas as pl
from jax.experimental.pallas import tpu as pltpu

F32 = jnp.float32
BF16 = jnp.bfloat16
HIGHEST = lax.Precision.HIGHEST

EPS = 1e-6
LANE = 128
SUBLANE = 8
BF16_SUBLANE = 16
SSD_HEAD_DIM = 64
SSD_GROUPS = 4
SSD_STATE = 128
SSD_CHUNK = 256
ATT_HEADS = 16
ATT_KV_HEADS = 4
ATT_GROUP = ATT_HEADS // ATT_KV_HEADS
ATT_HEAD_DIM = 64
ROT_DIM = ATT_HEAD_DIM // 4
ROPE_THETA = 500000.0
MOBA_BLOCK = 256
MOBA_TOPK = 3
SEL_LANES = 16
NEG_BIG = -1e30
VMEM_LIMIT = 56 * 1024 * 1024

NT_DIMS = (((1,), (1,)), ((), ()))


def _cparams(*sem):
    return pltpu.CompilerParams(dimension_semantics=sem, vmem_limit_bytes=VMEM_LIMIT)


def _silu(x):
    return x * jax.nn.sigmoid(x)


def _softplus(x):
    return jnp.maximum(x, 0.0) + jnp.log1p(jnp.exp(-jnp.abs(x)))


def _norm_mod(x, nw, shift, scale):
    ms = jnp.mean(x * x, axis=-1, keepdims=True)
    xn = x * lax.rsqrt(ms + EPS) * nw
    return xn * (1.0 + scale) + shift


def _resident(shape):
    nd = len(shape)
    return pl.BlockSpec(shape, lambda *_: (0,) * nd, pipeline_mode=pl.Buffered(1))


def _ada_kernel(c_ref, w_ref, b_ref, o_ref):
    s = _silu(c_ref[...]).astype(BF16)
    o_ref[0] = jnp.dot(s, w_ref[0].astype(BF16), preferred_element_type=F32) + b_ref[0]


def _ada(c_all, w_ada, b_ada):
    rows, d = c_all.shape
    n_layers, _, n6 = w_ada.shape
    tn = 1024
    return pl.pallas_call(
        _ada_kernel,
        grid=(n_layers, n6 // tn),
        in_specs=[pl.BlockSpec((rows, d), lambda l, j: (0, 0)),
                  pl.BlockSpec((1, d, tn), lambda l, j: (l, 0, j)),
                  pl.BlockSpec((1, 1, tn), lambda l, j: (l, 0, j))],
        out_specs=pl.BlockSpec((1, rows, tn), lambda l, j: (l, 0, j)),
        out_shape=jax.ShapeDtypeStruct((n_layers, rows, n6), F32),
        compiler_params=_cparams("parallel", "parallel"),
        name="adaln",
    )(c_all, w_ada, b_ada.reshape(n_layers, 1, n6))


def _mod_spec(mod, k, rows_per_group, tm):
    _, r, n6 = mod.shape
    d = n6 // 6
    tiles_per_group = max(rows_per_group // tm, 1)
    return pl.BlockSpec((1, r, d), lambda i, *_: (i // tiles_per_group, 0, k))


def _nm_matmul_kernel(x_ref, nw_ref, sh_ref, sc_ref, w_ref, w2_ref, o_ref, o2_ref, h_scr):
    @pl.when(pl.program_id(1) == 0)
    def _():
        hb = _norm_mod(x_ref[...], nw_ref[...], sh_ref[0], sc_ref[0]).astype(BF16)
        h_scr[...] = hb
        o2_ref[...] = jnp.dot(hb, w2_ref[...], preferred_element_type=F32)

    o_ref[...] = jnp.dot(h_scr[...], w_ref[...], preferred_element_type=F32)


def _nm_matmul(x, nw, mod, k_shift, k_scale, w, w2, rows_per_group, tm, tn):
    t, d = x.shape
    n = w.shape[1]
    n2 = w2.shape[1]
    return pl.pallas_call(
        _nm_matmul_kernel,
        grid=(t // tm, n // tn),
        in_specs=[pl.BlockSpec((tm, d), lambda i, j: (i, 0)),
                  pl.BlockSpec((1, d), lambda i, j: (0, 0)),
                  _mod_spec(mod, k_shift, rows_per_group, tm),
                  _mod_spec(mod, k_scale, rows_per_group, tm),
                  pl.BlockSpec((d, tn), lambda i, j: (0, j)),
                  pl.BlockSpec((d, n2), lambda i, j: (0, 0))],
        out_specs=[pl.BlockSpec((tm, tn), lambda i, j: (i, j)),
                   pl.BlockSpec((tm, n2), lambda i, j: (i, 0))],
        out_shape=[jax.ShapeDtypeStruct((t, n), F32), jax.ShapeDtypeStruct((t, n2), F32)],
        scratch_shapes=[pltpu.VMEM((tm, d), BF16)],
        compiler_params=_cparams("parallel", "arbitrary"),
        name="in_proj",
    )(x, nw, mod, mod, w, w2)


def _mm_res_kernel(a_ref, w_ref, res_ref, g_ref, o_ref):
    o_ref[...] = res_ref[...] + g_ref[0] * jnp.dot(a_ref[...], w_ref[...], preferred_element_type=F32)


def _mm_res(a, w, res, mod, k_gate, rows_per_group, tm):
    t, kdim = a.shape
    d = w.shape[1]
    return pl.pallas_call(
        _mm_res_kernel,
        grid=(t // tm,),
        in_specs=[pl.BlockSpec((tm, kdim), lambda i: (i, 0)),
                  _resident((kdim, d)),
                  pl.BlockSpec((tm, d), lambda i: (i, 0)),
                  _mod_spec(mod, k_gate, rows_per_group, tm)],
        out_specs=pl.BlockSpec((tm, d), lambda i: (i, 0)),
        out_shape=jax.ShapeDtypeStruct((t, d), F32),
        compiler_params=_cparams("parallel"),
        name="proj_residual",
    )(a, w, res, mod)


def _mlp_kernel(x_ref, nw_ref, sh_ref, sc_ref, g_ref, wu_ref, wd_ref, fw_ref, o_ref, *, final_norm, tf):
    x = x_ref[...]
    hb = _norm_mod(x, nw_ref[...], sh_ref[0], sc_ref[0]).astype(BF16)
    acc = jnp.zeros(x.shape, F32)
    for f in range(wu_ref.shape[1] // tf):
        a = jnp.maximum(jnp.dot(hb, wu_ref[:, f * tf:(f + 1) * tf], preferred_element_type=F32), 0.0)
        acc = acc + jnp.dot((a * a).astype(BF16), wd_ref[f * tf:(f + 1) * tf, :], preferred_element_type=F32)
    y = x + g_ref[0] * acc
    if final_norm:
        ms = jnp.mean(y * y, axis=-1, keepdims=True)
        y = y * lax.rsqrt(ms + EPS) * fw_ref[...]
    o_ref[...] = y


def _mlp(x, nw, mod, w_up, w_down, fw, rows_per_group, tm, final_norm):
    t, d = x.shape
    dff = w_up.shape[1]
    return pl.pallas_call(
        functools.partial(_mlp_kernel, final_norm=final_norm, tf=1024),
        grid=(t // tm,),
        in_specs=[pl.BlockSpec((tm, d), lambda i: (i, 0)),
                  pl.BlockSpec((1, d), lambda i: (0, 0)),
                  _mod_spec(mod, 3, rows_per_group, tm),
                  _mod_spec(mod, 4, rows_per_group, tm),
                  _mod_spec(mod, 5, rows_per_group, tm),
                  _resident((d, dff)),
                  _resident((dff, d)),
                  pl.BlockSpec((1, d), lambda i: (0, 0))],
        out_specs=pl.BlockSpec((tm, d), lambda i: (i, 0)),
        out_shape=jax.ShapeDtypeStruct((t, d), F32),
        compiler_params=_cparams("parallel"),
        name="mlp",
    )(x, nw, mod, mod, mod, w_up, w_down, fw)


def _causal_conv(win_ref, cols, cur, w):
    q = cur.shape[0]
    kk = w.shape[0]
    win_ref[SUBLANE:, cols] = cur
    out = cur * w[kk - 1:kk, :]
    for s in range(1, kk):
        out = out + win_ref[SUBLANE - s:SUBLANE - s + q, cols] * w[kk - 1 - s:kk - s, :]
    win_ref[:SUBLANE, cols] = cur[q - SUBLANE:, :]
    return out

def _expand_heads(m, h0, n_cols):
    rows = m.shape[0]
    lane = lax.broadcasted_iota(jnp.int32, (rows, LANE), 1)
    cols = []
    for c in range(n_cols):
        lo = jnp.broadcast_to(m[:, h0 + 2 * c:h0 + 2 * c + 1], (rows, LANE))
        hi = jnp.broadcast_to(m[:, h0 + 2 * c + 1:h0 + 2 * c + 2], (rows, LANE))
        cols.append(jnp.where(lane < SSD_HEAD_DIM, lo, hi))
    return jnp.concatenate(cols, axis=1)


def _ssd_prompt_kernel(scb_ref, scc_ref, scx_ref, z0_ref, z1_ref, x0_ref, x1_ref, bc_ref, dt_ref,
                       scw_ref, cw_ref, cb_ref, dtb_ref, alog_ref, dexp_ref, nw_ref,
                       y_ref, scst_ref, cvst_ref, ssm_ref,
                       uwin, xwin, s_scr):
    c = pl.program_id(1)
    q = scb_ref.shape[0]
    dsc = scb_ref.shape[1]
    gw = 2 * x0_ref.shape[1] // SSD_GROUPS
    hpg = gw // SSD_HEAD_DIM

    @pl.when(c == 0)
    def _():
        uwin[:SUBLANE, :] = jnp.zeros((SUBLANE, uwin.shape[1]), F32)
        xwin[:SUBLANE, :] = jnp.zeros((SUBLANE, xwin.shape[1]), F32)
        s_scr[...] = jnp.zeros(s_scr.shape, F32)

    u = scc_ref[...] * scx_ref[...]
    y_sc = scb_ref[...] * _causal_conv(uwin, slice(0, dsc), u, scw_ref[...])
    y_ref[:, :dsc] = y_sc.astype(BF16)

    bc = bc_ref[...]
    n_bc = bc.shape[1]
    o_bc = 2 * x0_ref.shape[1]
    bc_cols = slice(o_bc, o_bc + n_bc)
    bc_c = _silu(_causal_conv(xwin, bc_cols, bc, cw_ref[:, bc_cols]) + cb_ref[:, bc_cols])
    half = n_bc // 2

    dt = _softplus(dt_ref[...] + dtb_ref[...])
    a = dt * (-jnp.exp(alog_ref[...]))
    ri = lax.broadcasted_iota(jnp.int32, (q, q), 0)
    ci = lax.broadcasted_iota(jnp.int32, (q, q), 1)
    tril = ri >= ci
    a_cs = jnp.dot(tril.astype(F32), a, precision=HIGHEST, preferred_element_type=F32)
    a_cs_t = a_cs.T
    quad = 4 * SSD_HEAD_DIM
    hq = q // 2
    lane_head =lax.broadcasted_iota(jnp.int32, (q, quad), 1) // SSD_HEAD_DIM

    for g in range(SSD_GROUPS):
        x_ref = x0_ref if g < SSD_GROUPS // 2 else x1_ref
        z_ref = z0_ref if g < SSD_GROUPS // 2 else z1_ref
        lo = (g % (SSD_GROUPS // 2)) * gw
        go = g * gw
        xg = x_ref[:, lo:lo + gw]
        xs = _silu(_causal_conv(xwin, slice(go, go + gw), xg, cw_ref[:, go:go + gw]) + cb_ref[:, go:go + gw])

        n_cols = gw // LANE
        dt_x = _expand_heads(dt, g * hpg, n_cols)
        a_x = _expand_heads(a_cs, g * hpg, n_cols)
        ea = jnp.exp(a_x)
        dte = jnp.exp(a_x[q - 1:q, :] - a_x)
        xdt = xs * dt_x
        xdt_b = xdt.astype(BF16)

        bg = bc_c[:, g * SSD_STATE:(g + 1) * SSD_STATE]
        cg_b = bc_c[:, half + g * SSD_STATE:half + (g + 1) * SSD_STATE].astype(BF16)
        cb_mat = lax.dot_general(cg_b, bg.astype(BF16), NT_DIMS, preferred_element_type=F32)
        s_g = s_scr[:, go:go + gw]
        y_off = jnp.dot(cg_b, s_g.astype(BF16), preferred_element_type=F32) * ea

        y_quads = []
        for qd in range(gw // quad):
            xq = xdt[:, qd * quad:(qd + 1) * quad]
            acc = [None, None]
            for i in range(4):
                h = g * hpg + qd * 4 + i
                xm = jnp.where(lane_head == i, xq, 0.0).astype(BF16)
                for r, (rows, cols) in enumerate(((slice(0, hq), slice(0, hq)), (slice(hq, q), slice(0, q)))):
                    seg = a_cs[rows, h:h + 1] - a_cs_t[h:h + 1, cols]
                    dec = jnp.exp(jnp.where(tril[rows, cols], seg, -jnp.inf))
                    w_h = (cb_mat[rows, cols] * dec).astype(BF16)
                    d = jnp.dot(w_h, xm[cols], preferred_element_type=F32)
                    acc[r] = d if acc[r] is None else acc[r] + d
            y_quads.append(jnp.concatenate(acc, axis=0))
        y = jnp.concatenate(y_quads, axis=1) + y_off + dexp_ref[:, go:go + gw] * xs

        zg = z_ref[:, lo:lo + gw]
        y = y * _silu(zg)
        ms = jnp.mean(y * y, axis=-1, keepdims=True)
        y = y * lax.rsqrt(ms + EPS) * nw_ref[:, go:go + gw]
        y_ref[:, dsc + go:dsc + go + gw] = y.astype(BF16)

        s_new = s_g * ea[q - 1:q, :] + jnp.dot(bg.T.astype(BF16), (xdt * dte).astype(BF16),
                                               preferred_element_type=F32)
        s_scr[:, go:go + gw] = s_new

    @pl.when(c == pl.num_programs(1) - 1)
    def _():
        scst_ref[0] = uwin[:SUBLANE, :]
        cvst_ref[0] = xwin[:SUBLANE, :]
        ssm_ref[0] = s_scr[...].T


def _ssd_prompt(proj, dt_raw, scw, cw, cb, dtb, alog, dexp, nw, bp, seq):
    q = min(SSD_CHUNK, seq)
    nc = seq // q
    dsc = scw.shape[1]
    d_inner = dexp.shape[1]
    conv_dim = cw.shape[1]
    bw = dsc

    def col(kb):
        return pl.BlockSpec((q, bw), lambda b, c: (b * nc + c, kb))

    n_z = d_inner // bw
    specs = [col(0), col(1), col(2)]
    specs += [col(3 + i) for i in range(n_z)]
    specs += [col(3 + n_z + i) for i in range(n_z)]
    specs += [col(3 + 2 * n_z)]
    specs += [pl.BlockSpec((q, LANE), lambda b, c: (b * nc + c, 0))]
    for arr in (scw, cw, cb, dtb, alog, dexp, nw):
        specs.append(pl.BlockSpec(arr.shape, lambda b, c: (0, 0)))
    return pl.pallas_call(
        _ssd_prompt_kernel,
        grid=(bp, nc),
        in_specs=specs,
        out_specs=[pl.BlockSpec((q, dsc + d_inner), lambda b, c: (b * nc + c, 0)),
                   pl.BlockSpec((1, SUBLANE, dsc), lambda b, c: (b, 0, 0)),
                   pl.BlockSpec((1, SUBLANE, conv_dim), lambda b, c: (b, 0, 0)),
                   pl.BlockSpec((1, d_inner, SSD_STATE), lambda b, c: (b, 0, 0))],
        out_shape=[jax.ShapeDtypeStruct((bp * seq, dsc + d_inner), BF16),
                   jax.ShapeDtypeStruct((bp, SUBLANE, dsc), F32),
                   jax.ShapeDtypeStruct((bp, SUBLANE, conv_dim), F32),
                   jax.ShapeDtypeStruct((bp, d_inner, SSD_STATE), F32)],
        scratch_shapes=[pltpu.VMEM((SUBLANE + q, dsc), F32),
                        pltpu.VMEM((SUBLANE + q, conv_dim), F32),
                        pltpu.VMEM((SSD_STATE, d_inner), F32)],
        compiler_params=_cparams("parallel", "arbitrary"),
        name="ssd_prompt",
    )(*([proj] * (4 + 2 * n_z)), dt_raw, scw, cw, cb, dtb, alog, dexp, nw)


def _even_sample_prep_kernel(proj_ref, dt_ref, scst_ref, cvst_ref, scw_ref, cw_ref, cb_ref, dtb_ref, alog_ref,
                             ysc_ref, scn_ref, cvn_ref, xs_ref, b_ref, c_ref, xdt_t_ref, dec_ref):
    dsc = scw_ref.shape[1]
    conv_dim = cw_ref.shape[1]
    d_inner = xs_ref.shape[1]
    n_heads = d_inner // SSD_HEAD_DIM
    scb = proj_ref[:, :dsc]
    u = proj_ref[:, dsc:2 * dsc] * proj_ref[:, 2 * dsc:3 * dsc]
    xbc = proj_ref[:, 3 * dsc + d_inner:3 * dsc + d_inner + conv_dim]
    b0 = scst_ref[:, :dsc]
    b1 = scst_ref[:, dsc:]
    ysc_ref[...] = scb * (b0 * scw_ref[0:1, :] + b1 * scw_ref[1:2, :] + u * scw_ref[2:3, :])
    scn_ref[:, :dsc] = b1
    scn_ref[:, dsc:] = u

    c0 = cvst_ref[:, :conv_dim]
    c1 = cvst_ref[:, conv_dim:2 * conv_dim]
    c2 = cvst_ref[:, 2 * conv_dim:]
    act = _silu(c0 * cw_ref[0:1, :] + c1 * cw_ref[1:2, :] + c2 * cw_ref[2:3, :] + xbc * cw_ref[3:4, :] + cb_ref[...])
    cvn_ref[:, :conv_dim] = c1
    cvn_ref[:, conv_dim:2 * conv_dim] = c2
    cvn_ref[:, 2 * conv_dim:] = xbc
    xs = act[:, :d_inner]
    n_bc = (conv_dim - d_inner) // 2
    xs_ref[...] = xs
    b_ref[...] = act[:, d_inner:d_inner + n_bc]
    c_ref[...] = act[:, d_inner + n_bc:]

    dt = _softplus(dt_ref[...] + dtb_ref[...])
    dec_ref[...] = jnp.exp(dt * (-jnp.exp(alog_ref[...])))
    xdt_t_ref[...] = (xs * _expand_heads(dt, 0, n_heads // 2)).T


def _even_sample_prep(proj, dt_raw, scst, cvst, scw, cw, cb, dtb, alog, d_inner):
    bs = proj.shape[0]
    dsc = scw.shape[1]
    conv_dim = cw.shape[1]
    n_bc = (conv_dim - d_inner) // 2
    shapes = [(bs, dsc), (bs, 2 * dsc), (bs, 3 * conv_dim), (bs, d_inner), (bs, n_bc), (bs, n_bc),
              (d_inner, bs), (bs, LANE)]
    return pl.pallas_call(
        _even_sample_prep_kernel,
        out_shape=[jax.ShapeDtypeStruct(s, F32) for s in shapes],
        compiler_params=pltpu.CompilerParams(vmem_limit_bytes=VMEM_LIMIT),
        name="even_sample_prep",
    )(proj, dt_raw, scst, cvst, scw, cw, cb, dtb, alog)


def _ssm_sample_kernel(dec_ref, h0_ref, xt_ref, b_ref, c_ref, hn_ref, yt_ref):
    h = pl.program_id(0)
    bs = h0_ref.shape[0]
    p = h0_ref.shape[2]
    n = h0_ref.shape[3]
    lane = lax.broadcasted_iota(jnp.int32, (p, bs), 1)
    xt = xt_ref[...]
    ones = jnp.ones((n, bs), BF16)
    yt = jnp.zeros((p, bs), F32)
    for b in range(bs):
        xcol = jnp.broadcast_to(xt[:, b:b + 1], (p, n))
        hn = h0_ref[b, 0] * dec_ref[b, h] + xcol * b_ref[b:b + 1, :]
        hn_ref[b, 0] = hn
        ysum = jnp.zeros((p, bs), F32)
        rest = hn * c_ref[b:b + 1, :]
        for _ in range(3):
            piece = rest.astype(BF16)
            ysum = ysum + jnp.dot(piece, ones, preferred_element_type=F32)
            rest = rest - piece.astype(F32)
        yt = jnp.where(lane == b, ysum, yt)
    yt_ref[...] = yt


def _ssm_sample(h0, xdt_t, dec, bm, cm):
    bs, n_heads, p, n = h0.shape
    hpg = n_heads // SSD_GROUPS
    return pl.pallas_call(
        _ssm_sample_kernel,
        grid=(n_heads,),
        in_specs=[pl.BlockSpec(memory_space=pltpu.SMEM),
                  pl.BlockSpec((bs, 1, p, n), lambda h: (0, h, 0, 0)),
                  pl.BlockSpec((p, bs), lambda h: (h, 0)),
                  pl.BlockSpec((bs, n), lambda h: (0, h // hpg)),
                  pl.BlockSpec((bs, n), lambda h: (0, h // hpg))],
        out_specs=[pl.BlockSpec((bs, 1, p, n), lambda h: (0, h, 0, 0)),
                   pl.BlockSpec((p, bs), lambda h: (h, 0))],
        out_shape=[jax.ShapeDtypeStruct(h0.shape, F32), jax.ShapeDtypeStruct((n_heads * p, bs), F32)],
        compiler_params=_cparams("parallel"),
        name="ssm_sample",
    )(dec, h0, xdt_t, bm, cm)


def _even_out_sample_kernel(yt_ref, xs_ref, proj_ref, ysc_ref, dexp_ref, nw_ref, w_ref, x_ref, g_ref, o_ref):
    d_inner = xs_ref.shape[1]
    dsc = ysc_ref.shape[1]
    gw = d_inner // SSD_GROUPS
    y = yt_ref[...].T + dexp_ref[...] * xs_ref[...]
    y = y * _silu(proj_ref[:, 3 * dsc:3 * dsc + d_inner])
    parts = []
    for g in range(SSD_GROUPS):
        yg = y[:, g * gw:(g + 1) * gw]
        ms = jnp.mean(yg * yg, axis=-1, keepdims=True)
        parts.append(yg * lax.rsqrt(ms + EPS))
    y_ssd = jnp.concatenate(parts, axis=1) * nw_ref[...]
    cat = jnp.concatenate([ysc_ref[...], y_ssd], axis=1).astype(BF16)
    o_ref[...] = x_ref[...] + g_ref[0] * jnp.dot(cat, w_ref[...], preferred_element_type=F32)


def _even_out_sample(y_t, xs, proj, ysc, dexp, nw, w_out, x, mod):
    bs, d = x.shape
    full = lambda a: pl.BlockSpec(a.shape, lambda i: (0,) * a.ndim)
    return pl.pallas_call(
        _even_out_sample_kernel,
        grid=(1,),
        in_specs=[full(y_t), full(xs), full(proj), full(ysc), full(dexp), full(nw), full(w_out), full(x),
                  pl.BlockSpec((1, bs, d), lambda i: (0, 0, 2))],
        out_specs=pl.BlockSpec((bs, d), lambda i: (0, 0)),
        out_shape=jax.ShapeDtypeStruct((bs, d), F32),
        compiler_params=_cparams("arbitrary"),
        name="even_out_sample",
    )(y_t, xs, proj, ysc, dexp, nw, w_out, x, mod)


def _rope_table_kernel(ca_ref, cb_ref, cc_ref, *, pos0, stride):
    rows = ca_ref.shape[0]
    row = lax.broadcasted_iota(jnp.int32, (rows, LANE), 0) + pl.program_id(0) * rows
    d = lax.broadcasted_iota(jnp.int32, (rows, LANE), 1) % ATT_HEAD_DIM
    half = ROT_DIM // 2
    pos = (pos0 + row * stride).astype(F32)
    inv = jnp.power(jnp.full((rows, LANE), ROPE_THETA, F32), -((d % half).astype(F32) * 2.0 / ROT_DIM))
    ang = pos * inv
    cs = jnp.cos(ang)
    sn = jnp.sin(ang)
    ca_ref[...] = jnp.where(d < ROT_DIM, cs, 1.0)
    cb_ref[...] = jnp.where(d < half, -sn, 0.0)
    cc_ref[...] = jnp.where(d < half, 0.0, jnp.where(d < ROT_DIM, sn, 0.0))


def _rope_tables(rows, pos0, stride):
    tr = min(rows, 512)
    spec = pl.BlockSpec((tr, LANE), lambda i: (i, 0))
    return pl.pallas_call(
        functools.partial(_rope_table_kernel, pos0=pos0, stride=stride),
        grid=(rows // tr,),
        out_specs=[spec, spec, spec],
        out_shape=[jax.ShapeDtypeStruct((rows, LANE), F32)] * 3,
        compiler_params=_cparams("parallel"),
        name="rope_tables",
    )()


def _rope(t, ca, cb, cc):
    cols = []
    for c in range(t.shape[1] // LANE):
        s = t[:, c * LANE:(c + 1) * LANE]
        cols.append(s * ca + pltpu.roll(s, LANE - ROT_DIM // 2, axis=1) * cb + pltpu.roll(s, ROT_DIM // 2, axis=1) * cc)
    return jnp.concatenate(cols, axis=1)


def _sel_keys(k, j):
    lane = lax.broadcasted_iota(jnp.int32, (k.shape[0], LANE), 1)
    onehot = jnp.where(lane == j, 1.0, 0.0)
    in_sel = lane < SEL_LANES
    k_lo = jnp.concatenate([jnp.where(in_sel, onehot, k[:, :LANE]), k[:, LANE:]], axis=1)
    k_hi = jnp.concatenate([k[:, :LANE], jnp.where(in_sel, onehot, k[:, LANE:])], axis=1)
    return k_lo.astype(BF16), k_hi.astype(BF16)


def _qkv_kernel(x_ref, nw_ref, sh_ref, sc_ref, w_ref, ca_ref, cb_ref, cc_ref, q_ref, k_ref, v_ref, *blk_refs,
                blocks_per_seq):
    nq = q_ref.shape[1]
    nk = (w_ref.shape[1] - nq) // 2
    hb = _norm_mod(x_ref[...], nw_ref[...], sh_ref[0], sc_ref[0]).astype(BF16)
    qkv = jnp.dot(hb, w_ref[...], preferred_element_type=F32)
    ca, cb, cc = ca_ref[...], cb_ref[...], cc_ref[...]
    q_ref[...] = _rope(qkv[:, :nq], ca, cb, cc)
    k = _rope(qkv[:, nq:nq + nk], ca, cb, cc)
    v = qkv[:, nq + nk:]
    if blocks_per_seq:
        km_ref, klo_ref, khi_ref, vt_ref = blk_refs
        vt = v.T
        k_ref[0] = k.T
        v_ref[0] = vt
        km_ref[0] = jnp.mean(k, axis=0, keepdims=True)
        klo_ref[...], khi_ref[...] = _sel_keys(k, pl.program_id(0) % blocks_per_seq)
        hd = ATT_HEAD_DIM
        ones_row = jnp.where(lax.broadcasted_iota(jnp.int32, (BF16_SUBLANE, vt.shape[1]), 0) == 0, 1.0, 0.0)
        vt_ref[0] = jnp.concatenate([piece for h in range(nk // hd) for piece in (vt[h * hd:(h + 1) * hd], ones_row)],
                                    axis=0).astype(BF16)
    else:
        k_ref[...] = k
        v_ref[...] = v


def _qkv(x, nw, mod, w, tabs, rows_per_group, tm, per_block):
    t, d = x.shape
    nq = ATT_HEADS * ATT_HEAD_DIM
    nk = ATT_KV_HEADS * ATT_HEAD_DIM
    tab_tiles = tabs[0].shape[0] // tm
    tab = pl.BlockSpec((tm, LANE), lambda i: (i % tab_tiles, 0))
    row = lambda n: pl.BlockSpec((tm, n), lambda i: (i, 0))
    if per_block:
        bps = rows_per_group // tm
        kv_t = pl.BlockSpec((1, nk, tm), lambda i: (i // bps, 0, i % bps))
        v_rows = ATT_KV_HEADS * (ATT_HEAD_DIM + BF16_SUBLANE)
        out_specs = [row(nq), kv_t, kv_t, pl.BlockSpec((1, 1, nk), lambda i: (i, 0, 0)), row(nk), row(nk),
                     pl.BlockSpec((1, v_rows, tm), lambda i: (i, 0, 0))]
        out_shape = [jax.ShapeDtypeStruct((t, nq), F32),
                     jax.ShapeDtypeStruct((t // rows_per_group, nk, rows_per_group), F32),
                     jax.ShapeDtypeStruct((t // rows_per_group, nk, rows_per_group), F32),
                     jax.ShapeDtypeStruct((t // tm, 1, nk), F32), jax.ShapeDtypeStruct((t, nk), BF16),
                     jax.ShapeDtypeStruct((t, nk), BF16), jax.ShapeDtypeStruct((t // tm, v_rows, tm), BF16)]
    else:
        out_specs = [row(nq), row(nk), row(nk)]
        out_shape = [jax.ShapeDtypeStruct((t, nq), F32), jax.ShapeDtypeStruct((t, nk), F32),
                     jax.ShapeDtypeStruct((t, nk), F32)]
    return pl.pallas_call(
        functools.partial(_qkv_kernel, blocks_per_seq=rows_per_group // tm if per_block else 0),
        grid=(t // tm,),
        in_specs=[row(d), pl.BlockSpec((1, d), lambda i: (0, 0)),
                  _mod_spec(mod, 0, rows_per_group, tm), _mod_spec(mod, 1, rows_per_group, tm),
                  _resident(w.shape), tab, tab, tab],
        out_specs=out_specs,
        out_shape=out_shape,
        compiler_params=_cparams("parallel"),
        name="qkv_rope",
    )(x, nw, mod, mod, w, *tabs)


def _select_bias(scores, n_past, n_rounds):
    lane = lax.broadcasted_iota(jnp.int32, scores.shape, 1)
    lanef = lane.astype(F32)
    s = jnp.where(lane < n_past, scores, -jnp.inf)
    bias = jnp.full(scores.shape, NEG_BIG, F32)
    for r in range(n_rounds):
        mx = jnp.max(s, axis=1, keepdims=True)
        idx = jnp.min(jnp.where(s == mx, lanef, float(LANE)), axis=1, keepdims=True)
        bias = jnp.where(lanef == jnp.where(n_past > r, idx, -1.0), 0.0, bias)
        s = jnp.where(lanef == idx, -jnp.inf, s)
    return bias


def _select_bias_t(scores, n_past, n_rounds):
    row = lax.broadcasted_iota(jnp.int32, scores.shape, 0)
    rowf = row.astype(F32)
    s = jnp.where(row < n_past, scores, -jnp.inf)
    bias = jnp.full(scores.shape, NEG_BIG, F32)
    for r in range(n_rounds):
        mx = jnp.max(s, axis=0, keepdims=True)
        idx = jnp.min(jnp.where(s == mx, rowf, float(SEL_LANES)), axis=0, keepdims=True)
        bias = jnp.where(rowf == jnp.where(n_past > r, idx, -1.0), 0.0, bias)
        s = jnp.where(rowf == idx, -jnp.inf, s)
    return jnp.where(row == n_past, 0.0, bias)


def _moba_prompt_kernel(q_ref, klo_ref, khi_ref, vt_ref, km_ref, o_ref, qa_scr, m_scr, acc_scr):
    qi = pl.program_id(1)
    blk = q_ref.shape[0]
    kvw = klo_ref.shape[1]
    n_cols = ATT_GROUP * blk // LANE
    lane_h = lax.broadcasted_iota(jnp.int32, (blk, kvw), 1) // ATT_HEAD_DIM
    lane_c = lax.broadcasted_iota(jnp.int32, (LANE, LANE), 1)
    kmean = km_ref[0]
    km_hi = kmean.astype(BF16)
    km_lo = (kmean - km_hi.astype(F32)).astype(BF16)
    n_rounds = min(MOBA_TOPK, klo_ref.shape[0] // blk)
    pad_rows = jnp.zeros((LANE - SEL_LANES, LANE), F32)

    for h in range(ATT_KV_HEADS):
        qm = jnp.concatenate([jnp.where(lane_h == h, q_ref[:, g * kvw:(g + 1) * kvw], 0.0)
                              for g in range(ATT_GROUP)], axis=0)
        qm_b = qm.astype(BF16)
        scores_t = (lax.dot_general(km_hi, qm_b, NT_DIMS, preferred_element_type=F32)
                    + lax.dot_general(km_lo, qm_b, NT_DIMS, preferred_element_type=F32))
        bias_t = _select_bias_t(scores_t, qi, n_rounds)
        for c in range(n_cols):
            bias_c = jnp.concatenate([bias_t[:, c * LANE:(c + 1) * LANE], pad_rows], axis=0).T
            rows = qm[c * LANE:(c + 1) * LANE, :]
            lo, hi = rows[:, :LANE], rows[:, LANE:]
            if h == 0:
                hi = jnp.where(lane_c < SEL_LANES, bias_c, hi)
            else:
                lo = jnp.where(lane_c < SEL_LANES, bias_c, lo)
            qa_scr[h, c * LANE:(c + 1) * LANE, :] = jnp.concatenate([lo, hi], axis=1).astype(BF16)

    n_q = ATT_GROUP * blk
    v_rows = vt_ref.shape[1] // ATT_KV_HEADS
    key_row = lax.broadcasted_iota(jnp.int32, (blk, n_q), 0)
    q_pos = lax.broadcasted_iota(jnp.int32, (blk, n_q), 1) % blk

    def scores(h, j):
        off = pl.multiple_of(j * blk, blk)
        ka = (khi_ref if h == 0 else klo_ref)[pl.ds(off, blk), :]
        return lax.dot_general(ka, qa_scr[h], NT_DIMS, preferred_element_type=F32)

    def softmax(h, s, first):
        if first:
            s = jnp.where(key_row <= q_pos, s, NEG_BIG)
        mx = jnp.max(s, axis=0, keepdims=True)
        if first:
            m_new, alpha = mx, None
        else:
            m_old = m_scr[h, 0:1, :]
            m_new = jnp.maximum(m_old, mx)
            alpha = jnp.exp(m_old - m_new)
        m_scr[h] = jnp.broadcast_to(m_new, (SUBLANE, n_q))
        return jnp.exp((s - m_new).astype(BF16)), alpha

    def values(h, j, p, alpha):
        vt_h = vt_ref[j, h * v_rows:(h + 1) * v_rows, :]
        pv = jnp.dot(vt_h, p, preferred_element_type=F32)
        acc_scr[h] = pv if alpha is None else acc_scr[h] * alpha + pv

    def block_step(j, first):
        s_next = scores(0, j)
        for h in range(ATT_KV_HEADS):
            s = s_next
            if h + 1 < ATT_KV_HEADS:
                s_next = scores(h + 1, j)
            p, alpha = softmax(h, s, first)
            values(h, j, p, alpha)

    block_step(qi, True)

    def body(j, carry):
        block_step(j, False)
        return carry

    lax.fori_loop(0, qi, body, 0)

    hd = ATT_HEAD_DIM
    for g in range(ATT_GROUP):
        cols = slice(g * blk, (g + 1) * blk)
        o_t = jnp.concatenate([acc_scr[h, :hd, cols] / acc_scr[h, hd:hd + 1, cols]
                               for h in range(ATT_KV_HEADS)], axis=0)
        o_ref[:, g * kvw:(g + 1) * kvw] = o_t.T.astype(BF16)


def _moba_prompt(q, k_lo, k_hi, v_t, kmean, bp, seq):
    blk = MOBA_BLOCK
    nq = seq // blk
    kvw = k_lo.shape[1]
    n_q = ATT_GROUP * blk
    v_rows = v_t.shape[1]
    return pl.pallas_call(
        _moba_prompt_kernel,
        grid=(bp, nq),
        in_specs=[pl.BlockSpec((blk, q.shape[1]), lambda b, i: (b * nq + i, 0)),
                  pl.BlockSpec((seq, kvw), lambda b, i: (b, 0)),
                  pl.BlockSpec((seq, kvw), lambda b, i: (b, 0)),
                  pl.BlockSpec((nq, v_rows, blk), lambda b, i: (b, 0, 0)),
                  pl.BlockSpec((1, SEL_LANES, kvw), lambda b, i: (b, 0, 0))],
        out_specs=pl.BlockSpec((blk, q.shape[1]), lambda b, i: (b * nq + i, 0)),
        out_shape=jax.ShapeDtypeStruct(q.shape, BF16),
        scratch_shapes=[pltpu.VMEM((ATT_KV_HEADS, n_q, kvw), BF16),
                        pltpu.VMEM((ATT_KV_HEADS, SUBLANE, n_q), F32),
                        pltpu.VMEM((ATT_KV_HEADS, v_rows // ATT_KV_HEADS, n_q), F32)],
        compiler_params=_cparams("parallel", "arbitrary"),
        name="moba_prompt",
    )(q, k_lo, k_hi, v_t, kmean)


def _moba_sample_kernel(pt_ref, q_ref, kn_ref, vn_ref, avg_ref, *rest, n_pages):
    k_pages = rest[:n_pages]
    v_pages = rest[n_pages:2 * n_pages]
    o_ref = rest[2 * n_pages]
    hd = ATT_HEAD_DIM
    kvw = kn_ref.shape[2]
    page = k_pages[0].shape[3]
    n_past = n_pages * page // MOBA_BLOCK
    n_rows = ATT_KV_HEADS * SUBLANE
    q = q_ref[0]
    kn = kn_ref[0]
    vn = vn_ref[0]
    row = lax.broadcasted_iota(jnp.int32, (n_rows, hd), 0)
    q_rows = jnp.zeros((n_rows, hd), F32)
    kn_rows = jnp.zeros((n_rows, hd), F32)
    vn_rows = jnp.zeros((n_rows, hd), F32)
    for h in range(ATT_KV_HEADS):
        in_h = row // SUBLANE == h
        kn_rows = jnp.where(in_h, jnp.broadcast_to(kn[:, h * hd:(h + 1) * hd], (n_rows, hd)), kn_rows)
        vn_rows = jnp.where(in_h, jnp.broadcast_to(vn[:, h * hd:(h + 1) * hd], (n_rows, hd)), vn_rows)
        for g in range(ATT_GROUP):
            lo = g * kvw + h * hd
            q_rows = jnp.where(row == h * SUBLANE + g, jnp.broadcast_to(q[:, lo:lo + hd], (n_rows, hd)), q_rows)
    q_b = q_rows.astype(BF16)

    score_parts, logit_parts = [], []
    for h in range(ATT_KV_HEADS):
        rows = slice(h * SUBLANE, (h + 1) * SUBLANE)
        kt = jnp.concatenate([r[0, h] for r in k_pages], axis=1)
        kt_b = kt.astype(BF16)
        kt_lo = (kt - kt_b.astype(F32)).astype(BF16)
        kmt = (jnp.dot(kt_b, avg_ref[...], preferred_element_type=F32)
               + jnp.dot(kt_lo, avg_ref[...], preferred_element_type=F32))
        score_parts.append(jnp.dot(q_rows[rows], kmt, precision=HIGHEST, preferred_element_type=F32))
        logit_parts.append(jnp.dot(q_b[rows], kt_b, preferred_element_type=F32))
    bias = _select_bias(jnp.concatenate(score_parts, axis=0), n_past, min(MOBA_TOPK, n_past + 1))
    logits = jnp.concatenate(logit_parts, axis=0)
    vts = [jnp.concatenate([r[0, h] for r in v_pages], axis=1).astype(BF16) for h in range(ATT_KV_HEADS)]

    lane = lax.broadcasted_iota(jnp.int32, (n_rows, LANE), 1)
    blk_max = jnp.full((n_rows, LANE), NEG_BIG, F32)
    blk_sum = jnp.zeros((n_rows, LANE), F32)
    blk_out = []
    for j in range(n_past):
        keys = slice(j * MOBA_BLOCK, (j + 1) * MOBA_BLOCK)
        lj = logits[:, keys]
        mj = jnp.max(lj, axis=1, keepdims=True)
        pj = jnp.exp(lj - mj)
        blk_max = jnp.where(lane == j, mj, blk_max)
        blk_sum = jnp.where(lane == j, jnp.sum(pj, axis=1, keepdims=True), blk_sum)
        pj_b = pj.astype(BF16)
        blk_out.append(jnp.concatenate(
            [lax.dot_general(pj_b[h * SUBLANE:(h + 1) * SUBLANE], vts[h][:, keys], NT_DIMS, preferred_element_type=F32)
             for h in range(ATT_KV_HEADS)], axis=0))
    l_new = jnp.sum(q_rows * kn_rows, axis=1, keepdims=True)
    chosen = blk_max + bias
    m = jnp.maximum(jnp.max(chosen, axis=1, keepdims=True), l_new)
    w = jnp.exp(chosen - m)
    pn = jnp.exp(l_new - m)
    denom = jnp.sum(w * blk_sum, axis=1, keepdims=True) + pn
    o = pn * vn_rows
    for j in range(n_past):
        o = o + w[:, j:j + 1] * blk_out[j]
    o = o / denom
    for h in range(ATT_KV_HEADS):
        for g in range(ATT_GROUP):
            lo = g * kvw + h * hd
            o_ref[0, :, lo:lo + hd] = o[h * SUBLANE + g:h * SUBLANE + g + 1, :]


def _moba_sample(q, kn, vn, cache_kt, cache_vt, page_table):
    bs = q.shape[0]
    n_pages = page_table.shape[1]
    _, n_kv, hd, page = cache_kt.shape
    kvw = n_kv * hd
    past = n_pages * page
    one = lambda n: pl.BlockSpec((1, 1, n), lambda b, pt: (b, 0, 0))
    avg = jnp.where(jnp.arange(past)[:, None] // MOBA_BLOCK == jnp.arange(LANE)[None, :], 1.0 / MOBA_BLOCK, 0.0).astype(BF16)

    def page_spec(p):
        return pl.BlockSpec((1, n_kv, hd, page), lambda b, pt: (pt[b, p], 0, 0, 0))

    grid_spec = pltpu.PrefetchScalarGridSpec(
        num_scalar_prefetch=1,
        grid=(bs,),
        in_specs=[one(q.shape[1]), one(kvw), one(kvw), pl.BlockSpec((past, LANE), lambda b, pt: (0, 0))]
        + [page_spec(p) for p in range(n_pages)] * 2,
        out_specs=one(q.shape[1]),
    )
    return pl.pallas_call(
        functools.partial(_moba_sample_kernel, n_pages=n_pages),
        grid_spec=grid_spec,
        out_shape=jax.ShapeDtypeStruct((bs, 1, q.shape[1]), F32),
        compiler_params=_cparams("parallel"),
        name="moba_sample",
    )(page_table, q.reshape(bs, 1, -1), kn.reshape(bs, 1, kvw), vn.reshape(bs, 1, kvw), avg,
      *([cache_kt] * n_pages), *([cache_vt] * n_pages))


def _pad_lanes(v, fill=0.0):
    return jnp.pad(v.reshape(1, -1), ((0, 0), (0, LANE - v.shape[0])), constant_values=fill)


def kernel(x_prompt, x_sample, state_sc_conv, state_ssd_conv, state_ssm, cache_k, cache_v, page_table,
           c_prompt, c_sample, norm_mix_w, norm_mlp_w, norm_final_w, w_ada, b_ada, w_in_e, sc_conv_w,
           ssd_conv_w, ssd_conv_b, ssd_dt_bias, ssd_a_log, ssd_d, ssd_norm_w, w_out_e, w_qkv, w_o, w_up, w_down):
    bp, seq, d = x_prompt.shape
    bs, dec_seq, _ = x_sample.shape
    assert dec_seq == 1 and seq % MOBA_BLOCK == 0 and seq % SSD_CHUNK == 0 and seq // MOBA_BLOCK <= SEL_LANES
    n_pages, page = page_table.shape[1], cache_k.shape[2]
    past_len = n_pages * page
    assert past_len % MOBA_BLOCK == 0 and MOBA_BLOCK % page == 0
    dsc = sc_conv_w.shape[2]
    conv_dim = ssd_conv_w.shape[2]
    n_heads = ssd_d.shape[1]
    d_inner = n_heads * SSD_HEAD_DIM
    main = 3 * dsc + d_inner + conv_dim
    kvw = ATT_KV_HEADS * ATT_HEAD_DIM
    nq = ATT_HEADS * ATT_HEAD_DIM

    xp = x_prompt.reshape(bp * seq, d)
    xs = x_sample.reshape(bs, d)

    rows = -(-(bs + bp) // SUBLANE) * SUBLANE
    c_all = jnp.pad(jnp.concatenate([c_sample, c_prompt], axis=0), ((0, rows - bs - bp), (0, 0)))
    ada = _ada(c_all, w_ada, b_ada)

    w_in = w_in_e[0][:, :main].astype(BF16)
    w_dt = jnp.pad(w_in_e[0][:, main:], ((0, 0), (0, LANE - n_heads))).astype(BF16)
    w_out = w_out_e[0].astype(BF16)
    wq = w_qkv[0][:, :nq].reshape(d, ATT_KV_HEADS, ATT_GROUP, ATT_HEAD_DIM).transpose(0, 2, 1, 3).reshape(d, nq)
    w_qkv_p = jnp.concatenate([wq * ATT_HEAD_DIM ** -0.5, w_qkv[0][:, nq:]], axis=1).astype(BF16)
    w_o_p = w_o[0].reshape(ATT_KV_HEADS, ATT_GROUP, ATT_HEAD_DIM, d).transpose(1, 0, 2, 3).reshape(nq, d).astype(BF16)
    w_up_b = w_up.astype(BF16)
    w_down_b = w_down.astype(BF16)

    scw = sc_conv_w[0]
    cw = ssd_conv_w[0]
    cb = ssd_conv_b[0].reshape(1, conv_dim)
    dtb = _pad_lanes(ssd_dt_bias[0])
    alog = _pad_lanes(ssd_a_log[0])
    dexp = jnp.repeat(ssd_d[0], SSD_HEAD_DIM).reshape(1, d_inner)
    ssd_nw = ssd_norm_w[0].reshape(1, d_inner)

    mod_p = ada[0, bs:bs + bp].reshape(bp, 1, 6 * d)
    mod_s = ada[0, :bs].reshape(1, bs, 6 * d)
    nw_mix = norm_mix_w[0].reshape(1, d)
    nw_mlp = norm_mlp_w[0].reshape(1, d)
    fw = norm_final_w.reshape(1, d)

    proj_p, dt_p = _nm_matmul(xp, nw_mix, mod_p, 0, 1, w_in, w_dt, seq, min(seq, 1024), 2048)
    ycat_p, scst_p, cvst_p, ssm_p = _ssd_prompt(proj_p, dt_p, scw, cw, cb, dtb, alog, dexp, ssd_nw, bp, seq)
    xp = _mm_res(ycat_p, w_out, xp, mod_p, 2, seq, 512)
    xp = _mlp(xp, nw_mlp, mod_p, w_up_b[0], w_down_b[0], fw, seq, 512, False)

    proj_s, dt_s = _nm_matmul(xs, nw_mix, mod_s, 0, 1, w_in, w_dt, bs, bs, 1024)
    ysc_s, scn_s, cvn_s, xs_s, b_s, c_s, xdt_t, dec_s = _even_sample_prep(
        proj_s, dt_s, state_sc_conv[0].reshape(bs, -1), state_ssd_conv[0].reshape(bs, -1),
        scw, cw, cb, dtb, alog, d_inner)
    ssm_s, y_t = _ssm_sample(state_ssm[0], xdt_t, dec_s, b_s, c_s)
    xs = _even_out_sample(y_t, xs_s, proj_s, ysc_s, dexp, ssd_nw, w_out, xs, mod_s)
    xs = _mlp(xs, nw_mlp, mod_s, w_up_b[0], w_down_b[0], fw, bs, bs, False)

    mod_p = ada[1, bs:bs + bp].reshape(bp, 1, 6 * d)
    mod_s = ada[1, :bs].reshape(1, bs, 6 * d)
    nw_mix = norm_mix_w[1].reshape(1, d)
    nw_mlp = norm_mlp_w[1].reshape(1, d)

    tabs_p = _rope_tables(seq, 0, 1)
    q_p, kt_p, vt_p, km_p, klo_p, khi_p, vtb_p = _qkv(xp, nw_mix, mod_p, w_qkv_p, tabs_p, seq, MOBA_BLOCK, True)
    n_blk = seq // MOBA_BLOCK
    kmean_p = jnp.pad(km_p.reshape(bp, n_blk, kvw), ((0, 0), (0, SEL_LANES - n_blk), (0, 0)))
    o_p = _moba_prompt(q_p, klo_p, khi_p, vtb_p, kmean_p, bp, seq)
    xp = _mm_res(o_p, w_o_p, xp, mod_p, 2, seq, 512)
    y_prompt = _mlp(xp, nw_mlp, mod_p, w_up_b[1], w_down_b[1], fw, seq, 512, True)

    tabs_s = _rope_tables(bs, past_len, 0)
    q_s, k_s, v_s = _qkv(xs, nw_mix, mod_s, w_qkv_p, tabs_s, bs, bs, False)
    o_s = _moba_sample(q_s, k_s, v_s, cache_k[0].transpose(0, 2, 3, 1), cache_v[0].transpose(0, 2, 3, 1), page_table)
    xs = _mm_res(o_s.reshape(bs, nq).astype(BF16), w_o_p, xs, mod_s, 2, bs, bs)
    y_sample = _mlp(xs, nw_mlp, mod_s, w_up_b[1], w_down_b[1], fw, bs, bs, True)

    def kv_out(t):
        return t.reshape(bp, ATT_KV_HEADS, ATT_HEAD_DIM, seq).transpose(0, 3, 1, 2)[None]

    return (y_prompt.reshape(bp, seq, d), y_sample.reshape(bs, 1, d),
            scst_p[:, SUBLANE - 2:, :][None], scn_s.reshape(1, bs, 2, dsc),
            cvst_p[:, SUBLANE - 3:, :][None], cvn_s.reshape(1, bs, 3, conv_dim),
            ssm_p.reshape(1, bp, n_heads, SSD_HEAD_DIM, SSD_STATE), ssm_s[None],
            kv_out(kt_p), k_s.reshape(1, bs, 1, ATT_KV_HEADS, ATT_HEAD_DIM),
            kv_out(vt_p), v_s.reshape(1, bs, 1, ATT_KV_HEADS, ATT_HEAD_DIM))
```

```python
import functools
import math

import jax
import jax.numpy as jnp
from jax import lax
from jax.experimental import pallas as pl
from jax.experimental.pallas import tpu as pltpu

F32 = jnp.float32
BF16 = jnp.bfloat16
HIGHEST = lax.Precision.HIGHEST

EPS = 1e-6
LANE = 128
SUBLANE = 8
BF16_SUBLANE = 16
SSD_HEAD_DIM = 64
SSD_GROUPS = 4
SSD_STATE = 128
SSD_CHUNK = 256
ATT_HEADS = 16
ATT_KV_HEADS = 4
ATT_GROUP = ATT_HEADS // ATT_KV_HEADS
ATT_HEAD_DIM = 64
ROT_DIM = ATT_HEAD_DIM // 4
ROPE_THETA = 500000.0
MOBA_BLOCK = 256
MOBA_TOPK = 3
SEL_LANES = 16
NEG_BIG = -1e30
VMEM_LIMIT = 56 * 1024 * 1024

NT_DIMS = (((1,), (1,)), ((), ()))


def _cparams(*sem):
    return pltpu.CompilerParams(dimension_semantics=sem, vmem_limit_bytes=VMEM_LIMIT)


def _silu(x):
    return x * jax.nn.sigmoid(x)


def _softplus(x):
    return jnp.maximum(x, 0.0) + jnp.log1p(jnp.exp(-jnp.abs(x)))


def _norm_mod(x, nw, shift, scale):
    ms = jnp.mean(x * x, axis=-1, keepdims=True)
    xn = x * lax.rsqrt(ms + EPS) * nw
    return xn * (1.0 + scale) + shift


def _resident(shape):
    nd = len(shape)
    return pl.BlockSpec(shape, lambda *_: (0,) * nd, pipeline_mode=pl.Buffered(1))


def _ada_kernel(c_ref, w_ref, b_ref, o_ref):
    s = _silu(c_ref[...]).astype(BF16)
    o_ref[0] = jnp.dot(s, w_ref[0].astype(BF16), preferred_element_type=F32) + b_ref[0]


def _ada(c_all, w_ada, b_ada):
    rows, d = c_all.shape
    n_layers, _, n6 = w_ada.shape
    tn = 1024
    return pl.pallas_call(
        _ada_kernel,
        grid=(n_layers, n6 // tn),
        in_specs=[pl.BlockSpec((rows, d), lambda l, j: (0, 0)),
                  pl.BlockSpec((1, d, tn), lambda l, j: (l, 0, j)),
                  pl.BlockSpec((1, 1, tn), lambda l, j: (l, 0, j))],
        out_specs=pl.BlockSpec((1, rows, tn), lambda l, j: (l, 0, j)),
        out_shape=jax.ShapeDtypeStruct((n_layers, rows, n6), F32),
        compiler_params=_cparams("parallel", "parallel"),
        name="adaln",
    )(c_all, w_ada, b_ada.reshape(n_layers, 1, n6))


def _mod_spec(mod, k, rows_per_group, tm):
    _, r, n6 = mod.shape
    d = n6 // 6
    tiles_per_group = max(rows_per_group // tm, 1)
    return pl.BlockSpec((1, r, d), lambda i, *_: (i // tiles_per_group, 0, k))


def _nm_matmul_kernel(x_ref, nw_ref, sh_ref, sc_ref, w_ref, w2_ref, o_ref, o2_ref, h_scr):
    @pl.when(pl.program_id(1) == 0)
    def _():
        hb = _norm_mod(x_ref[...], nw_ref[...], sh_ref[0], sc_ref[0]).astype(BF16)
        h_scr[...] = hb
        o2_ref[...] = jnp.dot(hb, w2_ref[...], preferred_element_type=F32)

    o_ref[...] = jnp.dot(h_scr[...], w_ref[...], preferred_element_type=F32)


def _nm_matmul(x, nw, mod, k_shift, k_scale, w, w2, rows_per_group, tm, tn):
    t, d = x.shape
    n = w.shape[1]
    n2 = w2.shape[1]
    return pl.pallas_call(
        _nm_matmul_kernel,
        grid=(t // tm, n // tn),
        in_specs=[pl.BlockSpec((tm, d), lambda i, j: (i, 0)),
                  pl.BlockSpec((1, d), lambda i, j: (0, 0)),
                  _mod_spec(mod, k_shift, rows_per_group, tm),
                  _mod_spec(mod, k_scale, rows_per_group, tm),
                  pl.BlockSpec((d, tn), lambda i, j: (0, j)),
                  pl.BlockSpec((d, n2), lambda i, j: (0, 0))],
        out_specs=[pl.BlockSpec((tm, tn), lambda i, j: (i, j)),
                   pl.BlockSpec((tm, n2), lambda i, j: (i, 0))],
        out_shape=[jax.ShapeDtypeStruct((t, n), F32), jax.ShapeDtypeStruct((t, n2), F32)],
        scratch_shapes=[pltpu.VMEM((tm, d), BF16)],
        compiler_params=_cparams("parallel", "arbitrary"),
        name="in_proj",
    )(x, nw, mod, mod, w, w2)


def _mm_res_kernel(a_ref, w_ref, res_ref, g_ref, o_ref):
    o_ref[...] = res_ref[...] + g_ref[0] * jnp.dot(a_ref[...], w_ref[...], preferred_element_type=F32)


def _mm_res(a, w, res, mod, k_gate, rows_per_group, tm):
    t, kdim = a.shape
    d = w.shape[1]
    return pl.pallas_call(
        _mm_res_kernel,
        grid=(t // tm,),
        in_specs=[pl.BlockSpec((tm, kdim), lambda i: (i, 0)),
                  _resident((kdim, d)),
                  pl.BlockSpec((tm, d), lambda i: (i, 0)),
                  _mod_spec(mod, k_gate, rows_per_group, tm)],
        out_specs=pl.BlockSpec((tm, d), lambda i: (i, 0)),
        out_shape=jax.ShapeDtypeStruct((t, d), F32),
        compiler_params=_cparams("parallel"),
        name="proj_residual",
    )(a, w, res, mod)


def _mlp_kernel(x_ref, nw_ref, sh_ref, sc_ref, g_ref, wu_ref, wd_ref, fw_ref, o_ref, *, final_norm, tf):
    x = x_ref[...]
    hb = _norm_mod(x, nw_ref[...], sh_ref[0], sc_ref[0]).astype(BF16)
    acc = jnp.zeros(x.shape, F32)
    for f in range(wu_ref.shape[1] // tf):
        a = jnp.maximum(jnp.dot(hb, wu_ref[:, f * tf:(f + 1) * tf], preferred_element_type=F32), 0.0)
        acc = acc + jnp.dot((a * a).astype(BF16), wd_ref[f * tf:(f + 1) * tf, :], preferred_element_type=F32)
    y = x + g_ref[0] * acc
    if final_norm:
        ms = jnp.mean(y * y, axis=-1, keepdims=True)
        y = y * lax.rsqrt(ms + EPS) * fw_ref[...]
    o_ref[...] = y


def _mlp(x, nw, mod, w_up, w_down, fw, rows_per_group, tm, final_norm):
    t, d = x.shape
    dff = w_up.shape[1]
    return pl.pallas_call(
        functools.partial(_mlp_kernel, final_norm=final_norm, tf=1024),
        grid=(t // tm,),
        in_specs=[pl.BlockSpec((tm, d), lambda i: (i, 0)),
                  pl.BlockSpec((1, d), lambda i: (0, 0)),
                  _mod_spec(mod, 3, rows_per_group, tm),
                  _mod_spec(mod, 4, rows_per_group, tm),
                  _mod_spec(mod, 5, rows_per_group, tm),
                  _resident((d, dff)),
                  _resident((dff, d)),
                  pl.BlockSpec((1, d), lambda i: (0, 0))],
        out_specs=pl.BlockSpec((tm, d), lambda i: (i, 0)),
        out_shape=jax.ShapeDtypeStruct((t, d), F32),
        compiler_params=_cparams("parallel"),
        name="mlp",
    )(x, nw, mod, mod, mod, w_up, w_down, fw)


def _shift_rows(cur, tail, k):
    rolled = pltpu.roll(cur, k, axis=0)
    row = lax.broadcasted_iota(jnp.int32, tail.shape, 0)
    top = jnp.where(row < k, pltpu.roll(tail, k, axis=0), rolled[:SUBLANE])
    return jnp.concatenate([top, rolled[SUBLANE:]], axis=0)


def _causal_conv(cur, tail, w):
    kk = w.shape[0]
    out = cur * w[kk - 1:kk, :]
    for s in range(1, kk):
        out = out + _shift_rows(cur, tail, s) * w[kk - 1 - s:kk - s, :]
    return out

def _expand_heads(m, h0, n_cols):
    rows = m.shape[0]
    lane = lax.broadcasted_iota(jnp.int32, (rows, LANE), 1)
    cols = []
    for c in range(n_cols):
        lo = jnp.broadcast_to(m[:, h0 + 2 * c:h0 + 2 * c + 1], (rows, LANE))
        hi = jnp.broadcast_to(m[:, h0 + 2 * c + 1:h0 + 2 * c + 2], (rows, LANE))
        cols.append(jnp.where(lane < SSD_HEAD_DIM, lo, hi))
    return jnp.concatenate(cols, axis=1)


def _ssd_prompt_kernel(scb_ref, scc_ref, scx_ref, z0_ref, z1_ref, x0_ref, x1_ref, bc_ref, dt_ref,
                       scw_ref, cw_ref, cb_ref, dtb_ref, alog_ref, dexp_ref, nw_ref,
                       y_ref, scst_ref, cvst_ref, ssm_ref,
                       utail, xtail, s_scr):
    c = pl.program_id(1)
    q = scb_ref.shape[0]
    dsc = scb_ref.shape[1]
    gw = 2 * x0_ref.shape[1] // SSD_GROUPS
    hpg = gw // SSD_HEAD_DIM

    @pl.when(c == 0)
    def _():
        utail[...] = jnp.zeros(utail.shape, F32)
        xtail[...] = jnp.zeros(xtail.shape, F32)
        s_scr[...] = jnp.zeros(s_scr.shape, F32)

    u = scc_ref[...] * scx_ref[...]
    y_sc = scb_ref[...] * _causal_conv(u, utail[...], scw_ref[...])
    utail[...] = u[q - SUBLANE:, :]
    y_ref[:, :dsc] = y_sc.astype(BF16)

    bc = bc_ref[...]
    n_bc = bc.shape[1]
    o_bc = 2 * x0_ref.shape[1]
    bc_cols = slice(o_bc, o_bc + n_bc)
    bc_c = _silu(_causal_conv(bc, xtail[:, bc_cols], cw_ref[:, bc_cols]) + cb_ref[:, bc_cols])
    xtail[:, bc_cols] = bc[q - SUBLANE:, :]
    half = n_bc // 2

    dt = _softplus(dt_ref[...] + dtb_ref[...])
    a = dt * (-jnp.exp(alog_ref[...]))
    ri = lax.broadcasted_iota(jnp.int32, (q, q), 0)
    ci = lax.broadcasted_iota(jnp.int32, (q, q), 1)
    tril = ri >= ci
    a_cs = jnp.dot(tril.astype(F32), a, precision=HIGHEST, preferred_element_type=F32)
    a_cs_t = a_cs.T
    quad = 4 * SSD_HEAD_DIM
    lane_head = lax.broadcasted_iota(jnp.int32, (q, quad), 1) // SSD_HEAD_DIM

    for g in range(SSD_GROUPS):
        x_ref = x0_ref if g < SSD_GROUPS // 2 else x1_ref
        z_ref = z0_ref if g < SSD_GROUPS // 2 else z1_ref
        lo = (g % (SSD_GROUPS // 2)) * gw
        go = g * gw
        xg = x_ref[:, lo:lo + gw]
        xs = _silu(_causal_conv(xg, xtail[:, go:go + gw], cw_ref[:, go:go + gw]) + cb_ref[:, go:go + gw])
        xtail[:, go:go + gw] = xg[q - SUBLANE:, :]

        n_cols = gw // LANE
        dt_x = _expand_heads(dt, g * hpg, n_cols)
        a_x = _expand_heads(a_cs, g * hpg, n_cols)
        ea = jnp.exp(a_x)
        dte = jnp.exp(a_x[q - 1:q, :] - a_x)
        xdt = xs * dt_x
        xdt_b = xdt.astype(BF16)

        bg = bc_c[:, g * SSD_STATE:(g + 1) * SSD_STATE]
        cg_b = bc_c[:, half + g * SSD_STATE:half + (g + 1) * SSD_STATE].astype(BF16)
        cb_mat = lax.dot_general(cg_b, bg.astype(BF16), NT_DIMS, preferred_element_type=F32)
        s_g = s_scr[:, go:go + gw]
        y_off = jnp.dot(cg_b, s_g.astype(BF16), preferred_element_type=F32) * ea

        y_quads = []
        for qd in range(gw // quad):
            xq = xdt[:, qd * quad:(qd + 1) * quad]
            acc = None
            for i in range(4):
                h = g * hpg + qd * 4 + i
                seg = a_cs[:, h:h + 1] - a_cs_t[h:h + 1, :]
                dec = jnp.exp(jnp.where(tril, seg, -jnp.inf))
                w_h = (cb_mat * dec).astype(BF16)
                xm = jnp.where(lane_head == i, xq, 0.0).astype(BF16)
                d = jnp.dot(w_h, xm, preferred_element_type=F32)
                acc = d if acc is None else acc + d
            y_quads.append(acc)
        y = jnp.concatenate(y_quads, axis=1) + y_off + dexp_ref[:, go:go + gw] * xs

        zg = z_ref[:, lo:lo + gw]
        y = y * _silu(zg)
        ms = jnp.mean(y * y, axis=-1, keepdims=True)
        y = y * lax.rsqrt(ms + EPS) * nw_ref[:, go:go + gw]
        y_ref[:, dsc + go:dsc + go + gw] = y.astype(BF16)

        s_new = s_g * ea[q - 1:q, :] + jnp.dot(bg.T.astype(BF16), (xdt * dte).astype(BF16),
                                               preferred_element_type=F32)
        s_scr[:, go:go + gw] = s_new

    @pl.when(c == pl.num_programs(1) - 1)
    def _():
        scst_ref[0] = utail[...]
        cvst_ref[0] = xtail[...]
        ssm_ref[0] = s_scr[...].T


def _ssd_prompt(proj, dt_raw, scw, cw, cb, dtb, alog, dexp, nw, bp, seq):
    q = min(SSD_CHUNK, seq)
    nc = seq // q
    dsc = scw.shape[1]
    d_inner = dexp.shape[1]
    conv_dim = cw.shape[1]
    bw = dsc

    def col(kb):
        return pl.BlockSpec((q, bw), lambda b, c: (b * nc + c, kb))

    n_z = d_inner // bw
    specs = [col(0), col(1), col(2)]
    specs += [col(3 + i) for i in range(n_z)]
    specs += [col(3 + n_z + i) for i in range(n_z)]
    specs += [col(3 + 2 * n_z)]
    specs += [pl.BlockSpec((q, LANE), lambda b, c: (b * nc + c, 0))]
    for arr in (scw, cw, cb, dtb, alog, dexp, nw):
        specs.append(pl.BlockSpec(arr.shape, lambda b, c: (0, 0)))
    return pl.pallas_call(
        _ssd_prompt_kernel,
        grid=(bp, nc),
        in_specs=specs,
        out_specs=[pl.BlockSpec((q, dsc + d_inner), lambda b, c: (b * nc + c, 0)),
                   pl.BlockSpec((1, SUBLANE, dsc), lambda b, c: (b, 0, 0)),
                   pl.BlockSpec((1, SUBLANE, conv_dim), lambda b, c: (b, 0, 0)),
                   pl.BlockSpec((1, d_inner, SSD_STATE), lambda b, c: (b, 0, 0))],
        out_shape=[jax.ShapeDtypeStruct((bp * seq, dsc + d_inner), BF16),
                   jax.ShapeDtypeStruct((bp, SUBLANE, dsc), F32),
                   jax.ShapeDtypeStruct((bp, SUBLANE, conv_dim), F32),
                   jax.ShapeDtypeStruct((bp, d_inner, SSD_STATE), F32)],
        scratch_shapes=[pltpu.VMEM((SUBLANE, dsc), F32),
                        pltpu.VMEM((SUBLANE, conv_dim), F32),
                        pltpu.VMEM((SSD_STATE, d_inner), F32)],
        compiler_params=_cparams("parallel", "arbitrary"),
        name="ssd_prompt",
    )(*([proj] * (4 + 2 * n_z)), dt_raw, scw, cw, cb, dtb, alog, dexp, nw)


def _even_sample_prep_kernel(proj_ref, dt_ref, scst_ref, cvst_ref, scw_ref, cw_ref, cb_ref, dtb_ref, alog_ref,
                             ysc_ref, scn_ref, cvn_ref, xs_ref, b_ref, c_ref, xdt_t_ref, dec_ref):
    dsc = scw_ref.shape[1]
    conv_dim = cw_ref.shape[1]
    d_inner = xs_ref.shape[1]
    n_heads = d_inner // SSD_HEAD_DIM
    scb = proj_ref[:, :dsc]
    u = proj_ref[:, dsc:2 * dsc] * proj_ref[:, 2 * dsc:3 * dsc]
    xbc = proj_ref[:, 3 * dsc + d_inner:3 * dsc + d_inner + conv_dim]
    b0 = scst_ref[:, :dsc]
    b1 = scst_ref[:, dsc:]
    ysc_ref[...] = scb * (b0 * scw_ref[0:1, :] + b1 * scw_ref[1:2, :] + u * scw_ref[2:3, :])
    scn_ref[:, :dsc] = b1
    scn_ref[:, dsc:] = u

    c0 = cvst_ref[:, :conv_dim]
    c1 = cvst_ref[:, conv_dim:2 * conv_dim]
    c2 = cvst_ref[:, 2 * conv_dim:]
    act = _silu(c0 * cw_ref[0:1, :] + c1 * cw_ref[1:2, :] + c2 * cw_ref[2:3, :] + xbc * cw_ref[3:4, :] + cb_ref[...])
    cvn_ref[:, :conv_dim] = c1
    cvn_ref[:, conv_dim:2 * conv_dim] = c2
    cvn_ref[:, 2 * conv_dim:] = xbc
    xs = act[:, :d_inner]
    n_bc = (conv_dim - d_inner) // 2
    xs_ref[...] = xs
    b_ref[...] = act[:, d_inner:d_inner + n_bc]
    c_ref[...] = act[:, d_inner + n_bc:]

    dt = _softplus(dt_ref[...] + dtb_ref[...])
    dec_ref[...] = jnp.exp(dt * (-jnp.exp(alog_ref[...])))
    xdt_t_ref[...] = (xs * _expand_heads(dt, 0, n_heads // 2)).T


def _even_sample_prep(proj, dt_raw, scst, cvst, scw, cw, cb, dtb, alog, d_inner):
    bs = proj.shape[0]
    dsc = scw.shape[1]
    conv_dim = cw.shape[1]
    n_bc = (conv_dim - d_inner) // 2
    shapes = [(bs, dsc), (bs, 2 * dsc), (bs, 3 * conv_dim), (bs, d_inner), (bs, n_bc), (bs, n_bc),
              (d_inner, bs), (bs, LANE)]
    return pl.pallas_call(
        _even_sample_prep_kernel,
        out_shape=[jax.ShapeDtypeStruct(s, F32) for s in shapes],
        compiler_params=pltpu.CompilerParams(vmem_limit_bytes=VMEM_LIMIT),
        name="even_sample_prep",
    )(proj, dt_raw, scst, cvst, scw, cw, cb, dtb, alog)


def _ssm_sample_kernel(dec_ref, h0_ref, xt_ref, b_ref, c_ref, hn_ref, yt_ref):
    h = pl.program_id(0)
    bs = h0_ref.shape[0]
    p = h0_ref.shape[2]
    n = h0_ref.shape[3]
    lane = lax.broadcasted_iota(jnp.int32, (p, bs), 1)
    xt = xt_ref[...]
    ones = jnp.ones((n, bs), BF16)
    yt = jnp.zeros((p, bs), F32)
    for b in range(bs):
        xcol = jnp.broadcast_to(xt[:, b:b + 1], (p, n))
        hn = h0_ref[b, 0] * dec_ref[b, h] + xcol * b_ref[b:b + 1, :]
        hn_ref[b, 0] = hn
        ysum = jnp.zeros((p, bs), F32)
        rest = hn * c_ref[b:b + 1, :]
        for _ in range(3):
            piece = rest.astype(BF16)
            ysum = ysum + jnp.dot(piece, ones, preferred_element_type=F32)
            rest = rest - piece.astype(F32)
        yt = jnp.where(lane == b, ysum, yt)
    yt_ref[...] = yt


def _ssm_sample(h0, xdt_t, dec, bm, cm):
    bs, n_heads, p, n = h0.shape
    hpg = n_heads // SSD_GROUPS
    return pl.pallas_call(
        _ssm_sample_kernel,
        grid=(n_heads,),
        in_specs=[pl.BlockSpec(memory_space=pltpu.SMEM),
                  pl.BlockSpec((bs, 1, p, n), lambda h: (0, h, 0, 0)),
                  pl.BlockSpec((p, bs), lambda h: (h, 0)),
                  pl.BlockSpec((bs, n), lambda h: (0, h // hpg)),
                  pl.BlockSpec((bs, n), lambda h: (0, h // hpg))],
        out_specs=[pl.BlockSpec((bs, 1, p, n), lambda h: (0, h, 0, 0)),
                   pl.BlockSpec((p, bs), lambda h: (h, 0))],
        out_shape=[jax.ShapeDtypeStruct(h0.shape, F32), jax.ShapeDtypeStruct((n_heads * p, bs), F32)],
        compiler_params=_cparams("parallel"),
        name="ssm_sample",
    )(dec, h0, xdt_t, bm, cm)


def _even_out_sample_kernel(yt_ref, xs_ref, proj_ref, ysc_ref, dexp_ref, nw_ref, w_ref, x_ref, g_ref, o_ref):
    d_inner = xs_ref.shape[1]
    dsc = ysc_ref.shape[1]
    gw = d_inner // SSD_GROUPS
    y = yt_ref[...].T + dexp_ref[...] * xs_ref[...]
    y = y * _silu(proj_ref[:, 3 * dsc:3 * dsc + d_inner])
    parts = []
    for g in range(SSD_GROUPS):
        yg = y[:, g * gw:(g + 1) * gw]
        ms = jnp.mean(yg * yg, axis=-1, keepdims=True)
        parts.append(yg * lax.rsqrt(ms + EPS))
    y_ssd = jnp.concatenate(parts, axis=1) * nw_ref[...]
    cat = jnp.concatenate([ysc_ref[...], y_ssd], axis=1).astype(BF16)
    o_ref[...] = x_ref[...] + g_ref[0] * jnp.dot(cat, w_ref[...], preferred_element_type=F32)


def _even_out_sample(y_t, xs, proj, ysc, dexp, nw, w_out, x, mod):
    bs, d = x.shape
    full = lambda a: pl.BlockSpec(a.shape, lambda i: (0,) * a.ndim)
    return pl.pallas_call(
        _even_out_sample_kernel,
        grid=(1,),
        in_specs=[full(y_t), full(xs), full(proj), full(ysc), full(dexp), full(nw), full(w_out), full(x),
                  pl.BlockSpec((1, bs, d), lambda i: (0, 0, 2))],
        out_specs=pl.BlockSpec((bs, d), lambda i: (0, 0)),
        out_shape=jax.ShapeDtypeStruct((bs, d), F32),
        compiler_params=_cparams("arbitrary"),
        name="even_out_sample",
    )(y_t, xs, proj, ysc, dexp, nw, w_out, x, mod)


def _rope_table_kernel(ca_ref, cb_ref, cc_ref, *, pos0, stride):
    rows = ca_ref.shape[0]
    row = lax.broadcasted_iota(jnp.int32, (rows, LANE), 0) + pl.program_id(0) * rows
    d = lax.broadcasted_iota(jnp.int32, (rows, LANE), 1) % ATT_HEAD_DIM
    half = ROT_DIM // 2
    pos = (pos0 + row * stride).astype(F32)
    inv = jnp.power(jnp.full((rows, LANE), ROPE_THETA, F32), -((d % half).astype(F32) * 2.0 / ROT_DIM))
    ang = pos * inv
    cs = jnp.cos(ang)
    sn = jnp.sin(ang)
    ca_ref[...] = jnp.where(d < ROT_DIM, cs, 1.0)
    cb_ref[...] = jnp.where(d < half, -sn, 0.0)
    cc_ref[...] = jnp.where(d < half, 0.0, jnp.where(d < ROT_DIM, sn, 0.0))


def _rope_tables(rows, pos0, stride):
    tr = min(rows, 512)
    spec = pl.BlockSpec((tr, LANE), lambda i: (i, 0))
    return pl.pallas_call(
        functools.partial(_rope_table_kernel, pos0=pos0, stride=stride),
        grid=(rows // tr,),
        out_specs=[spec, spec, spec],
        out_shape=[jax.ShapeDtypeStruct((rows, LANE), F32)] * 3,
        compiler_params=_cparams("parallel"),
        name="rope_tables",
    )()


def _rope(t, ca, cb, cc):
    cols = []
    for c in range(t.shape[1] // LANE):
        s = t[:, c * LANE:(c + 1) * LANE]
        cols.append(s * ca + pltpu.roll(s, LANE - ROT_DIM // 2, axis=1) * cb + pltpu.roll(s, ROT_DIM // 2, axis=1) * cc)
    return jnp.concatenate(cols, axis=1)


def _sel_keys(k, j):
    lane = lax.broadcasted_iota(jnp.int32, (k.shape[0], LANE), 1)
    onehot = jnp.where(lane == j, 1.0, 0.0)
    in_sel = lane < SEL_LANES
    k_lo = jnp.concatenate([jnp.where(in_sel, onehot, k[:, :LANE]), k[:, LANE:]], axis=1)
    k_hi = jnp.concatenate([k[:, :LANE], jnp.where(in_sel, onehot, k[:, LANE:])], axis=1)
    return k_lo.astype(BF16), k_hi.astype(BF16)


def _qkv_kernel(x_ref, nw_ref, sh_ref, sc_ref, w_ref, ca_ref, cb_ref, cc_ref, q_ref, k_ref, v_ref, *blk_refs,
                blocks_per_seq):
    nq = q_ref.shape[1]
    nk = (w_ref.shape[1] - nq) // 2
    hb = _norm_mod(x_ref[...], nw_ref[...], sh_ref[0], sc_ref[0]).astype(BF16)
    qkv = jnp.dot(hb, w_ref[...], preferred_element_type=F32)
    ca, cb, cc = ca_ref[...], cb_ref[...], cc_ref[...]
    q_ref[...] = _rope(qkv[:, :nq], ca, cb, cc)
    k = _rope(qkv[:, nq:nq + nk], ca, cb, cc)
    v = qkv[:, nq + nk:]
    if blocks_per_seq:
        km_ref, klo_ref, khi_ref, vt_ref = blk_refs
        vt = v.T
        k_ref[0] = k.T
        v_ref[0] = vt
        km_ref[0] = jnp.mean(k, axis=0, keepdims=True)
        klo_ref[...], khi_ref[...] = _sel_keys(k, pl.program_id(0) % blocks_per_seq)
        hd = ATT_HEAD_DIM
        ones_row = jnp.where(lax.broadcasted_iota(jnp.int32, (BF16_SUBLANE, vt.shape[1]), 0) == 0, 1.0, 0.0)
        vt_ref[0] = jnp.concatenate([piece for h in range(nk // hd) for piece in (vt[h * hd:(h + 1) * hd], ones_row)],
                                    axis=0).astype(BF16)
    else:
        k_ref[...] = k
        v_ref[...] = v


def _qkv(x, nw, mod, w, tabs, rows_per_group, tm, per_block):
    t, d = x.shape
    nq = ATT_HEADS * ATT_HEAD_DIM
    nk = ATT_KV_HEADS * ATT_HEAD_DIM
    tab_tiles = tabs[0].shape[0] // tm
    tab = pl.BlockSpec((tm, LANE), lambda i: (i % tab_tiles, 0))
    row = lambda n: pl.BlockSpec((tm, n), lambda i: (i, 0))
    if per_block:
        bps = rows_per_group // tm
        kv_t = pl.BlockSpec((1, nk, tm), lambda i: (i // bps, 0, i % bps))
        v_rows = ATT_KV_HEADS * (ATT_HEAD_DIM + BF16_SUBLANE)
        out_specs = [row(nq), kv_t, kv_t, pl.BlockSpec((1, 1, nk), lambda i: (i, 0, 0)), row(nk), row(nk),
                     pl.BlockSpec((1, v_rows, tm), lambda i: (i, 0, 0))]
        out_shape = [jax.ShapeDtypeStruct((t, nq), F32),
                     jax.ShapeDtypeStruct((t // rows_per_group, nk, rows_per_group), F32),
                     jax.ShapeDtypeStruct((t // rows_per_group, nk, rows_per_group), F32),
                     jax.ShapeDtypeStruct((t // tm, 1, nk), F32), jax.ShapeDtypeStruct((t, nk), BF16),
                     jax.ShapeDtypeStruct((t, nk), BF16), jax.ShapeDtypeStruct((t // tm, v_rows, tm), BF16)]
    else:
        out_specs = [row(nq), row(nk), row(nk)]
        out_shape = [jax.ShapeDtypeStruct((t, nq), F32), jax.ShapeDtypeStruct((t, nk), F32),
                     jax.ShapeDtypeStruct((t, nk), F32)]
    return pl.pallas_call(
        functools.partial(_qkv_kernel, blocks_per_seq=rows_per_group // tm if per_block else 0),
        grid=(t // tm,),
        in_specs=[row(d), pl.BlockSpec((1, d), lambda i: (0, 0)),
                  _mod_spec(mod, 0, rows_per_group, tm), _mod_spec(mod, 1, rows_per_group, tm),
                  _resident(w.shape), tab, tab, tab],
        out_specs=out_specs,
        out_shape=out_shape,
        compiler_params=_cparams("parallel"),
        name="qkv_rope",
    )(x, nw, mod, mod, w, *tabs)


def _select_bias(scores, n_past, n_rounds):
    lane = lax.broadcasted_iota(jnp.int32, scores.shape, 1)
    lanef = lane.astype(F32)
    s = jnp.where(lane < n_past, scores, -jnp.inf)
    bias = jnp.full(scores.shape, NEG_BIG, F32)
    for r in range(n_rounds):
        mx = jnp.max(s, axis=1, keepdims=True)
        idx = jnp.min(jnp.where(s == mx, lanef, float(LANE)), axis=1, keepdims=True)
        bias = jnp.where(lanef == jnp.where(n_past > r, idx, -1.0), 0.0, bias)
        s = jnp.where(lanef == idx, -jnp.inf, s)
    return bias


def _select_bias_t(scores, n_past, n_rounds):
    row = lax.broadcasted_iota(jnp.int32, scores.shape, 0)
    rowf = row.astype(F32)
    s = jnp.where(row < n_past, scores, -jnp.inf)
    bias = jnp.full(scores.shape, NEG_BIG, F32)
    for r in range(n_rounds):
        mx = jnp.max(s, axis=0, keepdims=True)
        idx = jnp.min(jnp.where(s == mx, rowf, float(SEL_LANES)), axis=0, keepdims=True)
        bias = jnp.where(rowf == jnp.where(n_past > r, idx, -1.0), 0.0, bias)
        s = jnp.where(rowf == idx, -jnp.inf, s)
    return jnp.where(row == n_past, 0.0, bias)


def _moba_prompt_kernel(q_ref, klo_ref, khi_ref, vt_ref, km_ref, o_ref, qa_scr, m_scr, acc_scr):
    qi = pl.program_id(1)
    blk = q_ref.shape[0]
    kvw = klo_ref.shape[1]
    hd = ATT_HEAD_DIM
    n_q = ATT_GROUP * blk
    kmean = km_ref[0]
    km_hi = kmean.astype(BF16)
    km_lo = (kmean - km_hi.astype(F32)).astype(BF16)
    n_rounds = min(MOBA_TOPK, klo_ref.shape[0] // blk)

    q_t = q_ref[...].T
    for h in range(ATT_KV_HEADS):
        qh_b = jnp.concatenate([q_t[g * kvw + h * hd:g * kvw + (h + 1) * hd, :] for g in range(ATT_GROUP)],
                               axis=1).astype(BF16)
        scores_t = (jnp.dot(km_hi[:, h * hd:(h + 1) * hd], qh_b, preferred_element_type=F32)
                    + jnp.dot(km_lo[:, h * hd:(h + 1) * hd], qh_b, preferred_element_type=F32))
        bias_t = _select_bias_t(scores_t, qi, n_rounds)
        sel_row = LANE if h == 0 else 0
        qa_scr[h] = jnp.zeros((kvw, n_q), BF16)
        qa_scr[h, h * hd:(h + 1) * hd, :] = qh_b
        qa_scr[h, sel_row:sel_row + SEL_LANES, :] = bias_t.astype(BF16)

    v_rows = vt_ref.shape[1] // ATT_KV_HEADS
    key_row = lax.broadcasted_iota(jnp.int32, (blk, n_q), 0)
    q_pos = lax.broadcasted_iota(jnp.int32, (blk, n_q), 1) % blk

    def scores(h, j):
        off = pl.multiple_of(j * blk, blk)
        ka = (khi_ref if h == 0 else klo_ref)[pl.ds(off, blk), :]
        return jnp.dot(ka, qa_scr[h], preferred_element_type=F32)

    def softmax(h, s, first):
        if first:
            s = jnp.where(key_row <= q_pos, s, NEG_BIG)
        mx = jnp.max(s, axis=0, keepdims=True)
        if first:
            m_new, alpha = mx, None
        else:
            m_old = m_scr[h, 0:1, :]
            m_new = jnp.maximum(m_old, mx)
            alpha = jnp.exp2(m_old - m_new)
        m_scr[h] = jnp.broadcast_to(m_new, (SUBLANE, n_q))
        return jnp.exp2((s - m_new).astype(BF16)), alpha

    def values(h, j, p, alpha):
        vt_h = vt_ref[j, h * v_rows:(h + 1) * v_rows, :]
        pv = jnp.dot(vt_h, p, preferred_element_type=F32)
        acc_scr[h] = pv if alpha is None else acc_scr[h] * alpha + pv

    def block_step(j, first):
        s_next = scores(0, j)
        for h in range(ATT_KV_HEADS):
            s = s_next
            if h + 1 < ATT_KV_HEADS:
                s_next = scores(h + 1, j)
            p, alpha = softmax(h, s, first)
            values(h, j, p, alpha)

    block_step(qi, True)

    def body(j, carry):
        block_step(j, False)
        return carry

    lax.fori_loop(0, qi, body, 0)

    hd = ATT_HEAD_DIM
    for g in range(ATT_GROUP):
        cols = slice(g * blk, (g + 1) * blk)
        o_t = jnp.concatenate([acc_scr[h, :hd, cols] / acc_scr[h, hd:hd + 1, cols]
                               for h in range(ATT_KV_HEADS)], axis=0)
        o_ref[:, g * kvw:(g + 1) * kvw] = o_t.T.astype(BF16)


def _moba_prompt(q, k_lo, k_hi, v_t, kmean, bp, seq):
    blk = MOBA_BLOCK
    nq = seq // blk
    kvw = k_lo.shape[1]
    n_q = ATT_GROUP * blk
    v_rows = v_t.shape[1]
    return pl.pallas_call(
        _moba_prompt_kernel,
        grid=(bp, nq),
        in_specs=[pl.BlockSpec((blk, q.shape[1]), lambda b, i: (b * nq + i, 0)),
                  pl.BlockSpec((seq, kvw), lambda b, i: (b, 0)),
                  pl.BlockSpec((seq, kvw), lambda b, i: (b, 0)),
                  pl.BlockSpec((nq, v_rows, blk), lambda b, i: (b, 0, 0)),
                  pl.BlockSpec((1, SEL_LANES, kvw), lambda b, i: (b, 0, 0))],
        out_specs=pl.BlockSpec((blk, q.shape[1]), lambda b, i: (b * nq + i, 0)),
        out_shape=jax.ShapeDtypeStruct(q.shape, BF16),
        scratch_shapes=[pltpu.VMEM((ATT_KV_HEADS, kvw, n_q), BF16),
                        pltpu.VMEM((ATT_KV_HEADS, SUBLANE, n_q), F32),
                        pltpu.VMEM((ATT_KV_HEADS, v_rows // ATT_KV_HEADS, n_q), F32)],
        compiler_params=_cparams("parallel", "arbitrary"),
        name="moba_prompt",
    )(q, k_lo, k_hi, v_t, kmean)


def _moba_sample_kernel(pt_ref, q_ref, kn_ref, vn_ref, avg_ref, *rest, n_pages):
    k_pages = rest[:n_pages]
    v_pages = rest[n_pages:2 * n_pages]
    o_ref = rest[2 * n_pages]
    hd = ATT_HEAD_DIM
    kvw = kn_ref.shape[2]
    page = k_pages[0].shape[3]
    n_past = n_pages * page // MOBA_BLOCK
    n_rows = ATT_KV_HEADS * SUBLANE
    q = q_ref[0]
    kn = kn_ref[0]
    vn = vn_ref[0]
    row = lax.broadcasted_iota(jnp.int32, (n_rows, hd), 0)
    q_rows = jnp.zeros((n_rows, hd), F32)
    kn_rows = jnp.zeros((n_rows, hd), F32)
    vn_rows = jnp.zeros((n_rows, hd), F32)
    for h in range(ATT_KV_HEADS):
        in_h = row // SUBLANE == h
        kn_rows = jnp.where(in_h, jnp.broadcast_to(kn[:, h * hd:(h + 1) * hd], (n_rows, hd)), kn_rows)
        vn_rows = jnp.where(in_h, jnp.broadcast_to(vn[:, h * hd:(h + 1) * hd], (n_rows, hd)), vn_rows)
        for g in range(ATT_GROUP):
            lo = g * kvw + h * hd
            q_rows = jnp.where(row == h * SUBLANE + g, jnp.broadcast_to(q[:, lo:lo + hd], (n_rows, hd)), q_rows)
    q_b = q_rows.astype(BF16)

    score_parts, logit_parts = [], []
    for h in range(ATT_KV_HEADS):
        rows = slice(h * SUBLANE, (h + 1) * SUBLANE)
        kt = jnp.concatenate([r[0, h] for r in k_pages], axis=1)
        kt_b = kt.astype(BF16)
        kt_lo = (kt - kt_b.astype(F32)).astype(BF16)
        kmt = (jnp.dot(kt_b, avg_ref[...], preferred_element_type=F32)
               + jnp.dot(kt_lo, avg_ref[...], preferred_element_type=F32))
        score_parts.append(jnp.dot(q_rows[rows], kmt, precision=HIGHEST, preferred_element_type=F32))
        logit_parts.append(jnp.dot(q_b[rows], kt_b, preferred_element_type=F32))
    bias = _select_bias(jnp.concatenate(score_parts, axis=0), n_past, min(MOBA_TOPK, n_past + 1))
    logits = jnp.concatenate(logit_parts, axis=0)
    vts = [jnp.concatenate([r[0, h] for r in v_pages], axis=1).astype(BF16) for h in range(ATT_KV_HEADS)]

    lane = lax.broadcasted_iota(jnp.int32, (n_rows, LANE), 1)
    blk_max = jnp.full((n_rows, LANE), NEG_BIG, F32)
    blk_sum = jnp.zeros((n_rows, LANE), F32)
    blk_out = []
    for j in range(n_past):
        keys = slice(j * MOBA_BLOCK, (j + 1) * MOBA_BLOCK)
        lj = logits[:, keys]
        mj = jnp.max(lj, axis=1, keepdims=True)
        pj = jnp.exp2(lj - mj)
        blk_max = jnp.where(lane == j, mj, blk_max)
        blk_sum = jnp.where(lane == j, jnp.sum(pj, axis=1, keepdims=True), blk_sum)
        pj_b = pj.astype(BF16)
        blk_out.append(jnp.concatenate(
            [lax.dot_general(pj_b[h * SUBLANE:(h + 1) * SUBLANE], vts[h][:, keys], NT_DIMS, preferred_element_type=F32)
             for h in range(ATT_KV_HEADS)], axis=0))
    l_new = jnp.sum(q_rows * kn_rows, axis=1, keepdims=True)
    chosen = blk_max + bias
    m = jnp.maximum(jnp.max(chosen, axis=1, keepdims=True), l_new)
    w = jnp.exp2(chosen - m)
    pn = jnp.exp2(l_new - m)
    denom = jnp.sum(w * blk_sum, axis=1, keepdims=True) + pn
    o = pn * vn_rows
    for j in range(n_past):
        o = o + w[:, j:j + 1] * blk_out[j]
    o = o / denom
    for h in range(ATT_KV_HEADS):
        for g in range(ATT_GROUP):
            lo = g * kvw + h * hd
            o_ref[0, :, lo:lo + hd] = o[h * SUBLANE + g:h * SUBLANE + g + 1, :]


def _moba_sample(q, kn, vn, cache_kt, cache_vt, page_table):
    bs = q.shape[0]
    n_pages = page_table.shape[1]
    _, n_kv, hd, page = cache_kt.shape
    kvw = n_kv * hd
    past = n_pages * page
    one = lambda n: pl.BlockSpec((1, 1, n), lambda b, pt: (b, 0, 0))
    avg = jnp.where(jnp.arange(past)[:, None] // MOBA_BLOCK == jnp.arange(LANE)[None, :], 1.0 / MOBA_BLOCK, 0.0).astype(BF16)

    def page_spec(p):
        return pl.BlockSpec((1, n_kv, hd, page), lambda b, pt: (pt[b, p], 0, 0, 0))

    grid_spec = pltpu.PrefetchScalarGridSpec(
        num_scalar_prefetch=1,
        grid=(bs,),
        in_specs=[one(q.shape[1]), one(kvw), one(kvw), pl.BlockSpec((past, LANE), lambda b, pt: (0, 0))]
        + [page_spec(p) for p in range(n_pages)] * 2,
        out_specs=one(q.shape[1]),
    )
    return pl.pallas_call(
        functools.partial(_moba_sample_kernel, n_pages=n_pages),
        grid_spec=grid_spec,
        out_shape=jax.ShapeDtypeStruct((bs, 1, q.shape[1]), F32),
        compiler_params=_cparams("parallel"),
        name="moba_sample",
    )(page_table, q.reshape(bs, 1, -1), kn.reshape(bs, 1, kvw), vn.reshape(bs, 1, kvw), avg,
      *([cache_kt] * n_pages), *([cache_vt] * n_pages))


def _pad_lanes(v, fill=0.0):
    return jnp.pad(v.reshape(1, -1), ((0, 0), (0, LANE - v.shape[0])), constant_values=fill)


def kernel(x_prompt, x_sample, state_sc_conv, state_ssd_conv, state_ssm, cache_k, cache_v, page_table,
           c_prompt, c_sample, norm_mix_w, norm_mlp_w, norm_final_w, w_ada, b_ada, w_in_e, sc_conv_w,
           ssd_conv_w, ssd_conv_b, ssd_dt_bias, ssd_a_log, ssd_d, ssd_norm_w, w_out_e, w_qkv, w_o, w_up, w_down):
    bp, seq, d = x_prompt.shape
    bs, dec_seq, _ = x_sample.shape
    assert dec_seq == 1 and seq % MOBA_BLOCK == 0 and seq % SSD_CHUNK == 0 and seq // MOBA_BLOCK <= SEL_LANES
    n_pages, page = page_table.shape[1], cache_k.shape[2]
    past_len = n_pages * page
    assert past_len % MOBA_BLOCK == 0 and MOBA_BLOCK % page == 0
    dsc = sc_conv_w.shape[2]
    conv_dim = ssd_conv_w.shape[2]
    n_heads = ssd_d.shape[1]
    d_inner = n_heads * SSD_HEAD_DIM
    main = 3 * dsc + d_inner + conv_dim
    kvw = ATT_KV_HEADS * ATT_HEAD_DIM
    nq = ATT_HEADS * ATT_HEAD_DIM

    xp = x_prompt.reshape(bp * seq, d)
    xs = x_sample.reshape(bs, d)

    rows = -(-(bs + bp) // SUBLANE) * SUBLANE
    c_all = jnp.pad(jnp.concatenate([c_sample, c_prompt], axis=0), ((0, rows - bs - bp), (0, 0)))
    ada = _ada(c_all, w_ada, b_ada)

    w_in = w_in_e[0][:, :main].astype(BF16)
    w_dt = jnp.pad(w_in_e[0][:, main:], ((0, 0), (0, LANE - n_heads))).astype(BF16)
    w_out = w_out_e[0].astype(BF16)
    wq = w_qkv[0][:, :nq].reshape(d, ATT_KV_HEADS, ATT_GROUP, ATT_HEAD_DIM).transpose(0, 2, 1, 3).reshape(d, nq)
    w_qkv_p = jnp.concatenate([wq * (ATT_HEAD_DIM ** -0.5 * math.log2(math.e)), w_qkv[0][:, nq:]], axis=1).astype(BF16)
    w_o_p = w_o[0].reshape(ATT_KV_HEADS, ATT_GROUP, ATT_HEAD_DIM, d).transpose(1, 0, 2, 3).reshape(nq, d).astype(BF16)
    w_up_b = w_up.astype(BF16)
    w_down_b = w_down.astype(BF16)

    scw = sc_conv_w[0]
    cw = ssd_conv_w[0]
    cb = ssd_conv_b[0].reshape(1, conv_dim)
    dtb = _pad_lanes(ssd_dt_bias[0])
    alog = _pad_lanes(ssd_a_log[0])
    dexp = jnp.repeat(ssd_d[0], SSD_HEAD_DIM).reshape(1, d_inner)
    ssd_nw = ssd_norm_w[0].reshape(1, d_inner)

    mod_p = ada[0, bs:bs + bp].reshape(bp, 1, 6 * d)
    mod_s = ada[0, :bs].reshape(1, bs, 6 * d)
    nw_mix = norm_mix_w[0].reshape(1, d)
    nw_mlp = norm_mlp_w[0].reshape(1, d)
    fw = norm_final_w.reshape(1, d)

    proj_p, dt_p = _nm_matmul(xp, nw_mix, mod_p, 0, 1, w_in, w_dt, seq, min(seq, 1024), 2048)
    ycat_p, scst_p, cvst_p, ssm_p = _ssd_prompt(proj_p, dt_p, scw, cw, cb, dtb, alog, dexp, ssd_nw, bp, seq)
    xp = _mm_res(ycat_p, w_out, xp, mod_p, 2, seq, 512)
    xp = _mlp(xp, nw_mlp, mod_p, w_up_b[0], w_down_b[0], fw, seq, 512, False)

    proj_s, dt_s = _nm_matmul(xs, nw_mix, mod_s, 0, 1, w_in, w_dt, bs, bs, 1024)
    ysc_s, scn_s, cvn_s, xs_s, b_s, c_s, xdt_t, dec_s = _even_sample_prep(
        proj_s, dt_s, state_sc_conv[0].reshape(bs, -1), state_ssd_conv[0].reshape(bs, -1),
        scw, cw, cb, dtb, alog, d_inner)
    ssm_s, y_t = _ssm_sample(state_ssm[0], xdt_t, dec_s, b_s, c_s)
    xs = _even_out_sample(y_t, xs_s, proj_s, ysc_s, dexp, ssd_nw, w_out, xs, mod_s)
    xs = _mlp(xs, nw_mlp, mod_s, w_up_b[0], w_down_b[0], fw, bs, bs, False)

    mod_p = ada[1, bs:bs + bp].reshape(bp, 1, 6 * d)
    mod_s = ada[1, :bs].reshape(1, bs, 6 * d)
    nw_mix = norm_mix_w[1].reshape(1, d)
    nw_mlp = norm_mlp_w[1].reshape(1, d)

    tabs_p = _rope_tables(seq, 0, 1)
    q_p, kt_p, vt_p, km_p, klo_p, khi_p, vtb_p = _qkv(xp, nw_mix, mod_p, w_qkv_p, tabs_p, seq, MOBA_BLOCK, True)
    n_blk = seq // MOBA_BLOCK
    kmean_p = jnp.pad(km_p.reshape(bp, n_blk, kvw), ((0, 0), (0, SEL_LANES - n_blk), (0, 0)))
    o_p = _moba_prompt(q_p, klo_p, khi_p, vtb_p, kmean_p, bp, seq)
    xp = _mm_res(o_p, w_o_p, xp, mod_p, 2, seq, 512)
    y_prompt = _mlp(xp, nw_mlp, mod_p, w_up_b[1], w_down_b[1], fw, seq, 512, True)

    tabs_s = _rope_tables(bs, past_len, 0)
    q_s, k_s, v_s = _qkv(xs, nw_mix, mod_s, w_qkv_p, tabs_s, bs, bs, False)
    o_s = _moba_sample(q_s, k_s, v_s, cache_k[0].transpose(0, 2, 3, 1), cache_v[0].transpose(0, 2, 3, 1), page_table)
    xs = _mm_res(o_s.reshape(bs, nq).astype(BF16), w_o_p, xs, mod_s, 2, bs, bs)
    y_sample = _mlp(xs, nw_mlp, mod_s, w_up_b[1], w_down_b[1], fw, bs, bs, True)

    def kv_out(t):
        return t.reshape(bp, ATT_KV_HEADS, ATT_HEAD_DIM, seq).transpose(0, 3, 1, 2)[None]

    return (y_prompt.reshape(bp, seq, d), y_sample.reshape(bs, 1, d),
            scst_p[:, SUBLANE - 2:, :][None], scn_s.reshape(1, bs, 2, dsc),
            cvst_p[:, SUBLANE - 3:, :][None], cvn_s.reshape(1, bs, 3, conv_dim),
            ssm_p.reshape(1, bp, n_heads, SSD_HEAD_DIM, SSD_STATE), ssm_s[None],
            kv_out(kt_p), k_s.reshape(1, bs, 1, ATT_KV_HEADS, ATT_HEAD_DIM),
            kv_out(vt_p), v_s.reshape(1, bs, 1, ATT_KV_HEADS, ATT_HEAD_DIM))
```

```python
import functools
import math

import jax
import jax.numpy as jnp
from jax import lax
from jax.experimental import pallas as pl
from jax.experimental.pallas import tpu as pltpu

F32 = jnp.float32
BF16 = jnp.bfloat16
HIGHEST = lax.Precision.HIGHEST

EPS = 1e-6
LANE = 128
SUBLANE = 8
BF16_SUBLANE = 16
SSD_HEAD_DIM = 64
SSD_GROUPS = 4
SSD_STATE = 128
SSD_CHUNK = 256
ATT_HEADS = 16
ATT_KV_HEADS = 4
ATT_GROUP = ATT_HEADS // ATT_KV_HEADS
ATT_HEAD_DIM = 64
ROT_DIM = ATT_HEAD_DIM // 4
ROPE_THETA = 500000.0
MOBA_BLOCK = 256
MOBA_TOPK = 3
SEL_LANES = 16
NEG_BIG = -1e30
EXP2_CLAMP = 60.0
VMEM_LIMIT = 56 * 1024 * 1024

NT_DIMS = (((1,), (1,)), ((), ()))


def _cparams(*sem):
    return pltpu.CompilerParams(dimension_semantics=sem, vmem_limit_bytes=VMEM_LIMIT)


def _silu(x):
    return x * jax.nn.sigmoid(x)


def _softplus(x):
    return jnp.maximum(x, 0.0) + jnp.log1p(jnp.exp(-jnp.abs(x)))


def _norm_mod(x, nw, shift, scale):
    ms = jnp.mean(x * x, axis=-1, keepdims=True)
    xn = x * lax.rsqrt(ms + EPS) * nw
    return xn * (1.0 + scale) + shift


def _resident(shape):
    nd = len(shape)
    return pl.BlockSpec(shape, lambda *_: (0,) * nd, pipeline_mode=pl.Buffered(1))


def _ada_kernel(c_ref, w_ref, b_ref, o_ref):
    s = _silu(c_ref[...]).astype(BF16)
    o_ref[0] = jnp.dot(s, w_ref[0].astype(BF16), preferred_element_type=F32) + b_ref[0]


def _ada(c_all, w_ada, b_ada):
    rows, d = c_all.shape
    n_layers, _, n6 = w_ada.shape
    tn = 1024
    return pl.pallas_call(
        _ada_kernel,
        grid=(n_layers, n6 // tn),
        in_specs=[pl.BlockSpec((rows, d), lambda l, j: (0, 0)),
                  pl.BlockSpec((1, d, tn), lambda l, j: (l, 0, j)),
                  pl.BlockSpec((1, 1, tn), lambda l, j: (l, 0, j))],
        out_specs=pl.BlockSpec((1, rows, tn), lambda l, j: (l, 0, j)),
        out_shape=jax.ShapeDtypeStruct((n_layers, rows, n6), F32),
        compiler_params=_cparams("parallel", "parallel"),
        name="adaln",
    )(c_all, w_ada, b_ada.reshape(n_layers, 1, n6))


def _mod_spec(mod, k, rows_per_group, tm):
    _, r, n6 = mod.shape
    d = n6 // 6
    tiles_per_group = max(rows_per_group // tm, 1)
    return pl.BlockSpec((1, r, d), lambda i, *_: (i // tiles_per_group, 0, k))


def _nm_matmul_kernel(x_ref, nw_ref, sh_ref, sc_ref, w_ref, w2_ref, o_ref, o2_ref, h_scr):
    @pl.when(pl.program_id(1) == 0)
    def _():
        hb = _norm_mod(x_ref[...], nw_ref[...], sh_ref[0], sc_ref[0]).astype(BF16)
        h_scr[...] = hb
        o2_ref[...] = jnp.dot(hb, w2_ref[...], preferred_element_type=F32)

    o_ref[...] = jnp.dot(h_scr[...], w_ref[...], preferred_element_type=F32)


def _nm_matmul(x, nw, mod, k_shift, k_scale, w, w2, rows_per_group, tm, tn):
    t, d = x.shape
    n = w.shape[1]
    n2 = w2.shape[1]
    return pl.pallas_call(
        _nm_matmul_kernel,
        grid=(t // tm, n // tn),
        in_specs=[pl.BlockSpec((tm, d), lambda i, j: (i, 0)),
                  pl.BlockSpec((1, d), lambda i, j: (0, 0)),
                  _mod_spec(mod, k_shift, rows_per_group, tm),
                  _mod_spec(mod, k_scale, rows_per_group, tm),
                  pl.BlockSpec((d, tn), lambda i, j: (0, j)),
                  pl.BlockSpec((d, n2), lambda i, j: (0, 0))],
        out_specs=[pl.BlockSpec((tm, tn), lambda i, j: (i, j)),
                   pl.BlockSpec((tm, n2), lambda i, j: (i, 0))],
        out_shape=[jax.ShapeDtypeStruct((t, n), F32), jax.ShapeDtypeStruct((t, n2), F32)],
        scratch_shapes=[pltpu.VMEM((tm, d), BF16)],
        compiler_params=_cparams("parallel", "arbitrary"),
        name="in_proj",
    )(x, nw, mod, mod, w, w2)


def _mm_res_kernel(a_ref, w_ref, res_ref, g_ref, o_ref):
    o_ref[...] = res_ref[...] + g_ref[0] * jnp.dot(a_ref[...], w_ref[...], preferred_element_type=F32)


def _mm_res(a, w, res, mod, k_gate, rows_per_group, tm):
    t, kdim = a.shape
    d = w.shape[1]
    return pl.pallas_call(
        _mm_res_kernel,
        grid=(t // tm,),
        in_specs=[pl.BlockSpec((tm, kdim), lambda i: (i, 0)),
                  _resident((kdim, d)),
                  pl.BlockSpec((tm, d), lambda i: (i, 0)),
                  _mod_spec(mod, k_gate, rows_per_group, tm)],
        out_specs=pl.BlockSpec((tm, d), lambda i: (i, 0)),
        out_shape=jax.ShapeDtypeStruct((t, d), F32),
        compiler_params=_cparams("parallel"),
        name="proj_residual",
    )(a, w, res, mod)


def _mlp_kernel(x_ref, nw_ref, sh_ref, sc_ref, g_ref, wu_ref, wd_ref, fw_ref, o_ref, *, final_norm, tf):
    x = x_ref[...]
    hb = _norm_mod(x, nw_ref[...], sh_ref[0], sc_ref[0]).astype(BF16)
    acc = jnp.zeros(x.shape, F32)
    for f in range(wu_ref.shape[1] // tf):
        a = jnp.maximum(jnp.dot(hb, wu_ref[:, f * tf:(f + 1) * tf], preferred_element_type=F32), 0.0)
        acc = acc + jnp.dot((a * a).astype(BF16), wd_ref[f * tf:(f + 1) * tf, :], preferred_element_type=F32)
    y = x + g_ref[0] * acc
    if final_norm:
        ms = jnp.mean(y * y, axis=-1, keepdims=True)
        y = y * lax.rsqrt(ms + EPS) * fw_ref[...]
    o_ref[...] = y


def _mlp(x, nw, mod, w_up, w_down, fw, rows_per_group, tm, final_norm):
    t, d = x.shape
    dff = w_up.shape[1]
    return pl.pallas_call(
        functools.partial(_mlp_kernel, final_norm=final_norm, tf=1024),
        grid=(t // tm,),
        in_specs=[pl.BlockSpec((tm, d), lambda i: (i, 0)),
                  pl.BlockSpec((1, d), lambda i: (0, 0)),
                  _mod_spec(mod, 3, rows_per_group, tm),
                  _mod_spec(mod, 4, rows_per_group, tm),
                  _mod_spec(mod, 5, rows_per_group, tm),
                  _resident((d, dff)),
                  _resident((dff, d)),
                  pl.BlockSpec((1, d), lambda i: (0, 0))],
        out_specs=pl.BlockSpec((tm, d), lambda i: (i, 0)),
        out_shape=jax.ShapeDtypeStruct((t, d), F32),
        compiler_params=_cparams("parallel"),
        name="mlp",
    )(x, nw, mod, mod, mod, w_up, w_down, fw)


def _shift_rows(cur, tail, k):
    rolled = pltpu.roll(cur, k, axis=0)
    row = lax.broadcasted_iota(jnp.int32, tail.shape, 0)
    top = jnp.where(row < k, pltpu.roll(tail, k, axis=0), rolled[:SUBLANE])
    return jnp.concatenate([top, rolled[SUBLANE:]], axis=0)


def _causal_conv(cur, tail, w):
    kk = w.shape[0]
    out = cur * w[kk - 1:kk, :]
    for s in range(1, kk):
        out = out + _shift_rows(cur, tail, s) * w[kk - 1 - s:kk - s, :]
    return out

def _expand_heads(m, h0, n_cols):
    rows = m.shape[0]
    lane = lax.broadcasted_iota(jnp.int32, (rows, LANE), 1)
    cols = []
    for c in range(n_cols):
        lo = jnp.broadcast_to(m[:, h0 + 2 * c:h0 + 2 * c + 1], (rows, LANE))
        hi = jnp.broadcast_to(m[:, h0 + 2 * c + 1:h0 + 2 * c + 2], (rows, LANE))
        cols.append(jnp.where(lane < SSD_HEAD_DIM, lo, hi))
    return jnp.concatenate(cols, axis=1)


def _ssd_prompt_kernel(scb_ref, scc_ref, scx_ref, z0_ref, z1_ref, x0_ref, x1_ref, bc_ref, dt_ref,
                       scw_ref, cw_ref, cb_ref, dtb_ref, alog_ref, dexp_ref, nw_ref,
                       y_ref, scst_ref, cvst_ref, ssm_ref,
                       utail, xtail, s_scr):
    c = pl.program_id(1)
    q = scb_ref.shape[0]
    dsc = scb_ref.shape[1]
    gw = 2 * x0_ref.shape[1] // SSD_GROUPS
    hpg = gw // SSD_HEAD_DIM

    @pl.when(c == 0)
    def _():
        utail[...] = jnp.zeros(utail.shape, F32)
        xtail[...] = jnp.zeros(xtail.shape, F32)
        s_scr[...] = jnp.zeros(s_scr.shape, F32)

    u = scc_ref[...] * scx_ref[...]
    y_sc = scb_ref[...] * _causal_conv(u, utail[...], scw_ref[...])
    utail[...] = u[q - SUBLANE:, :]
    y_ref[:, :dsc] = y_sc.astype(BF16)

    bc = bc_ref[...]
    n_bc = bc.shape[1]
    o_bc = 2 * x0_ref.shape[1]
    bc_cols = slice(o_bc, o_bc + n_bc)
    bc_c = _silu(_causal_conv(bc, xtail[:, bc_cols], cw_ref[:, bc_cols]) + cb_ref[:, bc_cols])
    xtail[:, bc_cols] = bc[q - SUBLANE:, :]
    half = n_bc // 2

    dt = _softplus(dt_ref[...] + dtb_ref[...])
    a = dt * (-jnp.exp(alog_ref[...]))
    ri = lax.broadcasted_iota(jnp.int32, (q, q), 0)
    ci = lax.broadcasted_iota(jnp.int32, (q, q), 1)
    tril = ri >= ci
    a_cs = jnp.dot(tril.astype(F32), a, precision=HIGHEST, preferred_element_type=F32)
    a_cs_t = a_cs.T
    quad = 4 * SSD_HEAD_DIM
    lane_head = lax.broadcasted_iota(jnp.int32, (q, quad), 1) // SSD_HEAD_DIM

    for g in range(SSD_GROUPS):
        x_ref = x0_ref if g < SSD_GROUPS // 2 else x1_ref
        z_ref = z0_ref if g < SSD_GROUPS // 2 else z1_ref
        lo = (g % (SSD_GROUPS // 2)) * gw
        go = g * gw
        xg = x_ref[:, lo:lo + gw]
        xs = _silu(_causal_conv(xg, xtail[:, go:go + gw], cw_ref[:, go:go + gw]) + cb_ref[:, go:go + gw])
        xtail[:, go:go + gw] = xg[q - SUBLANE:, :]

        n_cols = gw // LANE
        dt_x = _expand_heads(dt, g * hpg, n_cols)
        a_x = _expand_heads(a_cs, g * hpg, n_cols)
        ea = jnp.exp(a_x)
        dte = jnp.exp(a_x[q - 1:q, :] - a_x)
        xdt = xs * dt_x
        xdt_b = xdt.astype(BF16)

        bg = bc_c[:, g * SSD_STATE:(g + 1) * SSD_STATE]
        cg_b = bc_c[:, half + g * SSD_STATE:half + (g + 1) * SSD_STATE].astype(BF16)
        cb_mat = lax.dot_general(cg_b, bg.astype(BF16), NT_DIMS, preferred_element_type=F32)
        s_g = s_scr[:, go:go + gw]
        y_off = jnp.dot(cg_b, s_g.astype(BF16), preferred_element_type=F32) * ea

        y_quads = []
        for qd in range(gw // quad):
            xq = xdt[:, qd * quad:(qd + 1) * quad]
            acc = None
            for i in range(4):
                h = g * hpg + qd * 4 + i
                seg = a_cs[:, h:h + 1] - a_cs_t[h:h + 1, :]
                dec = jnp.exp(jnp.where(tril, seg, -jnp.inf))
                w_h = (cb_mat * dec).astype(BF16)
                xm = jnp.where(lane_head == i, xq, 0.0).astype(BF16)
                d = jnp.dot(w_h, xm, preferred_element_type=F32)
                acc = d if acc is None else acc + d
            y_quads.append(acc)
        y = jnp.concatenate(y_quads, axis=1) + y_off + dexp_ref[:, go:go + gw] * xs

        zg = z_ref[:, lo:lo + gw]
        y = y * _silu(zg)
        ms = jnp.mean(y * y, axis=-1, keepdims=True)
        y = y * lax.rsqrt(ms + EPS) * nw_ref[:, go:go + gw]
        y_ref[:, dsc + go:dsc + go + gw] = y.astype(BF16)

        s_new = s_g * ea[q - 1:q, :] + jnp.dot(bg.T.astype(BF16), (xdt * dte).astype(BF16),
                                               preferred_element_type=F32)
        s_scr[:, go:go + gw] = s_new

    @pl.when(c == pl.num_programs(1) - 1)
    def _():
        scst_ref[0] = utail[...]
        cvst_ref[0] = xtail[...]
        ssm_ref[0] = s_scr[...].T


def _ssd_prompt(proj, dt_raw, scw, cw, cb, dtb, alog, dexp, nw, bp, seq):
    q = min(SSD_CHUNK, seq)
    nc = seq // q
    dsc = scw.shape[1]
    d_inner = dexp.shape[1]
    conv_dim = cw.shape[1]
    bw = dsc

    def col(kb):
        return pl.BlockSpec((q, bw), lambda b, c: (b * nc + c, kb))

    n_z = d_inner // bw
    specs = [col(0), col(1), col(2)]
    specs += [col(3 + i) for i in range(n_z)]
    specs += [col(3 + n_z + i) for i in range(n_z)]
    specs += [col(3 + 2 * n_z)]
    specs += [pl.BlockSpec((q, LANE), lambda b, c: (b * nc + c, 0))]
    for arr in (scw, cw, cb, dtb, alog, dexp, nw):
        specs.append(pl.BlockSpec(arr.shape, lambda b, c: (0, 0)))
    return pl.pallas_call(
        _ssd_prompt_kernel,
        grid=(bp, nc),
        in_specs=specs,
        out_specs=[pl.BlockSpec((q, dsc + d_inner), lambda b, c: (b * nc + c, 0)),
                   pl.BlockSpec((1, SUBLANE, dsc), lambda b, c: (b, 0, 0)),
                   pl.BlockSpec((1, SUBLANE, conv_dim), lambda b, c: (b, 0, 0)),
                   pl.BlockSpec((1, d_inner, SSD_STATE), lambda b, c: (b, 0, 0))],
        out_shape=[jax.ShapeDtypeStruct((bp * seq, dsc + d_inner), BF16),
                   jax.ShapeDtypeStruct((bp, SUBLANE, dsc), F32),
                   jax.ShapeDtypeStruct((bp, SUBLANE, conv_dim), F32),
                   jax.ShapeDtypeStruct((bp, d_inner, SSD_STATE), F32)],
        scratch_shapes=[pltpu.VMEM((SUBLANE, dsc), F32),
                        pltpu.VMEM((SUBLANE, conv_dim), F32),
                        pltpu.VMEM((SSD_STATE, d_inner), F32)],
        compiler_params=_cparams("parallel", "arbitrary"),
        name="ssd_prompt",
    )(*([proj] * (4 + 2 * n_z)), dt_raw, scw, cw, cb, dtb, alog, dexp, nw)


def _even_sample_prep_kernel(proj_ref, dt_ref, scst_ref, cvst_ref, scw_ref, cw_ref, cb_ref, dtb_ref, alog_ref,
                             ysc_ref, scn_ref, cvn_ref, xs_ref, b_ref, c_ref, xdt_t_ref, dec_ref):
    dsc = scw_ref.shape[1]
    conv_dim = cw_ref.shape[1]
    d_inner = xs_ref.shape[1]
    n_heads = d_inner // SSD_HEAD_DIM
    scb = proj_ref[:, :dsc]
    u = proj_ref[:, dsc:2 * dsc] * proj_ref[:, 2 * dsc:3 * dsc]
    xbc = proj_ref[:, 3 * dsc + d_inner:3 * dsc + d_inner + conv_dim]
    b0 = scst_ref[:, :dsc]
    b1 = scst_ref[:, dsc:]
    ysc_ref[...] = scb * (b0 * scw_ref[0:1, :] + b1 * scw_ref[1:2, :] + u * scw_ref[2:3, :])
    scn_ref[:, :dsc] = b1
    scn_ref[:, dsc:] = u

    c0 = cvst_ref[:, :conv_dim]
    c1 = cvst_ref[:, conv_dim:2 * conv_dim]
    c2 = cvst_ref[:, 2 * conv_dim:]
    act = _silu(c0 * cw_ref[0:1, :] + c1 * cw_ref[1:2, :] + c2 * cw_ref[2:3, :] + xbc * cw_ref[3:4, :] + cb_ref[...])
    cvn_ref[:, :conv_dim] = c1
    cvn_ref[:, conv_dim:2 * conv_dim] = c2
    cvn_ref[:, 2 * conv_dim:] = xbc
    xs = act[:, :d_inner]
    n_bc = (conv_dim - d_inner) // 2
    xs_ref[...] = xs
    b_ref[...] = act[:, d_inner:d_inner + n_bc]
    c_ref[...] = act[:, d_inner + n_bc:]

    dt = _softplus(dt_ref[...] + dtb_ref[...])
    dec_ref[...] = jnp.exp(dt * (-jnp.exp(alog_ref[...])))
    xdt_t_ref[...] = (xs * _expand_heads(dt, 0, n_heads // 2)).T


def _even_sample_prep(proj, dt_raw, scst, cvst, scw, cw, cb, dtb, alog, d_inner):
    bs = proj.shape[0]
    dsc = scw.shape[1]
    conv_dim = cw.shape[1]
    n_bc = (conv_dim - d_inner) // 2
    shapes = [(bs, dsc), (bs, 2 * dsc), (bs, 3 * conv_dim), (bs, d_inner), (bs, n_bc), (bs, n_bc),
              (d_inner, bs), (bs, LANE)]
    return pl.pallas_call(
        _even_sample_prep_kernel,
        out_shape=[jax.ShapeDtypeStruct(s, F32) for s in shapes],
        compiler_params=pltpu.CompilerParams(vmem_limit_bytes=VMEM_LIMIT),
        name="even_sample_prep",
    )(proj, dt_raw, scst, cvst, scw, cw, cb, dtb, alog)


def _ssm_sample_kernel(dec_ref, h0_ref, xt_ref, b_ref, c_ref, hn_ref, yt_ref):
    h = pl.program_id(0)
    bs = h0_ref.shape[0]
    p = h0_ref.shape[2]
    n = h0_ref.shape[3]
    lane = lax.broadcasted_iota(jnp.int32, (p, bs), 1)
    xt = xt_ref[...]
    ones = jnp.ones((n, bs), BF16)
    yt = jnp.zeros((p, bs), F32)
    for b in range(bs):
        xcol = jnp.broadcast_to(xt[:, b:b + 1], (p, n))
        hn = h0_ref[b, 0] * dec_ref[b, h] + xcol * b_ref[b:b + 1, :]
        hn_ref[b, 0] = hn
        ysum = jnp.zeros((p, bs), F32)
        rest = hn * c_ref[b:b + 1, :]
        for _ in range(2):
            piece = rest.astype(BF16)
            ysum = ysum + jnp.dot(piece, ones, preferred_element_type=F32)
            rest = rest - piece.astype(F32)
        yt = jnp.where(lane == b, ysum, yt)
    yt_ref[...] = yt


def _ssm_sample(h0, xdt_t, dec, bm, cm):
    bs, n_heads, p, n = h0.shape
    hpg = n_heads // SSD_GROUPS
    return pl.pallas_call(
        _ssm_sample_kernel,
        grid=(n_heads,),
        in_specs=[pl.BlockSpec(memory_space=pltpu.SMEM),
                  pl.BlockSpec((bs, 1, p, n), lambda h: (0, h, 0, 0)),
                  pl.BlockSpec((p, bs), lambda h: (h, 0)),
                  pl.BlockSpec((bs, n), lambda h: (0, h // hpg)),
                  pl.BlockSpec((bs, n), lambda h: (0, h // hpg))],
        out_specs=[pl.BlockSpec((bs, 1, p, n), lambda h: (0, h, 0, 0)),
                   pl.BlockSpec((p, bs), lambda h: (h, 0))],
        out_shape=[jax.ShapeDtypeStruct(h0.shape, F32), jax.ShapeDtypeStruct((n_heads * p, bs), F32)],
        compiler_params=_cparams("parallel"),
        name="ssm_sample",
    )(dec, h0, xdt_t, bm, cm)


def _even_out_sample_kernel(yt_ref, xs_ref, proj_ref, ysc_ref, dexp_ref, nw_ref, w_ref, x_ref, g_ref, o_ref):
    d_inner = xs_ref.shape[1]
    dsc = ysc_ref.shape[1]
    gw = d_inner // SSD_GROUPS
    y = yt_ref[...].T + dexp_ref[...] * xs_ref[...]
    y = y * _silu(proj_ref[:, 3 * dsc:3 * dsc + d_inner])
    parts = []
    for g in range(SSD_GROUPS):
        yg = y[:, g * gw:(g + 1) * gw]
        ms = jnp.mean(yg * yg, axis=-1, keepdims=True)
        parts.append(yg * lax.rsqrt(ms + EPS))
    y_ssd = jnp.concatenate(parts, axis=1) * nw_ref[...]
    cat = jnp.concatenate([ysc_ref[...], y_ssd], axis=1).astype(BF16)
    o_ref[...] = x_ref[...] + g_ref[0] * jnp.dot(cat, w_ref[...], preferred_element_type=F32)


def _even_out_sample(y_t, xs, proj, ysc, dexp, nw, w_out, x, mod):
    bs, d = x.shape
    full = lambda a: pl.BlockSpec(a.shape, lambda i: (0,) * a.ndim)
    return pl.pallas_call(
        _even_out_sample_kernel,
        grid=(1,),
        in_specs=[full(y_t), full(xs), full(proj), full(ysc), full(dexp), full(nw), full(w_out), full(x),
                  pl.BlockSpec((1, bs, d), lambda i: (0, 0, 2))],
        out_specs=pl.BlockSpec((bs, d), lambda i: (0, 0)),
        out_shape=jax.ShapeDtypeStruct((bs, d), F32),
        compiler_params=_cparams("arbitrary"),
        name="even_out_sample",
    )(y_t, xs, proj, ysc, dexp, nw, w_out, x, mod)


def _rope_table_kernel(ca_ref, cb_ref, cc_ref, *, pos0, stride):
    rows = ca_ref.shape[0]
    row = lax.broadcasted_iota(jnp.int32, (rows, LANE), 0) + pl.program_id(0) * rows
    d = lax.broadcasted_iota(jnp.int32, (rows, LANE), 1) % ATT_HEAD_DIM
    half = ROT_DIM // 2
    pos = (pos0 + row * stride).astype(F32)
    inv = jnp.power(jnp.full((rows, LANE), ROPE_THETA, F32), -((d % half).astype(F32) * 2.0 / ROT_DIM))
    ang = pos * inv
    cs = jnp.cos(ang)
    sn = jnp.sin(ang)
    ca_ref[...] = jnp.where(d < ROT_DIM, cs, 1.0)
    cb_ref[...] = jnp.where(d < half, -sn, 0.0)
    cc_ref[...] = jnp.where(d < half, 0.0, jnp.where(d < ROT_DIM, sn, 0.0))


def _rope_tables(rows, pos0, stride):
    tr = min(rows, 512)
    spec = pl.BlockSpec((tr, LANE), lambda i: (i, 0))
    return pl.pallas_call(
        functools.partial(_rope_table_kernel, pos0=pos0, stride=stride),
        grid=(rows // tr,),
        out_specs=[spec, spec, spec],
        out_shape=[jax.ShapeDtypeStruct((rows, LANE), F32)] * 3,
        compiler_params=_cparams("parallel"),
        name="rope_tables",
    )()


def _rope(t, ca, cb, cc):
    cols = []
    for c in range(t.shape[1] // LANE):
        s = t[:, c * LANE:(c + 1) * LANE]
        cols.append(s * ca + pltpu.roll(s, LANE - ROT_DIM // 2, axis=1) * cb + pltpu.roll(s, ROT_DIM // 2, axis=1) * cc)
    return jnp.concatenate(cols, axis=1)


def _sel_keys(k, j):
    lane = lax.broadcasted_iota(jnp.int32, (k.shape[0], LANE), 1)
    onehot = jnp.where(lane == j, 1.0, 0.0)
    in_sel = lane < SEL_LANES
    k_lo = jnp.concatenate([jnp.where(in_sel, onehot, k[:, :LANE]), k[:, LANE:]], axis=1)
    k_hi = jnp.concatenate([k[:, :LANE], jnp.where(in_sel, onehot, k[:, LANE:])], axis=1)
    return k_lo.astype(BF16), k_hi.astype(BF16)


def _qkv_kernel(x_ref, nw_ref, sh_ref, sc_ref, w_ref, ca_ref, cb_ref, cc_ref, q_ref, k_ref, v_ref, *blk_refs,
                blocks_per_seq):
    nq = q_ref.shape[1]
    nk = (w_ref.shape[1] - nq) // 2
    hb = _norm_mod(x_ref[...], nw_ref[...], sh_ref[0], sc_ref[0]).astype(BF16)
    qkv = jnp.dot(hb, w_ref[...], preferred_element_type=F32)
    ca, cb, cc = ca_ref[...], cb_ref[...], cc_ref[...]
    q_ref[...] = _rope(qkv[:, :nq], ca, cb, cc)
    k = _rope(qkv[:, nq:nq + nk], ca, cb, cc)
    v = qkv[:, nq + nk:]
    if blocks_per_seq:
        km_ref, klo_ref, khi_ref, vt_ref = blk_refs
        vt = v.T
        k_ref[0] = k.T
        v_ref[0] = vt
        km_ref[0] = jnp.mean(k, axis=0, keepdims=True)
        klo_ref[...], khi_ref[...] = _sel_keys(k, pl.program_id(0) % blocks_per_seq)
        hd = ATT_HEAD_DIM
        ones_row = jnp.where(lax.broadcasted_iota(jnp.int32, (BF16_SUBLANE, vt.shape[1]), 0) == 0, 1.0, 0.0)
        vt_ref[0] = jnp.concatenate([piece for h in range(nk // hd) for piece in (vt[h * hd:(h + 1) * hd], ones_row)],
                                    axis=0).astype(BF16)
    else:
        k_ref[...] = k
        v_ref[...] = v


def _qkv(x, nw, mod, w, tabs, rows_per_group, tm, per_block):
    t, d = x.shape
    nq = ATT_HEADS * ATT_HEAD_DIM
    nk = ATT_KV_HEADS * ATT_HEAD_DIM
    tab_tiles = tabs[0].shape[0] // tm
    tab = pl.BlockSpec((tm, LANE), lambda i: (i % tab_tiles, 0))
    row = lambda n: pl.BlockSpec((tm, n), lambda i: (i, 0))
    if per_block:
        bps = rows_per_group // tm
        kv_t = pl.BlockSpec((1, nk, tm), lambda i: (i // bps, 0, i % bps))
        v_rows = ATT_KV_HEADS * (ATT_HEAD_DIM + BF16_SUBLANE)
        out_specs = [row(nq), kv_t, kv_t, pl.BlockSpec((1, 1, nk), lambda i: (i, 0, 0)), row(nk), row(nk),
                     pl.BlockSpec((1, v_rows, tm), lambda i: (i, 0, 0))]
        out_shape = [jax.ShapeDtypeStruct((t, nq), F32),
                     jax.ShapeDtypeStruct((t // rows_per_group, nk, rows_per_group), F32),
                     jax.ShapeDtypeStruct((t // rows_per_group, nk, rows_per_group), F32),
                     jax.ShapeDtypeStruct((t // tm, 1, nk), F32), jax.ShapeDtypeStruct((t, nk), BF16),
                     jax.ShapeDtypeStruct((t, nk), BF16), jax.ShapeDtypeStruct((t // tm, v_rows, tm), BF16)]
    else:
        out_specs = [row(nq), row(nk), row(nk)]
        out_shape = [jax.ShapeDtypeStruct((t, nq), F32), jax.ShapeDtypeStruct((t, nk), F32),
                     jax.ShapeDtypeStruct((t, nk), F32)]
    return pl.pallas_call(
        functools.partial(_qkv_kernel, blocks_per_seq=rows_per_group // tm if per_block else 0),
        grid=(t // tm,),
        in_specs=[row(d), pl.BlockSpec((1, d), lambda i: (0, 0)),
                  _mod_spec(mod, 0, rows_per_group, tm), _mod_spec(mod, 1, rows_per_group, tm),
                  _resident(w.shape), tab, tab, tab],
        out_specs=out_specs,
        out_shape=out_shape,
        compiler_params=_cparams("parallel"),
        name="qkv_rope",
    )(x, nw, mod, mod, w, *tabs)


def _select_bias(scores, n_past, n_rounds):
    lane = lax.broadcasted_iota(jnp.int32, scores.shape, 1)
    lanef = lane.astype(F32)
    s = jnp.where(lane < n_past, scores, -jnp.inf)
    bias = jnp.full(scores.shape, NEG_BIG, F32)
    for r in range(n_rounds):
        mx = jnp.max(s, axis=1, keepdims=True)
        idx = jnp.min(jnp.where(s == mx, lanef, float(LANE)), axis=1, keepdims=True)
        bias = jnp.where(lanef == jnp.where(n_past > r, idx, -1.0), 0.0, bias)
        s = jnp.where(lanef == idx, -jnp.inf, s)
    return bias


def _select_bias_t(scores, n_past, n_rounds):
    row = lax.broadcasted_iota(jnp.int32, scores.shape, 0)
    rowf = row.astype(F32)
    s = jnp.where(row < n_past, scores, -jnp.inf)
    bias = jnp.full(scores.shape, NEG_BIG, F32)
    for r in range(n_rounds):
        mx = jnp.max(s, axis=0, keepdims=True)
        idx = jnp.min(jnp.where(s == mx, rowf, float(SEL_LANES)), axis=0, keepdims=True)
        bias = jnp.where(rowf == jnp.where(n_past > r, idx, -1.0), 0.0, bias)
        s = jnp.where(rowf == idx, -jnp.inf, s)
    return jnp.where(row == n_past, 0.0, bias)


def _moba_prompt_kernel(q_ref, klo_ref, khi_ref, vt_ref, km_ref, o_ref, qa_scr, m_scr, over_scr, acc_scr):
    qi = pl.program_id(1)
    blk = q_ref.shape[0]
    kvw = klo_ref.shape[1]
    hd = ATT_HEAD_DIM
    n_q = ATT_GROUP * blk
    kmean = km_ref[0]
    km_hi = kmean.astype(BF16)
    km_lo = (kmean - km_hi.astype(F32)).astype(BF16)
    n_rounds = min(MOBA_TOPK, klo_ref.shape[0] // blk)

    q_t = q_ref[...].T
    for h in range(ATT_KV_HEADS):
        qh_b = jnp.concatenate([q_t[g * kvw + h * hd:g * kvw + (h + 1) * hd, :] for g in range(ATT_GROUP)],
                               axis=1).astype(BF16)
        scores_t = (jnp.dot(km_hi[:, h * hd:(h + 1) * hd], qh_b, preferred_element_type=F32)
                    + jnp.dot(km_lo[:, h * hd:(h + 1) * hd], qh_b, preferred_element_type=F32))
        bias_t = _select_bias_t(scores_t, qi, n_rounds)
        sel_row = LANE if h == 0 else 0
        qa_scr[h] = jnp.zeros((kvw, n_q), BF16)
        qa_scr[h, h * hd:(h + 1) * hd, :] = qh_b
        qa_scr[h, sel_row:sel_row + SEL_LANES, :] = bias_t.astype(BF16)

    v_rows = vt_ref.shape[1] // ATT_KV_HEADS
    key_row = lax.broadcasted_iota(jnp.int32, (blk, n_q), 0)
    q_pos = lax.broadcasted_iota(jnp.int32, (blk, n_q), 1) % blk

    def scores(h, j):
        off = pl.multiple_of(j * blk, blk)
        ka = (khi_ref if h == 0 else klo_ref)[pl.ds(off, blk), :]
        return jnp.dot(ka, qa_scr[h], preferred_element_type=F32)

    def softmax(h, s, first):
        if first:
            s = jnp.where(key_row <= q_pos, s, NEG_BIG)
        mx = jnp.max(s, axis=0, keepdims=True)
        if first:
            m_new, alpha = mx, None
        else:
            m_old = m_scr[h, 0:1, :]
            m_new = jnp.maximum(m_old, mx)
            alpha = jnp.exp2(m_old - m_new)
        m_scr[h] = jnp.broadcast_to(m_new, (SUBLANE, n_q))
        return jnp.exp2((s - m_new).astype(BF16)), alpha

    def values(h, j, p, alpha):
        vt_h = vt_ref[j, h * v_rows:(h + 1) * v_rows, :]
        pv = jnp.dot(vt_h, p, preferred_element_type=F32)
        acc_scr[h] = pv if alpha is None else acc_scr[h] * alpha + pv

    def block_step(j, first):
        s_next = scores(0, j)
        for h in range(ATT_KV_HEADS):
            s = s_next
            if h + 1 < ATT_KV_HEADS:
                s_next = scores(h + 1, j)
            p, alpha = softmax(h, s, first)
            values(h, j, p, alpha)

    def finalize():
        for g in range(ATT_GROUP):
            cols = slice(g * blk, (g + 1) * blk)
            o_t = jnp.concatenate([acc_scr[h, :hd, cols] / acc_scr[h, hd:hd + 1, cols]
                                   for h in range(ATT_KV_HEADS)], axis=0)
            o_ref[:, g * kvw:(g + 1) * kvw] = o_t.T.astype(BF16)

    block_step(qi, True)

    for h in range(ATT_KV_HEADS):
        over_scr[h] = jnp.full((SUBLANE, n_q), NEG_BIG, F32)

    def fast_body(j, carry):
        s_next = scores(0, j)
        for h in range(ATT_KV_HEADS):
            s = s_next
            if h + 1 < ATT_KV_HEADS:
                s_next = scores(h + 1, j)
            d = s - m_scr[h, 0:1, :]
            over_scr[h] = jnp.maximum(over_scr[h], jnp.broadcast_to(jnp.max(d, axis=0, keepdims=True), (SUBLANE, n_q)))
            p = jnp.exp2(jnp.minimum(d, EXP2_CLAMP)).astype(BF16)
            vt_h = vt_ref[j, h * v_rows:(h + 1) * v_rows, :]
            acc_scr[h] = acc_scr[h] + jnp.dot(vt_h, p, preferred_element_type=F32)
        return carry

    lax.fori_loop(0, qi, fast_body, 0)
    finalize()

    @pl.when(jnp.max(over_scr[...]) > EXP2_CLAMP)
    def _():
        block_step(qi, True)

        def body(j, carry):
            block_step(j, False)
            return carry

        lax.fori_loop(0, qi, body, 0)
        finalize()


def _moba_prompt(q, k_lo, k_hi, v_t, kmean, bp, seq):
    blk = MOBA_BLOCK
    nq = seq // blk
    kvw = k_lo.shape[1]
    n_q = ATT_GROUP * blk
    v_rows = v_t.shape[1]
    return pl.pallas_call(
        _moba_prompt_kernel,
        grid=(bp, nq),
        in_specs=[pl.BlockSpec((blk, q.shape[1]), lambda b, i: (b * nq + i, 0)),
                  pl.BlockSpec((seq, kvw), lambda b, i: (b, 0)),
                  pl.BlockSpec((seq, kvw), lambda b, i: (b, 0)),
                  pl.BlockSpec((nq, v_rows, blk), lambda b, i: (b, 0, 0)),
                  pl.BlockSpec((1, SEL_LANES, kvw), lambda b, i: (b, 0, 0))],
        out_specs=pl.BlockSpec((blk, q.shape[1]), lambda b, i: (b * nq + i, 0)),
        out_shape=jax.ShapeDtypeStruct(q.shape, BF16),
        scratch_shapes=[pltpu.VMEM((ATT_KV_HEADS, kvw, n_q), BF16),
                        pltpu.VMEM((ATT_KV_HEADS, SUBLANE, n_q), F32),
                        pltpu.VMEM((ATT_KV_HEADS, SUBLANE, n_q), F32),
                        pltpu.VMEM((ATT_KV_HEADS, v_rows // ATT_KV_HEADS, n_q), F32)],
        compiler_params=_cparams("parallel", "arbitrary"),
        name="moba_prompt",
    )(q, k_lo, k_hi, v_t, kmean)


def _moba_sample_kernel(pt_ref, q_ref, kn_ref, vn_ref, avg_ref, *rest, n_pages):
    k_pages = rest[:n_pages]
    v_pages = rest[n_pages:2 * n_pages]
    o_ref = rest[2 * n_pages]
    hd = ATT_HEAD_DIM
    kvw = kn_ref.shape[2]
    page = k_pages[0].shape[3]
    n_past = n_pages * page // MOBA_BLOCK
    n_rows = ATT_KV_HEADS * SUBLANE
    q = q_ref[0]
    kn = kn_ref[0]
    vn = vn_ref[0]
    row = lax.broadcasted_iota(jnp.int32, (n_rows, hd), 0)
    q_rows = jnp.zeros((n_rows, hd), F32)
    kn_rows = jnp.zeros((n_rows, hd), F32)
    vn_rows = jnp.zeros((n_rows, hd), F32)
    for h in range(ATT_KV_HEADS):
        in_h = row // SUBLANE == h
        kn_rows = jnp.where(in_h, jnp.broadcast_to(kn[:, h * hd:(h + 1) * hd], (n_rows, hd)), kn_rows)
        vn_rows = jnp.where(in_h, jnp.broadcast_to(vn[:, h * hd:(h + 1) * hd], (n_rows, hd)), vn_rows)
        for g in range(ATT_GROUP):
            lo = g * kvw + h * hd
            q_rows = jnp.where(row == h * SUBLANE + g, jnp.broadcast_to(q[:, lo:lo + hd], (n_rows, hd)), q_rows)
    q_b = q_rows.astype(BF16)

    score_parts, logit_parts = [], []
    for h in range(ATT_KV_HEADS):
        rows = slice(h * SUBLANE, (h + 1) * SUBLANE)
        kt = jnp.concatenate([r[0, h] for r in k_pages], axis=1)
        kt_b = kt.astype(BF16)
        kt_lo = (kt - kt_b.astype(F32)).astype(BF16)
        kmt = (jnp.dot(kt_b, avg_ref[...], preferred_element_type=F32)
               + jnp.dot(kt_lo, avg_ref[...], preferred_element_type=F32))
        score_parts.append(jnp.dot(q_rows[rows], kmt, precision=HIGHEST, preferred_element_type=F32))
        logit_parts.append(jnp.dot(q_b[rows], kt_b, preferred_element_type=F32))
    bias = _select_bias(jnp.concatenate(score_parts, axis=0), n_past, min(MOBA_TOPK, n_past + 1))
    logits = jnp.concatenate(logit_parts, axis=0)
    vts = [jnp.concatenate([r[0, h] for r in v_pages], axis=1).astype(BF16) for h in range(ATT_KV_HEADS)]

    lane = lax.broadcasted_iota(jnp.int32, (n_rows, LANE), 1)
    blk_max = jnp.full((n_rows, LANE), NEG_BIG, F32)
    blk_sum = jnp.zeros((n_rows, LANE), F32)
    blk_out = []
    for j in range(n_past):
        keys = slice(j * MOBA_BLOCK, (j + 1) * MOBA_BLOCK)
        lj = logits[:, keys]
        mj = jnp.max(lj, axis=1, keepdims=True)
        pj = jnp.exp2(lj - mj)
        blk_max = jnp.where(lane == j, mj, blk_max)
        blk_sum = jnp.where(lane == j, jnp.sum(pj, axis=1, keepdims=True), blk_sum)
        pj_b = pj.astype(BF16)
        blk_out.append(jnp.concatenate(
            [lax.dot_general(pj_b[h * SUBLANE:(h + 1) * SUBLANE], vts[h][:, keys], NT_DIMS, preferred_element_type=F32)
             for h in range(ATT_KV_HEADS)], axis=0))
    l_new = jnp.sum(q_rows * kn_rows, axis=1, keepdims=True)
    chosen = blk_max + bias
    m = jnp.maximum(jnp.max(chosen, axis=1, keepdims=True), l_new)
    w = jnp.exp2(chosen - m)
    pn = jnp.exp2(l_new - m)
    denom = jnp.sum(w * blk_sum, axis=1, keepdims=True) + pn
    o = pn * vn_rows
    for j in range(n_past):
        o = o + w[:, j:j + 1] * blk_out[j]
    o = o / denom
    for h in range(ATT_KV_HEADS):
        for g in range(ATT_GROUP):
            lo = g * kvw + h * hd
            o_ref[0, :, lo:lo + hd] = o[h * SUBLANE + g:h * SUBLANE + g + 1, :]


def _moba_sample(q, kn, vn, cache_kt, cache_vt, page_table):
    bs = q.shape[0]
    n_pages = page_table.shape[1]
    _, n_kv, hd, page = cache_kt.shape
    kvw = n_kv * hd
    past = n_pages * page
    one = lambda n: pl.BlockSpec((1, 1, n), lambda b, pt: (b, 0, 0))
    avg = jnp.where(jnp.arange(past)[:, None] // MOBA_BLOCK == jnp.arange(LANE)[None, :], 1.0 / MOBA_BLOCK, 0.0).astype(BF16)

    def page_spec(p):
        return pl.BlockSpec((1, n_kv, hd, page), lambda b, pt: (pt[b, p], 0, 0, 0))

    grid_spec = pltpu.PrefetchScalarGridSpec(
        num_scalar_prefetch=1,
        grid=(bs,),
        in_specs=[one(q.shape[1]), one(kvw), one(kvw), pl.BlockSpec((past, LANE), lambda b, pt: (0, 0))]
        + [page_spec(p) for p in range(n_pages)] * 2,
        out_specs=one(q.shape[1]),
    )
    return pl.pallas_call(
        functools.partial(_moba_sample_kernel, n_pages=n_pages),
        grid_spec=grid_spec,
        out_shape=jax.ShapeDtypeStruct((bs, 1, q.shape[1]), F32),
        compiler_params=_cparams("parallel"),
        name="moba_sample",
    )(page_table, q.reshape(bs, 1, -1), kn.reshape(bs, 1, kvw), vn.reshape(bs, 1, kvw), avg,
      *([cache_kt] * n_pages), *([cache_vt] * n_pages))


def _pad_lanes(v, fill=0.0):
    return jnp.pad(v.reshape(1, -1), ((0, 0), (0, LANE - v.shape[0])), constant_values=fill)


def kernel(x_prompt, x_sample, state_sc_conv, state_ssd_conv, state_ssm, cache_k, cache_v, page_table,
           c_prompt, c_sample, norm_mix_w, norm_mlp_w, norm_final_w, w_ada, b_ada, w_in_e, sc_conv_w,
           ssd_conv_w, ssd_conv_b, ssd_dt_bias, ssd_a_log, ssd_d, ssd_norm_w, w_out_e, w_qkv, w_o, w_up, w_down):
    bp, seq, d = x_prompt.shape
    bs, dec_seq, _ = x_sample.shape
    assert dec_seq == 1 and seq % MOBA_BLOCK == 0 and seq % SSD_CHUNK == 0 and seq // MOBA_BLOCK <= SEL_LANES
    n_pages, page = page_table.shape[1], cache_k.shape[2]
    past_len = n_pages * page
    assert past_len % MOBA_BLOCK == 0 and MOBA_BLOCK % page == 0
    dsc = sc_conv_w.shape[2]
    conv_dim = ssd_conv_w.shape[2]
    n_heads = ssd_d.shape[1]
    d_inner = n_heads * SSD_HEAD_DIM
    main = 3 * dsc + d_inner + conv_dim
    kvw = ATT_KV_HEADS * ATT_HEAD_DIM
    nq = ATT_HEADS * ATT_HEAD_DIM

    xp = x_prompt.reshape(bp * seq, d)
    xs = x_sample.reshape(bs, d)

    rows = -(-(bs + bp) // SUBLANE) * SUBLANE
    c_all = jnp.pad(jnp.concatenate([c_sample, c_prompt], axis=0), ((0, rows - bs - bp), (0, 0)))
    ada = _ada(c_all, w_ada, b_ada)

    w_in = w_in_e[0][:, :main].astype(BF16)
    w_dt = jnp.pad(w_in_e[0][:, main:], ((0, 0), (0, LANE - n_heads))).astype(BF16)
    w_out = w_out_e[0].astype(BF16)
    wq = w_qkv[0][:, :nq].reshape(d, ATT_KV_HEADS, ATT_GROUP, ATT_HEAD_DIM).transpose(0, 2, 1, 3).reshape(d, nq)
    w_qkv_p = jnp.concatenate([wq * (ATT_HEAD_DIM ** -0.5 * math.log2(math.e)), w_qkv[0][:, nq:]], axis=1).astype(BF16)
    w_o_p = w_o[0].reshape(ATT_KV_HEADS, ATT_GROUP, ATT_HEAD_DIM, d).transpose(1, 0, 2, 3).reshape(nq, d).astype(BF16)
    w_up_b = w_up.astype(BF16)
    w_down_b = w_down.astype(BF16)

    scw = sc_conv_w[0]
    cw = ssd_conv_w[0]
    cb = ssd_conv_b[0].reshape(1, conv_dim)
    dtb = _pad_lanes(ssd_dt_bias[0])
    alog = _pad_lanes(ssd_a_log[0])
    dexp = jnp.repeat(ssd_d[0], SSD_HEAD_DIM).reshape(1, d_inner)
    ssd_nw = ssd_norm_w[0].reshape(1, d_inner)

    mod_p = ada[0, bs:bs + bp].reshape(bp, 1, 6 * d)
    mod_s = ada[0, :bs].reshape(1, bs, 6 * d)
    nw_mix = norm_mix_w[0].reshape(1, d)
    nw_mlp = norm_mlp_w[0].reshape(1, d)
    fw = norm_final_w.reshape(1, d)

    proj_p, dt_p = _nm_matmul(xp, nw_mix, mod_p, 0, 1, w_in, w_dt, seq, min(seq, 1024), 2048)
    ycat_p, scst_p, cvst_p, ssm_p = _ssd_prompt(proj_p, dt_p, scw, cw, cb, dtb, alog, dexp, ssd_nw, bp, seq)
    xp = _mm_res(ycat_p, w_out, xp, mod_p, 2, seq, 512)
    xp = _mlp(xp, nw_mlp, mod_p, w_up_b[0], w_down_b[0], fw, seq, 512, False)

    proj_s, dt_s = _nm_matmul(xs, nw_mix, mod_s, 0, 1, w_in, w_dt, bs, bs, 1024)
    ysc_s, scn_s, cvn_s, xs_s, b_s, c_s, xdt_t, dec_s = _even_sample_prep(
        proj_s, dt_s, state_sc_conv[0].reshape(bs, -1), state_ssd_conv[0].reshape(bs, -1),
        scw, cw, cb, dtb, alog, d_inner)
    ssm_s, y_t = _ssm_sample(state_ssm[0], xdt_t, dec_s, b_s, c_s)
    xs = _even_out_sample(y_t, xs_s, proj_s, ysc_s, dexp, ssd_nw, w_out, xs, mod_s)
    xs = _mlp(xs, nw_mlp, mod_s, w_up_b[0], w_down_b[0], fw, bs, bs, False)

    mod_p = ada[1, bs:bs + bp].reshape(bp, 1, 6 * d)
    mod_s = ada[1, :bs].reshape(1, bs, 6 * d)
    nw_mix = norm_mix_w[1].reshape(1, d)
    nw_mlp = norm_mlp_w[1].reshape(1, d)

    tabs_p = _rope_tables(seq, 0, 1)
    q_p, kt_p, vt_p, km_p, klo_p, khi_p, vtb_p = _qkv(xp, nw_mix, mod_p, w_qkv_p, tabs_p, seq, MOBA_BLOCK, True)
    n_blk = seq // MOBA_BLOCK
    kmean_p = jnp.pad(km_p.reshape(bp, n_blk, kvw), ((0, 0), (0, SEL_LANES - n_blk), (0, 0)))
    o_p = _moba_prompt(q_p, klo_p, khi_p, vtb_p, kmean_p, bp, seq)
    xp = _mm_res(o_p, w_o_p, xp, mod_p, 2, seq, 512)
    y_prompt = _mlp(xp, nw_mlp, mod_p, w_up_b[1], w_down_b[1], fw, seq, 512, True)

    tabs_s = _rope_tables(bs, past_len, 0)
    q_s, k_s, v_s = _qkv(xs, nw_mix, mod_s, w_qkv_p, tabs_s, bs, bs, False)
    o_s = _moba_sample(q_s, k_s, v_s, cache_k[0].transpose(0, 2, 3, 1), cache_v[0].transpose(0, 2, 3, 1), page_table)
    xs = _mm_res(o_s.reshape(bs, nq).astype(BF16), w_o_p, xs, mod_s, 2, bs, bs)
    y_sample = _mlp(xs, nw_mlp, mod_s, w_up_b[1], w_down_b[1], fw, bs, bs, True)

    def kv_out(t):
        return t.reshape(bp, ATT_KV_HEADS, ATT_HEAD_DIM, seq).transpose(0, 3, 1, 2)[None]

    return (y_prompt.reshape(bp, seq, d), y_sample.reshape(bs, 1, d),
            scst_p[:, SUBLANE - 2:, :][None], scn_s.reshape(1, bs, 2, dsc),
            cvst_p[:, SUBLANE - 3:, :][None], cvn_s.reshape(1, bs, 3, conv_dim),
            ssm_p.reshape(1, bp, n_heads, SSD_HEAD_DIM, SSD_STATE), ssm_s[None],
            kv_out(kt_p), k_s.reshape(1, bs, 1, ATT_KV_HEADS, ATT_HEAD_DIM),
            kv_out(vt_p), v_s.reshape(1, bs, 1, ATT_KV_HEADS, ATT_HEAD_DIM))
```

```python
import functools
import math

import jax
import jax.numpy as jnp
from jax import lax
from jax.experimental import pallas as pl
from jax.experimental.pallas import tpu as pltpu

F32 = jnp.float32
BF16 = jnp.bfloat16
HIGHEST = lax.Precision.HIGHEST

EPS = 1e-6
LANE = 128
SUBLANE = 8
BF16_SUBLANE = 16
SSD_HEAD_DIM = 64
SSD_GROUPS = 4
SSD_STATE = 128
SSD_CHUNK = 256
ATT_HEADS = 16
ATT_KV_HEADS = 4
ATT_GROUP = ATT_HEADS // ATT_KV_HEADS
ATT_HEAD_DIM = 64
ROT_DIM = ATT_HEAD_DIM // 4
ROPE_THETA = 500000.0
MOBA_BLOCK = 256
MOBA_TOPK = 3
SEL_LANES = 16
NEG_BIG = -1e30
EXP2_CLAMP = 60.0
VMEM_LIMIT = 56 * 1024 * 1024

NT_DIMS = (((1,), (1,)), ((), ()))


def _cparams(*sem):
    return pltpu.CompilerParams(dimension_semantics=sem, vmem_limit_bytes=VMEM_LIMIT)


def _silu(x):
    return x * jax.nn.sigmoid(x)


def _softplus(x):
    return jnp.maximum(x, 0.0) + jnp.log1p(jnp.exp(-jnp.abs(x)))


def _norm_mod(x, nw, shift, scale):
    ms = jnp.mean(x * x, axis=-1, keepdims=True)
    xn = x * lax.rsqrt(ms + EPS) * nw
    return xn * (1.0 + scale) + shift


def _resident(shape):
    nd = len(shape)
    return pl.BlockSpec(shape, lambda *_: (0,) * nd, pipeline_mode=pl.Buffered(1))


def _ada_kernel(c_ref, w_ref, b_ref, o_ref):
    s = _silu(c_ref[...]).astype(BF16)
    o_ref[0] = jnp.dot(s, w_ref[0].astype(BF16), preferred_element_type=F32) + b_ref[0]


def _ada(c_all, w_ada, b_ada):
    rows, d = c_all.shape
    n_layers, _, n6 = w_ada.shape
    tn = 1024
    return pl.pallas_call(
        _ada_kernel,
        grid=(n_layers, n6 // tn),
        in_specs=[pl.BlockSpec((rows, d), lambda l, j: (0, 0)),
                  pl.BlockSpec((1, d, tn), lambda l, j: (l, 0, j)),
                  pl.BlockSpec((1, 1, tn), lambda l, j: (l, 0, j))],
        out_specs=pl.BlockSpec((1, rows, tn), lambda l, j: (l, 0, j)),
        out_shape=jax.ShapeDtypeStruct((n_layers, rows, n6), F32),
        compiler_params=_cparams("parallel", "parallel"),
        name="adaln",
    )(c_all, w_ada, b_ada.reshape(n_layers, 1, n6))


def _mod_spec(mod, k, rows_per_group, tm):
    _, r, n6 = mod.shape
    d = n6 // 6
    tiles_per_group = max(rows_per_group // tm, 1)
    return pl.BlockSpec((1, r, d), lambda i, *_: (i // tiles_per_group, 0, k))


def _nm_matmul_kernel(x_ref, nw_ref, sh_ref, sc_ref, w_ref, w2_ref, o_ref, o2_ref, h_scr):
    @pl.when(pl.program_id(1) == 0)
    def _():
        hb = _norm_mod(x_ref[...], nw_ref[...], sh_ref[0], sc_ref[0]).astype(BF16)
        h_scr[...] = hb
        o2_ref[...] = jnp.dot(hb, w2_ref[...], preferred_element_type=F32)

    o_ref[...] = jnp.dot(h_scr[...], w_ref[...], preferred_element_type=F32)


def _nm_matmul(x, nw, mod, k_shift, k_scale, w, w2, rows_per_group, tm, tn):
    t, d = x.shape
    n = w.shape[1]
    n2 = w2.shape[1]
    return pl.pallas_call(
        _nm_matmul_kernel,
        grid=(t // tm, n // tn),
        in_specs=[pl.BlockSpec((tm, d), lambda i, j: (i, 0)),
                  pl.BlockSpec((1, d), lambda i, j: (0, 0)),
                  _mod_spec(mod, k_shift, rows_per_group, tm),
                  _mod_spec(mod, k_scale, rows_per_group, tm),
                  pl.BlockSpec((d, tn), lambda i, j: (0, j)),
                  pl.BlockSpec((d, n2), lambda i, j: (0, 0))],
        out_specs=[pl.BlockSpec((tm, tn), lambda i, j: (i, j)),
                   pl.BlockSpec((tm, n2), lambda i, j: (i, 0))],
        out_shape=[jax.ShapeDtypeStruct((t, n), F32), jax.ShapeDtypeStruct((t, n2), F32)],
        scratch_shapes=[pltpu.VMEM((tm, d), BF16)],
        compiler_params=_cparams("parallel", "arbitrary"),
        name="in_proj",
    )(x, nw, mod, mod, w, w2)


def _mm_res_kernel(a_ref, w_ref, res_ref, g_ref, o_ref):
    o_ref[...] = res_ref[...] + g_ref[0] * jnp.dot(a_ref[...], w_ref[...], preferred_element_type=F32)


def _mm_res(a, w, res, mod, k_gate, rows_per_group, tm):
    t, kdim = a.shape
    d = w.shape[1]
    return pl.pallas_call(
        _mm_res_kernel,
        grid=(t // tm,),
        in_specs=[pl.BlockSpec((tm, kdim), lambda i: (i, 0)),
                  _resident((kdim, d)),
                  pl.BlockSpec((tm, d), lambda i: (i, 0)),
                  _mod_spec(mod, k_gate, rows_per_group, tm)],
        out_specs=pl.BlockSpec((tm, d), lambda i: (i, 0)),
        out_shape=jax.ShapeDtypeStruct((t, d), F32),
        compiler_params=_cparams("parallel"),
        name="proj_residual",
    )(a, w, res, mod)


def _mlp_kernel(x_ref, nw_ref, sh_ref, sc_ref, g_ref, wu_ref, wd_ref, fw_ref, o_ref, *, final_norm, tf):
    x = x_ref[...]
    hb = _norm_mod(x, nw_ref[...], sh_ref[0], sc_ref[0]).astype(BF16)
    acc = jnp.zeros(x.shape, F32)
    for f in range(wu_ref.shape[1] // tf):
        a = jnp.maximum(jnp.dot(hb, wu_ref[:, f * tf:(f + 1) * tf], preferred_element_type=F32), 0.0)
        acc = acc + jnp.dot((a * a).astype(BF16), wd_ref[f * tf:(f + 1) * tf, :], preferred_element_type=F32)
    y = x + g_ref[0] * acc
    if final_norm:
        ms = jnp.mean(y * y, axis=-1, keepdims=True)
        y = y * lax.rsqrt(ms + EPS) * fw_ref[...]
    o_ref[...] = y


def _mlp(x, nw, mod, w_up, w_down, fw, rows_per_group, tm, final_norm):
    t, d = x.shape
    dff = w_up.shape[1]
    return pl.pallas_call(
        functools.partial(_mlp_kernel, final_norm=final_norm, tf=1024),
        grid=(t // tm,),
        in_specs=[pl.BlockSpec((tm, d), lambda i: (i, 0)),
                  pl.BlockSpec((1, d), lambda i: (0, 0)),
                  _mod_spec(mod, 3, rows_per_group, tm),
                  _mod_spec(mod, 4, rows_per_group, tm),
                  _mod_spec(mod, 5, rows_per_group, tm),
                  _resident((d, dff)),
                  _resident((dff, d)),
                  pl.BlockSpec((1, d), lambda i: (0, 0))],
        out_specs=pl.BlockSpec((tm, d), lambda i: (i, 0)),
        out_shape=jax.ShapeDtypeStruct((t, d), F32),
        compiler_params=_cparams("parallel"),
        name="mlp",
    )(x, nw, mod, mod, mod, w_up, w_down, fw)


def _shift_rows(cur, tail, k):
    rolled = pltpu.roll(cur, k, axis=0)
    row = lax.broadcasted_iota(jnp.int32, tail.shape, 0)
    top = jnp.where(row < k, pltpu.roll(tail, k, axis=0), rolled[:SUBLANE])
    return jnp.concatenate([top, rolled[SUBLANE:]], axis=0)


def _causal_conv(cur, tail, w):
    kk = w.shape[0]
    out = cur * w[kk - 1:kk, :]
    for s in range(1, kk):
        out = out + _shift_rows(cur, tail, s) * w[kk - 1 - s:kk - s, :]
    return out

def _expand_heads(m, h0, n_cols):
    rows = m.shape[0]
    lane = lax.broadcasted_iota(jnp.int32, (rows, LANE), 1)
    cols = []
    for c in range(n_cols):
        lo = jnp.broadcast_to(m[:, h0 + 2 * c:h0 + 2 * c + 1], (rows, LANE))
        hi = jnp.broadcast_to(m[:, h0 + 2 * c + 1:h0 + 2 * c + 2], (rows, LANE))
        cols.append(jnp.where(lane < SSD_HEAD_DIM, lo, hi))
    return jnp.concatenate(cols, axis=1)


def _ssd_prompt_kernel(scb_ref, scc_ref, scx_ref, z0_ref, z1_ref, x0_ref, x1_ref, bc_ref, dt_ref,
                       scw_ref, cw_ref, cb_ref, dtb_ref, alog_ref, dexp_ref, nw_ref,
                       y_ref, scst_ref, cvst_ref, ssm_ref,
                       utail, xtail, s_scr):
    c = pl.program_id(1)
    q = scb_ref.shape[0]
    dsc = scb_ref.shape[1]
    gw = 2 * x0_ref.shape[1] // SSD_GROUPS
    hpg = gw // SSD_HEAD_DIM

    @pl.when(c == 0)
    def _():
        utail[...] = jnp.zeros(utail.shape, F32)
        xtail[...] = jnp.zeros(xtail.shape, F32)
        s_scr[...] = jnp.zeros(s_scr.shape, F32)

    u = scc_ref[...] * scx_ref[...]
    y_sc = scb_ref[...] * _causal_conv(u, utail[...], scw_ref[...])
    utail[...] = u[q - SUBLANE:, :]
    y_ref[:, :dsc] = y_sc.astype(BF16)

    bc = bc_ref[...]
    n_bc = bc.shape[1]
    o_bc = 2 * x0_ref.shape[1]
    bc_cols = slice(o_bc, o_bc + n_bc)
    bc_c = _silu(_causal_conv(bc, xtail[:, bc_cols], cw_ref[:, bc_cols]) + cb_ref[:, bc_cols])
    xtail[:, bc_cols] = bc[q - SUBLANE:, :]
    half = n_bc // 2

    dt = _softplus(dt_ref[...] + dtb_ref[...])
    a = dt * (-jnp.exp(alog_ref[...]))
    ri = lax.broadcasted_iota(jnp.int32, (q, q), 0)
    ci = lax.broadcasted_iota(jnp.int32, (q, q), 1)
    tril = ri >= ci
    a_cs = jnp.dot(tril.astype(F32), a, precision=HIGHEST, preferred_element_type=F32)
    a_cs_t = a_cs.T
    quad = 4 * SSD_HEAD_DIM
    lane_head = lax.broadcasted_iota(jnp.int32, (q, quad), 1) // SSD_HEAD_DIM

    for g in range(SSD_GROUPS):
        x_ref = x0_ref if g < SSD_GROUPS // 2 else x1_ref
        z_ref = z0_ref if g < SSD_GROUPS // 2 else z1_ref
        lo = (g % (SSD_GROUPS // 2)) * gw
        go = g * gw
        xg = x_ref[:, lo:lo + gw]
        xs = _silu(_causal_conv(xg, xtail[:, go:go + gw], cw_ref[:, go:go + gw]) + cb_ref[:, go:go + gw])
        xtail[:, go:go + gw] = xg[q - SUBLANE:, :]

        n_cols = gw // LANE
        dt_x = _expand_heads(dt, g * hpg, n_cols)
        a_x = _expand_heads(a_cs, g * hpg, n_cols)
        ea = jnp.exp(a_x)
        dte = jnp.exp(a_x[q - 1:q, :] - a_x)
        xdt = xs * dt_x
        xdt_b = xdt.astype(BF16)

        bg = bc_c[:, g * SSD_STATE:(g + 1) * SSD_STATE]
        cg_b = bc_c[:, half + g * SSD_STATE:half + (g + 1) * SSD_STATE].astype(BF16)
        cb_mat = lax.dot_general(cg_b, bg.astype(BF16), NT_DIMS, preferred_element_type=F32)
        s_g = s_scr[:, go:go + gw]
        y_off = jnp.dot(cg_b, s_g.astype(BF16), preferred_element_type=F32) * ea

        y_quads = []
        for qd in range(gw // quad):
            xq = xdt[:, qd * quad:(qd + 1) * quad]
            acc = None
            for i in range(4):
                h = g * hpg + qd * 4 + i
                seg = a_cs[:, h:h + 1] - a_cs_t[h:h + 1, :]
                dec = jnp.exp(jnp.where(tril, seg, -jnp.inf))
                w_h = (cb_mat * dec).astype(BF16)
                xm = jnp.where(lane_head == i, xq, 0.0).astype(BF16)
                d = jnp.dot(w_h, xm, preferred_element_type=F32)
                acc = d if acc is None else acc + d
            y_quads.append(acc)
        y = jnp.concatenate(y_quads, axis=1) + y_off + dexp_ref[:, go:go + gw] * xs

        zg = z_ref[:, lo:lo + gw]
        y = y * _silu(zg)
        ms = jnp.mean(y * y, axis=-1, keepdims=True)
        y = y * lax.rsqrt(ms + EPS) * nw_ref[:, go:go + gw]
        y_ref[:, dsc + go:dsc + go + gw] = y.astype(BF16)

        s_new = s_g * ea[q - 1:q, :] + jnp.dot(bg.T.astype(BF16), (xdt * dte).astype(BF16),
                                               preferred_element_type=F32)
        s_scr[:, go:go + gw] = s_new

    @pl.when(c == pl.num_programs(1) - 1)
    def _():
        scst_ref[0] = utail[...]
        cvst_ref[0] = xtail[...]
        ssm_ref[0] = s_scr[...].T


def _ssd_prompt(proj, dt_raw, scw, cw, cb, dtb, alog, dexp, nw, bp, seq):
    q = min(SSD_CHUNK, seq)
    nc = seq // q
    dsc = scw.shape[1]
    d_inner = dexp.shape[1]
    conv_dim = cw.shape[1]
    bw = dsc

    def col(kb):
        return pl.BlockSpec((q, bw), lambda b, c: (b * nc + c, kb))

    n_z = d_inner // bw
    specs = [col(0), col(1), col(2)]
    specs += [col(3 + i) for i in range(n_z)]
    specs += [col(3 + n_z + i) for i in range(n_z)]
    specs += [col(3 + 2 * n_z)]
    specs += [pl.BlockSpec((q, LANE), lambda b, c: (b * nc + c, 0))]
    for arr in (scw, cw, cb, dtb, alog, dexp, nw):
        specs.append(pl.BlockSpec(arr.shape, lambda b, c: (0, 0)))
    return pl.pallas_call(
        _ssd_prompt_kernel,
        grid=(bp, nc),
        in_specs=specs,
        out_specs=[pl.BlockSpec((q, dsc + d_inner), lambda b, c: (b * nc + c, 0)),
                   pl.BlockSpec((1, SUBLANE, dsc), lambda b, c: (b, 0, 0)),
                   pl.BlockSpec((1, SUBLANE, conv_dim), lambda b, c: (b, 0, 0)),
                   pl.BlockSpec((1, d_inner, SSD_STATE), lambda b, c: (b, 0, 0))],
        out_shape=[jax.ShapeDtypeStruct((bp * seq, dsc + d_inner), BF16),
                   jax.ShapeDtypeStruct((bp, SUBLANE, dsc), F32),
                   jax.ShapeDtypeStruct((bp, SUBLANE, conv_dim), F32),
                   jax.ShapeDtypeStruct((bp, d_inner, SSD_STATE), F32)],
        scratch_shapes=[pltpu.VMEM((SUBLANE, dsc), F32),
                        pltpu.VMEM((SUBLANE, conv_dim), F32),
                        pltpu.VMEM((SSD_STATE, d_inner), F32)],
        compiler_params=_cparams("parallel", "arbitrary"),
        name="ssd_prompt",
    )(*([proj] * (4 + 2 * n_z)), dt_raw, scw, cw, cb, dtb, alog, dexp, nw)


def _even_sample_prep_kernel(proj_ref, dt_ref, scst_ref, cvst_ref, scw_ref, cw_ref, cb_ref, dtb_ref, alog_ref,
                             ysc_ref, scn_ref, cvn_ref, xs_ref, b_ref, c_ref, xdt_t_ref, dec_ref):
    dsc = scw_ref.shape[1]
    conv_dim = cw_ref.shape[1]
    d_inner = xs_ref.shape[1]
    n_heads = d_inner // SSD_HEAD_DIM
    scb = proj_ref[:, :dsc]
    u = proj_ref[:, dsc:2 * dsc] * proj_ref[:, 2 * dsc:3 * dsc]
    xbc = proj_ref[:, 3 * dsc + d_inner:3 * dsc + d_inner + conv_dim]
    b0 = scst_ref[:, :dsc]
    b1 = scst_ref[:, dsc:]
    ysc_ref[...] = scb * (b0 * scw_ref[0:1, :] + b1 * scw_ref[1:2, :] + u * scw_ref[2:3, :])
    scn_ref[:, :dsc] = b1
    scn_ref[:, dsc:] = u

    c0 = cvst_ref[:, :conv_dim]
    c1 = cvst_ref[:, conv_dim:2 * conv_dim]
    c2 = cvst_ref[:, 2 * conv_dim:]
    act = _silu(c0 * cw_ref[0:1, :] + c1 * cw_ref[1:2, :] + c2 * cw_ref[2:3, :] + xbc * cw_ref[3:4, :] + cb_ref[...])
    cvn_ref[:, :conv_dim] = c1
    cvn_ref[:, conv_dim:2 * conv_dim] = c2
    cvn_ref[:, 2 * conv_dim:] = xbc
    xs = act[:, :d_inner]
    n_bc = (conv_dim - d_inner) // 2
    xs_ref[...] = xs
    b_ref[...] = act[:, d_inner:d_inner + n_bc]
    c_ref[...] = act[:, d_inner + n_bc:]

    dt = _softplus(dt_ref[...] + dtb_ref[...])
    dec_ref[...] = jnp.exp(dt * (-jnp.exp(alog_ref[...])))
    xdt_t_ref[...] = (xs * _expand_heads(dt, 0, n_heads // 2)).T


def _even_sample_prep(proj, dt_raw, scst, cvst, scw, cw, cb, dtb, alog, d_inner):
    bs = proj.shape[0]
    dsc = scw.shape[1]
    conv_dim = cw.shape[1]
    n_bc = (conv_dim - d_inner) // 2
    shapes = [(bs, dsc), (bs, 2 * dsc), (bs, 3 * conv_dim), (bs, d_inner), (bs, n_bc), (bs, n_bc),
              (d_inner, bs), (bs, LANE)]
    return pl.pallas_call(
        _even_sample_prep_kernel,
        out_shape=[jax.ShapeDtypeStruct(s, F32) for s in shapes],
        compiler_params=pltpu.CompilerParams(vmem_limit_bytes=VMEM_LIMIT),
        name="even_sample_prep",
    )(proj, dt_raw, scst, cvst, scw, cw, cb, dtb, alog)


def _ssm_sample_kernel(dec_ref, h0_ref, xt_ref, b_ref, c_ref, hn_ref, yt_ref):
    h = pl.program_id(0)
    bs = h0_ref.shape[0]
    p = h0_ref.shape[2]
    n = h0_ref.shape[3]
    lane = lax.broadcasted_iota(jnp.int32, (p, bs), 1)
    xt = xt_ref[...]
    ones = jnp.ones((n, bs), BF16)
    yt = jnp.zeros((p, bs), F32)
    for b in range(bs):
        xcol = jnp.broadcast_to(xt[:, b:b + 1], (p, n))
        hn = h0_ref[b, 0] * dec_ref[b, h] + xcol * b_ref[b:b + 1, :]
        hn_ref[b, 0] = hn
        ysum = jnp.zeros((p, bs), F32)
        rest = hn * c_ref[b:b + 1, :]
        for _ in range(2):
            piece = rest.astype(BF16)
            ysum = ysum + jnp.dot(piece, ones, preferred_element_type=F32)
            rest = rest - piece.astype(F32)
        yt = jnp.where(lane == b, ysum, yt)
    yt_ref[...] = yt


def _ssm_sample(h0, xdt_t, dec, bm, cm):
    bs, n_heads, p, n = h0.shape
    hpg = n_heads // SSD_GROUPS
    return pl.pallas_call(
        _ssm_sample_kernel,
        grid=(n_heads,),
        in_specs=[pl.BlockSpec(memory_space=pltpu.SMEM),
                  pl.BlockSpec((bs, 1, p, n), lambda h: (0, h, 0, 0)),
                  pl.BlockSpec((p, bs), lambda h: (h, 0)),
                  pl.BlockSpec((bs, n), lambda h: (0, h // hpg)),
                  pl.BlockSpec((bs, n), lambda h: (0, h // hpg))],
        out_specs=[pl.BlockSpec((bs, 1, p, n), lambda h: (0, h, 0, 0)),
                   pl.BlockSpec((p, bs), lambda h: (h, 0))],
        out_shape=[jax.ShapeDtypeStruct(h0.shape, F32), jax.ShapeDtypeStruct((n_heads * p, bs), F32)],
        compiler_params=_cparams("parallel"),
        name="ssm_sample",
    )(dec, h0, xdt_t, bm, cm)


def _even_out_sample_kernel(yt_ref, xs_ref, proj_ref, ysc_ref, dexp_ref, nw_ref, w_ref, x_ref, g_ref, o_ref):
    d_inner = xs_ref.shape[1]
    dsc = ysc_ref.shape[1]
    gw = d_inner // SSD_GROUPS
    y = yt_ref[...].T + dexp_ref[...] * xs_ref[...]
    y = y * _silu(proj_ref[:, 3 * dsc:3 * dsc + d_inner])
    parts = []
    for g in range(SSD_GROUPS):
        yg = y[:, g * gw:(g + 1) * gw]
        ms = jnp.mean(yg * yg, axis=-1, keepdims=True)
        parts.append(yg * lax.rsqrt(ms + EPS))
    y_ssd = jnp.concatenate(parts, axis=1) * nw_ref[...]
    cat = jnp.concatenate([ysc_ref[...], y_ssd], axis=1).astype(BF16)
    o_ref[...] = x_ref[...] + g_ref[0] * jnp.dot(cat, w_ref[...], preferred_element_type=F32)


def _even_out_sample(y_t, xs, proj, ysc, dexp, nw, w_out, x, mod):
    bs, d = x.shape
    full = lambda a: pl.BlockSpec(a.shape, lambda i: (0,) * a.ndim)
    return pl.pallas_call(
        _even_out_sample_kernel,
        grid=(1,),
        in_specs=[full(y_t), full(xs), full(proj), full(ysc), full(dexp), full(nw), full(w_out), full(x),
                  pl.BlockSpec((1, bs, d), lambda i: (0, 0, 2))],
        out_specs=pl.BlockSpec((bs, d), lambda i: (0, 0)),
        out_shape=jax.ShapeDtypeStruct((bs, d), F32),
        compiler_params=_cparams("arbitrary"),
        name="even_out_sample",
    )(y_t, xs, proj, ysc, dexp, nw, w_out, x, mod)


def _rope_table_kernel(ca_ref, cb_ref, cc_ref, *, pos0, stride):
    rows = ca_ref.shape[0]
    row = lax.broadcasted_iota(jnp.int32, (rows, LANE), 0) + pl.program_id(0) * rows
    d = lax.broadcasted_iota(jnp.int32, (rows, LANE), 1) % ATT_HEAD_DIM
    half = ROT_DIM // 2
    pos = (pos0 + row * stride).astype(F32)
    inv = jnp.power(jnp.full((rows, LANE), ROPE_THETA, F32), -((d % half).astype(F32) * 2.0 / ROT_DIM))
    ang = pos * inv
    cs = jnp.cos(ang)
    sn = jnp.sin(ang)
    ca_ref[...] = jnp.where(d < ROT_DIM, cs, 1.0)
    cb_ref[...] = jnp.where(d < half, -sn, 0.0)
    cc_ref[...] = jnp.where(d < half, 0.0, jnp.where(d < ROT_DIM, sn, 0.0))


def _rope_tables(rows, pos0, stride):
    tr = min(rows, 512)
    spec = pl.BlockSpec((tr, LANE), lambda i: (i, 0))
    return pl.pallas_call(
        functools.partial(_rope_table_kernel, pos0=pos0, stride=stride),
        grid=(rows // tr,),
        out_specs=[spec, spec, spec],
        out_shape=[jax.ShapeDtypeStruct((rows, LANE), F32)] * 3,
        compiler_params=_cparams("parallel"),
        name="rope_tables",
    )()


def _rope(t, ca, cb, cc):
    cols = []
    for c in range(t.shape[1] // LANE):
        s = t[:, c * LANE:(c + 1) * LANE]
        cols.append(s * ca + pltpu.roll(s, LANE - ROT_DIM // 2, axis=1) * cb + pltpu.roll(s, ROT_DIM // 2, axis=1) * cc)
    return jnp.concatenate(cols, axis=1)


def _sel_keys(k, j):
    lane = lax.broadcasted_iota(jnp.int32, (k.shape[0], LANE), 1)
    onehot = jnp.where(lane == j, 1.0, 0.0)
    in_sel = lane < SEL_LANES
    k_lo = jnp.concatenate([jnp.where(in_sel, onehot, k[:, :LANE]), k[:, LANE:]], axis=1)
    k_hi = jnp.concatenate([k[:, :LANE], jnp.where(in_sel, onehot, k[:, LANE:])], axis=1)
    return k_lo.astype(BF16), k_hi.astype(BF16)


def _qkv_kernel(x_ref, nw_ref, sh_ref, sc_ref, w_ref, ca_ref, cb_ref, cc_ref, q_ref, k_ref, v_ref, *blk_refs,
                blocks_per_seq):
    nq = q_ref.shape[1]
    nk = (w_ref.shape[1] - nq) // 2
    hb = _norm_mod(x_ref[...], nw_ref[...], sh_ref[0], sc_ref[0]).astype(BF16)
    qkv = jnp.dot(hb, w_ref[...], preferred_element_type=F32)
    ca, cb, cc = ca_ref[...], cb_ref[...], cc_ref[...]
    q_ref[...] = _rope(qkv[:, :nq], ca, cb, cc)
    k = _rope(qkv[:, nq:nq + nk], ca, cb, cc)
    v = qkv[:, nq + nk:]
    if blocks_per_seq:
        km_ref, klo_ref, khi_ref, vt_ref = blk_refs
        vt = v.T
        k_ref[0] = k.T
        v_ref[0] = vt
        km_ref[0] = jnp.mean(k, axis=0, keepdims=True)
        klo_ref[...], khi_ref[...] = _sel_keys(k, pl.program_id(0) % blocks_per_seq)
        hd = ATT_HEAD_DIM
        ones_row = jnp.where(lax.broadcasted_iota(jnp.int32, (BF16_SUBLANE, vt.shape[1]), 0) == 0, 1.0, 0.0)
        vt_ref[0] = jnp.concatenate([piece for h in range(nk // hd) for piece in (vt[h * hd:(h + 1) * hd], ones_row)],
                                    axis=0).astype(BF16)
    else:
        k_ref[...] = k
        v_ref[...] = v


def _qkv(x, nw, mod, w, tabs, rows_per_group, tm, per_block):
    t, d = x.shape
    nq = ATT_HEADS * ATT_HEAD_DIM
    nk = ATT_KV_HEADS * ATT_HEAD_DIM
    tab_tiles = tabs[0].shape[0] // tm
    tab = pl.BlockSpec((tm, LANE), lambda i: (i % tab_tiles, 0))
    row = lambda n: pl.BlockSpec((tm, n), lambda i: (i, 0))
    if per_block:
        bps = rows_per_group // tm
        kv_t = pl.BlockSpec((1, nk, tm), lambda i: (i // bps, 0, i % bps))
        v_rows = ATT_KV_HEADS * (ATT_HEAD_DIM + BF16_SUBLANE)
        out_specs = [row(nq), kv_t, kv_t, pl.BlockSpec((1, 1, nk), lambda i: (i, 0, 0)), row(nk), row(nk),
                     pl.BlockSpec((1, v_rows, tm), lambda i: (i, 0, 0))]
        out_shape = [jax.ShapeDtypeStruct((t, nq), F32),
                     jax.ShapeDtypeStruct((t // rows_per_group, nk, rows_per_group), F32),
                     jax.ShapeDtypeStruct((t // rows_per_group, nk, rows_per_group), F32),
                     jax.ShapeDtypeStruct((t // tm, 1, nk), F32), jax.ShapeDtypeStruct((t, nk), BF16),
                     jax.ShapeDtypeStruct((t, nk), BF16), jax.ShapeDtypeStruct((t // tm, v_rows, tm), BF16)]
    else:
        out_specs = [row(nq), row(nk), row(nk)]
        out_shape = [jax.ShapeDtypeStruct((t, nq), F32), jax.ShapeDtypeStruct((t, nk), F32),
                     jax.ShapeDtypeStruct((t, nk), F32)]
    return pl.pallas_call(
        functools.partial(_qkv_kernel, blocks_per_seq=rows_per_group // tm if per_block else 0),
        grid=(t // tm,),
        in_specs=[row(d), pl.BlockSpec((1, d), lambda i: (0, 0)),
                  _mod_spec(mod, 0, rows_per_group, tm), _mod_spec(mod, 1, rows_per_group, tm),
                  _resident(w.shape), tab, tab, tab],
        out_specs=out_specs,
        out_shape=out_shape,
        compiler_params=_cparams("parallel"),
        name="qkv_rope",
    )(x, nw, mod, mod, w, *tabs)


def _select_bias(scores, n_past, n_rounds):
    lane = lax.broadcasted_iota(jnp.int32, scores.shape, 1)
    lanef = lane.astype(F32)
    s = jnp.where(lane < n_past, scores, -jnp.inf)
    bias = jnp.full(scores.shape, NEG_BIG, F32)
    for r in range(n_rounds):
        mx = jnp.max(s, axis=1, keepdims=True)
        idx = jnp.min(jnp.where(s == mx, lanef, float(LANE)), axis=1, keepdims=True)
        bias = jnp.where(lanef == jnp.where(n_past > r, idx, -1.0), 0.0, bias)
        s = jnp.where(lanef == idx, -jnp.inf, s)
    return bias


def _select_bias_t(scores, n_past, n_rounds):
    row = lax.broadcasted_iota(jnp.int32, scores.shape, 0)
    rowf = row.astype(F32)
    s = jnp.where(row < n_past, scores, -jnp.inf)
    bias = jnp.full(scores.shape, NEG_BIG, F32)
    for r in range(n_rounds):
        mx = jnp.max(s, axis=0, keepdims=True)
        idx = jnp.min(jnp.where(s == mx, rowf, float(SEL_LANES)), axis=0, keepdims=True)
        bias = jnp.where(rowf == jnp.where(n_past > r, idx, -1.0), 0.0, bias)
        s = jnp.where(rowf == idx, -jnp.inf, s)
    return jnp.where(row == n_past, 0.0, bias)


def _moba_prompt_kernel(q_ref, klo_ref, khi_ref, vt_ref, km_ref, o_ref, qa_scr, m_scr, over_scr, acc_scr):
    qi = pl.program_id(1)
    blk = q_ref.shape[0]
    kvw = klo_ref.shape[1]
    hd = ATT_HEAD_DIM
    n_q = ATT_GROUP * blk
    kmean = km_ref[0]
    km_hi = kmean.astype(BF16)
    km_lo = (kmean - km_hi.astype(F32)).astype(BF16)
    n_rounds = min(MOBA_TOPK, klo_ref.shape[0] // blk)

    q_t = q_ref[...].T
    for h in range(ATT_KV_HEADS):
        qh_b = jnp.concatenate([q_t[g * kvw + h * hd:g * kvw + (h + 1) * hd, :] for g in range(ATT_GROUP)],
                               axis=1).astype(BF16)
        scores_t = (jnp.dot(km_hi[:, h * hd:(h + 1) * hd], qh_b, preferred_element_type=F32)
                    + jnp.dot(km_lo[:, h * hd:(h + 1) * hd], qh_b, preferred_element_type=F32))
        bias_t = _select_bias_t(scores_t, qi, n_rounds)
        sel_row = LANE if h == 0 else 0
        qa_scr[h] = jnp.zeros((kvw, n_q), BF16)
        qa_scr[h, h * hd:(h + 1) * hd, :] = qh_b
        qa_scr[h, sel_row:sel_row + SEL_LANES, :] = bias_t.astype(BF16)

    v_rows = vt_ref.shape[1] // ATT_KV_HEADS
    key_row = lax.broadcasted_iota(jnp.int32, (blk, n_q), 0)
    q_pos = lax.broadcasted_iota(jnp.int32, (blk, n_q), 1) % blk

    def scores(h, j):
        off = pl.multiple_of(j * blk, blk)
        ka = (khi_ref if h == 0 else klo_ref)[pl.ds(off, blk), :]
        return jnp.dot(ka, qa_scr[h], preferred_element_type=F32)

    def softmax(h, s, first):
        if first:
            s = jnp.where(key_row <= q_pos, s, NEG_BIG)
        mx = jnp.max(s, axis=0, keepdims=True)
        if first:
            m_new, alpha = mx, None
        else:
            m_old = m_scr[h, 0:1, :]
            m_new = jnp.maximum(m_old, mx)
            alpha = jnp.exp2(m_old - m_new)
        m_scr[h] = jnp.broadcast_to(m_new, (SUBLANE, n_q))
        return jnp.exp2((s - m_new).astype(BF16)), alpha

    def values(h, j, p, alpha):
        vt_h = vt_ref[j, h * v_rows:(h + 1) * v_rows, :]
        pv = jnp.dot(vt_h, p, preferred_element_type=F32)
        acc_scr[h] = pv if alpha is None else acc_scr[h] * alpha + pv

    def block_step(j, first):
        s_next = scores(0, j)
        for h in range(ATT_KV_HEADS):
            s = s_next
            if h + 1 < ATT_KV_HEADS:
                s_next = scores(h + 1, j)
            p, alpha = softmax(h, s, first)
            values(h, j, p, alpha)

    def finalize():
        for g in range(ATT_GROUP):
            cols = slice(g * blk, (g + 1) * blk)
            o_t = jnp.concatenate([acc_scr[h, :hd, cols] / acc_scr[h, hd:hd + 1, cols]
                                   for h in range(ATT_KV_HEADS)], axis=0)
            o_ref[:, g * kvw:(g + 1) * kvw] = o_t.T.astype(BF16)

    block_step(qi, True)

    for h in range(ATT_KV_HEADS):
        over_scr[h] = jnp.full((SUBLANE, n_q), NEG_BIG, F32)

    def fast_body(j, carry):
        s_next = scores(0, j)
        for h in range(ATT_KV_HEADS):
            s = s_next
            if h + 1 < ATT_KV_HEADS:
                s_next = scores(h + 1, j)
            d = s - m_scr[h, 0:1, :]
            over_scr[h] = jnp.maximum(over_scr[h], jnp.broadcast_to(jnp.max(d, axis=0, keepdims=True), (SUBLANE, n_q)))
            p = jnp.exp2(jnp.minimum(d, EXP2_CLAMP)).astype(BF16)
            vt_h = vt_ref[j, h * v_rows:(h + 1) * v_rows, :]
            acc_scr[h] = acc_scr[h] + jnp.dot(vt_h, p, preferred_element_type=F32)
        return carry

    lax.fori_loop(0, qi, fast_body, 0)
    finalize()

    @pl.when(jnp.max(over_scr[...]) > EXP2_CLAMP)
    def _():
        block_step(qi, True)

        def body(j, carry):
            block_step(j, False)
            return carry

        lax.fori_loop(0, qi, body, 0)
        finalize()


def _moba_prompt(q, k_lo, k_hi, v_t, kmean, bp, seq):
    blk = MOBA_BLOCK
    nq = seq // blk
    kvw = k_lo.shape[1]
    n_q = ATT_GROUP * blk
    v_rows = v_t.shape[1]
    return pl.pallas_call(
        _moba_prompt_kernel,
        grid=(bp, nq),
        in_specs=[pl.BlockSpec((blk, q.shape[1]), lambda b, i: (b * nq + i, 0)),
                  pl.BlockSpec((seq, kvw), lambda b, i: (b, 0)),
                  pl.BlockSpec((seq, kvw), lambda b, i: (b, 0)),
                  pl.BlockSpec((nq, v_rows, blk), lambda b, i: (b, 0, 0)),
                  pl.BlockSpec((1, SEL_LANES, kvw), lambda b, i: (b, 0, 0))],
        out_specs=pl.BlockSpec((blk, q.shape[1]), lambda b, i: (b * nq + i, 0)),
        out_shape=jax.ShapeDtypeStruct(q.shape, BF16),
        scratch_shapes=[pltpu.VMEM((ATT_KV_HEADS, kvw, n_q), BF16),
                        pltpu.VMEM((ATT_KV_HEADS, SUBLANE, n_q), F32),
                        pltpu.VMEM((ATT_KV_HEADS, SUBLANE, n_q), F32),
                        pltpu.VMEM((ATT_KV_HEADS, v_rows // ATT_KV_HEADS, n_q), F32)],
        compiler_params=_cparams("parallel", "arbitrary"),
        name="moba_prompt",
    )(q, k_lo, k_hi, v_t, kmean)


def _moba_sample_kernel(pt_ref, q_ref, kn_ref, vn_ref, avg_ref, *rest, n_pages):
    k_pages = rest[:n_pages]
    v_pages = rest[n_pages:2 * n_pages]
    o_ref = rest[2 * n_pages]
    page = k_pages[0].shape[3]
    n_past = n_pages * page // MOBA_BLOCK
    q_rows = q_ref[0]
    kn_rows = kn_ref[0]
    vn_rows = vn_ref[0]
    n_rows = q_rows.shape[0]
    q_b = q_rows.astype(BF16)

    heads = range(ATT_KV_HEADS)
    rows = [slice(h * SUBLANE, (h + 1) * SUBLANE) for h in heads]
    kt_bs = [jnp.concatenate([r[0, h] for r in k_pages], axis=1).astype(BF16) for h in heads]
    logits = jnp.concatenate([jnp.dot(q_b[rows[h]], kt_bs[h], preferred_element_type=F32) for h in heads], axis=0)
    kmts = [jnp.dot(kt_bs[h], avg_ref[...], preferred_element_type=F32) for h in heads]
    vts =[jnp.concatenate([r[0, h] for r in v_pages], axis=1).astype(BF16) for h in heads]

    blocks = range(n_past)
    keys = [slice(j * MOBA_BLOCK, (j + 1) * MOBA_BLOCK) for j in blocks]
    mjs = [jnp.max(logits[:, keys[j]], axis=1, keepdims=True) for j in blocks]
    km_his = [km.astype(BF16) for km in kmts]
    km_los = [(kmts[h] - km_his[h].astype(F32)).astype(BF16) for h in heads]
    scores = jnp.concatenate([jnp.dot(q_b[rows[h]], km_his[h], preferred_element_type=F32)
                              + jnp.dot(q_b[rows[h]], km_los[h], preferred_element_type=F32) for h in heads], axis=0)
    pjs = [jnp.exp2(logits[:, keys[j]] - mjs[j]) for j in blocks]
    sjs = [jnp.sum(pjs[j], axis=1, keepdims=True) for j in blocks]
    pj_bs = [pjs[j].astype(BF16) for j in blocks]
    blk_out = [jnp.concatenate([lax.dot_general(pj_bs[j][rows[h]], vts[h][:, keys[j]], NT_DIMS,
                                                preferred_element_type=F32) for h in heads], axis=0)
               for j in blocks]
    bias = _select_bias(scores, n_past, min(MOBA_TOPK, n_past + 1))
    lane = lax.broadcasted_iota(jnp.int32, (n_rows, LANE), 1)
    blk_max = jnp.full((n_rows, LANE), NEG_BIG, F32)
    blk_sum = jnp.zeros((n_rows, LANE), F32)
    for j in blocks:
        blk_max = jnp.where(lane == j, mjs[j], blk_max)
        blk_sum = jnp.where(lane == j, sjs[j], blk_sum)
    l_new = jnp.sum(q_rows * kn_rows, axis=1, keepdims=True)
    chosen = blk_max + bias
    m = jnp.maximum(jnp.max(chosen, axis=1, keepdims=True), l_new)
    w = jnp.exp2(chosen - m)
    pn = jnp.exp2(l_new - m)
    denom = jnp.sum(w * blk_sum, axis=1, keepdims=True) + pn
    terms = [w[:, j:j + 1] * blk_out[j] for j in blocks] + [pn * vn_rows]
    while len(terms) > 1:
        terms = [terms[i] + terms[i + 1] for i in range(0, len(terms) - 1, 2)] + terms[len(terms) & ~1:]
    o_ref[0] = terms[0] / denom


def _moba_sample(q, kn, vn, cache_kt, cache_vt, page_table):
    bs = q.shape[0]
    n_pages = page_table.shape[1]
    _, n_kv, hd, page = cache_kt.shape
    past = n_pages * page
    n_rows = n_kv * SUBLANE
    assert ATT_GROUP <= SUBLANE

    def head_rows(t):
        return jnp.pad(t, ((0, 0), (0, 0), (0, SUBLANE - t.shape[2]), (0, 0))).reshape(bs, n_rows, hd)

    q_rows = head_rows(q.reshape(bs, ATT_GROUP, n_kv, hd).transpose(0, 2, 1, 3))
    kn_rows = jnp.broadcast_to(kn.reshape(bs, n_kv, 1, hd), (bs, n_kv, SUBLANE, hd)).reshape(bs, n_rows, hd)
    vn_rows = jnp.broadcast_to(vn.reshape(bs, n_kv, 1, hd), (bs, n_kv, SUBLANE, hd)).reshape(bs, n_rows, hd)
    one = pl.BlockSpec((1, n_rows, hd), lambda b, pt: (b, 0, 0))
    avg = jnp.where(jnp.arange(past)[:, None] // MOBA_BLOCK == jnp.arange(LANE)[None, :], 1.0 / MOBA_BLOCK, 0.0).astype(BF16)

    def page_spec(p):
        return pl.BlockSpec((1, n_kv, hd, page), lambda b, pt: (pt[b, p], 0, 0, 0))

    grid_spec = pltpu.PrefetchScalarGridSpec(
        num_scalar_prefetch=1,
        grid=(bs,),
        in_specs=[one, one, one, pl.BlockSpec((past, LANE), lambda b, pt: (0, 0))]
        + [page_spec(p) for p in range(n_pages)] * 2,
        out_specs=one,
    )
    o_rows = pl.pallas_call(
        functools.partial(_moba_sample_kernel, n_pages=n_pages),
        grid_spec=grid_spec,
        out_shape=jax.ShapeDtypeStruct((bs, n_rows, hd), F32),
        compiler_params=_cparams("parallel"),
        name="moba_sample",
    )(page_table, q_rows, kn_rows, vn_rows, avg, *([cache_kt] * n_pages), *([cache_vt] * n_pages))
    return o_rows.reshape(bs, n_kv, SUBLANE, hd)[:, :, :ATT_GROUP].transpose(0, 2, 1, 3).reshape(bs, ATT_GROUP * n_kv * hd)


def _pad_lanes(v, fill=0.0):
    return jnp.pad(v.reshape(1, -1), ((0, 0), (0, LANE - v.shape[0])), constant_values=fill)


def kernel(x_prompt, x_sample, state_sc_conv, state_ssd_conv, state_ssm, cache_k, cache_v, page_table,
           c_prompt, c_sample, norm_mix_w, norm_mlp_w, norm_final_w, w_ada, b_ada, w_in_e, sc_conv_w,
           ssd_conv_w, ssd_conv_b, ssd_dt_bias, ssd_a_log, ssd_d, ssd_norm_w, w_out_e, w_qkv, w_o, w_up, w_down):
    bp, seq, d = x_prompt.shape
    bs, dec_seq, _ = x_sample.shape
    assert dec_seq == 1 and seq % MOBA_BLOCK == 0 and seq % SSD_CHUNK == 0 and seq // MOBA_BLOCK <= SEL_LANES
    n_pages, page = page_table.shape[1], cache_k.shape[2]
    past_len = n_pages * page
    assert past_len % MOBA_BLOCK == 0 and MOBA_BLOCK % page == 0
    dsc = sc_conv_w.shape[2]
    conv_dim = ssd_conv_w.shape[2]
    n_heads = ssd_d.shape[1]
    d_inner = n_heads * SSD_HEAD_DIM
    main = 3 * dsc + d_inner + conv_dim
    kvw = ATT_KV_HEADS * ATT_HEAD_DIM
    nq = ATT_HEADS * ATT_HEAD_DIM

    xp = x_prompt.reshape(bp * seq, d)
    xs = x_sample.reshape(bs, d)

    rows = -(-(bs + bp) // SUBLANE) * SUBLANE
    c_all = jnp.pad(jnp.concatenate([c_sample, c_prompt], axis=0), ((0, rows - bs - bp), (0, 0)))
    ada = _ada(c_all, w_ada, b_ada)

    w_in = w_in_e[0][:, :main].astype(BF16)
    w_dt = jnp.pad(w_in_e[0][:, main:], ((0, 0), (0, LANE - n_heads))).astype(BF16)
    w_out = w_out_e[0].astype(BF16)
    wq = w_qkv[0][:, :nq].reshape(d, ATT_KV_HEADS, ATT_GROUP, ATT_HEAD_DIM).transpose(0, 2, 1, 3).reshape(d, nq)
    w_qkv_p = jnp.concatenate([wq * (ATT_HEAD_DIM ** -0.5 * math.log2(math.e)), w_qkv[0][:, nq:]], axis=1).astype(BF16)
    w_o_p = w_o[0].reshape(ATT_KV_HEADS, ATT_GROUP, ATT_HEAD_DIM, d).transpose(1, 0, 2, 3).reshape(nq, d).astype(BF16)
    w_up_b = w_up.astype(BF16)
    w_down_b = w_down.astype(BF16)

    scw = sc_conv_w[0]
    cw = ssd_conv_w[0]
    cb = ssd_conv_b[0].reshape(1, conv_dim)
    dtb = _pad_lanes(ssd_dt_bias[0])
    alog = _pad_lanes(ssd_a_log[0])
    dexp = jnp.repeat(ssd_d[0], SSD_HEAD_DIM).reshape(1, d_inner)
    ssd_nw = ssd_norm_w[0].reshape(1, d_inner)

    mod_p = ada[0, bs:bs + bp].reshape(bp, 1, 6 * d)
    mod_s = ada[0, :bs].reshape(1, bs, 6 * d)
    nw_mix = norm_mix_w[0].reshape(1, d)
    nw_mlp = norm_mlp_w[0].reshape(1, d)
    fw = norm_final_w.reshape(1, d)

    proj_p, dt_p = _nm_matmul(xp, nw_mix, mod_p, 0, 1, w_in, w_dt, seq, min(seq, 1024), 2048)
    ycat_p, scst_p, cvst_p, ssm_p = _ssd_prompt(proj_p, dt_p, scw, cw, cb, dtb, alog, dexp, ssd_nw, bp, seq)
    xp = _mm_res(ycat_p, w_out, xp, mod_p, 2, seq, 512)
    xp = _mlp(xp, nw_mlp, mod_p, w_up_b[0], w_down_b[0], fw, seq, 512, False)

    proj_s, dt_s = _nm_matmul(xs, nw_mix, mod_s, 0, 1, w_in, w_dt, bs, bs, 1024)
    ysc_s, scn_s, cvn_s, xs_s, b_s, c_s, xdt_t, dec_s = _even_sample_prep(
        proj_s, dt_s, state_sc_conv[0].reshape(bs, -1), state_ssd_conv[0].reshape(bs, -1),
        scw, cw, cb, dtb, alog, d_inner)
    ssm_s, y_t = _ssm_sample(state_ssm[0], xdt_t, dec_s, b_s, c_s)
    xs = _even_out_sample(y_t, xs_s, proj_s, ysc_s, dexp, ssd_nw, w_out, xs, mod_s)
    xs = _mlp(xs, nw_mlp, mod_s, w_up_b[0], w_down_b[0], fw, bs, bs, False)

    mod_p = ada[1, bs:bs + bp].reshape(bp, 1, 6 * d)
    mod_s = ada[1, :bs].reshape(1, bs, 6 * d)
    nw_mix = norm_mix_w[1].reshape(1, d)
    nw_mlp = norm_mlp_w[1].reshape(1, d)

    tabs_p = _rope_tables(seq, 0, 1)
    q_p, kt_p, vt_p, km_p, klo_p, khi_p, vtb_p = _qkv(xp, nw_mix, mod_p, w_qkv_p, tabs_p, seq, MOBA_BLOCK, True)
    n_blk = seq // MOBA_BLOCK
    kmean_p = jnp.pad(km_p.reshape(bp, n_blk, kvw), ((0, 0), (0, SEL_LANES - n_blk), (0, 0)))
    o_p = _moba_prompt(q_p, klo_p, khi_p, vtb_p, kmean_p, bp, seq)
    xp = _mm_res(o_p, w_o_p, xp, mod_p, 2, seq, 512)
    y_prompt = _mlp(xp, nw_mlp, mod_p, w_up_b[1], w_down_b[1], fw, seq, 512, True)

    tabs_s = _rope_tables(bs, past_len, 0)
    q_s, k_s, v_s = _qkv(xs, nw_mix, mod_s, w_qkv_p, tabs_s, bs, bs, False)
    o_s = _moba_sample(q_s, k_s, v_s, cache_k[0].transpose(0, 2, 3, 1), cache_v[0].transpose(0, 2, 3, 1), page_table)
    xs = _mm_res(o_s.reshape(bs, nq).astype(BF16), w_o_p, xs, mod_s, 2, bs, bs)
    y_sample = _mlp(xs, nw_mlp, mod_s, w_up_b[1], w_down_b[1], fw, bs, bs, True)

    def kv_out(t):
        return t.reshape(bp, ATT_KV_HEADS, ATT_HEAD_DIM, seq).transpose(0, 3, 1, 2)[None]

    return (y_prompt.reshape(bp, seq, d), y_sample.reshape(bs, 1, d),
            scst_p[:, SUBLANE - 2:, :][None], scn_s.reshape(1, bs, 2, dsc),
            cvst_p[:, SUBLANE - 3:, :][None], cvn_s.reshape(1, bs, 3, conv_dim),
            ssm_p.reshape(1, bp, n_heads, SSD_HEAD_DIM, SSD_STATE), ssm_s[None],
            kv_out(kt_p), k_s.reshape(1, bs, 1, ATT_KV_HEADS, ATT_HEAD_DIM),
            kv_out(vt_p), v_s.reshape(1, bs, 1, ATT_KV_HEADS, ATT_HEAD_DIM))
```

```python
import functools
import math

import jax
import jax.numpy as jnp
from jax import lax
from jax.experimental import pallas as pl
from jax.experimental.pallas import tpu as pltpu

F32 = jnp.float32
BF16 = jnp.bfloat16
HIGHEST = lax.Precision.HIGHEST

EPS = 1e-6
LANE = 128
SUBLANE = 8
BF16_SUBLANE = 16
SSD_HEAD_DIM = 64
SSD_GROUPS = 4
SSD_STATE = 128
SSD_CHUNK = 256
ATT_HEADS = 16
ATT_KV_HEADS = 4
ATT_GROUP = ATT_HEADS // ATT_KV_HEADS
ATT_HEAD_DIM = 64
ROT_DIM = ATT_HEAD_DIM // 4
ROPE_THETA = 500000.0
MOBA_BLOCK = 256
MOBA_TOPK = 3
SEL_LANES = 16
NEG_BIG = -1e30
EXP2_CLAMP = 60.0
VMEM_LIMIT = 56 * 1024 * 1024

NT_DIMS = (((1,), (1,)), ((), ()))


def _cparams(*sem):
    return pltpu.CompilerParams(dimension_semantics=sem, vmem_limit_bytes=VMEM_LIMIT)


def _silu(x):
    return x * jax.nn.sigmoid(x)


def _softplus(x):
    return jnp.maximum(x, 0.0) + jnp.log1p(jnp.exp(-jnp.abs(x)))


def _norm_mod(x, nw, shift, scale):
    ms = jnp.mean(x * x, axis=-1, keepdims=True)
    xn = x * lax.rsqrt(ms + EPS) * nw
    return xn * (1.0 + scale) + shift


def _resident(shape):
    nd = len(shape)
    return pl.BlockSpec(shape, lambda *_: (0,) * nd, pipeline_mode=pl.Buffered(1))


def _ada_kernel(c_ref, w_ref, b_ref, o_ref):
    s = _silu(c_ref[...]).astype(BF16)
    o_ref[0] = jnp.dot(s, w_ref[0].astype(BF16), preferred_element_type=F32) + b_ref[0]


def _ada(c_all, w_ada, b_ada):
    rows, d = c_all.shape
    n_layers, _, n6 = w_ada.shape
    tn = 1024
    return pl.pallas_call(
        _ada_kernel,
        grid=(n_layers, n6 // tn),
        in_specs=[pl.BlockSpec((rows, d), lambda l, j: (0, 0)),
                  pl.BlockSpec((1, d, tn), lambda l, j: (l, 0, j)),
                  pl.BlockSpec((1, 1, tn), lambda l, j: (l, 0, j))],
        out_specs=pl.BlockSpec((1, rows, tn), lambda l, j: (l, 0, j)),
        out_shape=jax.ShapeDtypeStruct((n_layers, rows, n6), F32),
        compiler_params=_cparams("parallel", "parallel"),
        name="adaln",
    )(c_all, w_ada, b_ada.reshape(n_layers, 1, n6))


def _mod_spec(mod, k, rows_per_group, tm):
    _, r, n6 = mod.shape
    d = n6 // 6
    tiles_per_group = max(rows_per_group // tm, 1)
    return pl.BlockSpec((1, r, d), lambda i, *_: (i // tiles_per_group, 0, k))


def _nm_matmul_kernel(x_ref, nw_ref, sh_ref, sc_ref, w_ref, w2_ref, o_ref, o2_ref, h_scr):
    @pl.when(pl.program_id(1) == 0)
    def _():
        hb = _norm_mod(x_ref[...], nw_ref[...], sh_ref[0], sc_ref[0]).astype(BF16)
        h_scr[...] = hb
        o2_ref[...] = jnp.dot(hb, w2_ref[...], preferred_element_type=F32)

    o_ref[...] = jnp.dot(h_scr[...], w_ref[...], preferred_element_type=F32)


def _nm_matmul(x, nw, mod, k_shift, k_scale, w, w2, rows_per_group, tm, tn):
    t, d = x.shape
    n = w.shape[1]
    n2 = w2.shape[1]
    return pl.pallas_call(
        _nm_matmul_kernel,
        grid=(t // tm, n // tn),
        in_specs=[pl.BlockSpec((tm, d), lambda i, j: (i, 0)),
                  pl.BlockSpec((1, d), lambda i, j: (0, 0)),
                  _mod_spec(mod, k_shift, rows_per_group, tm),
                  _mod_spec(mod, k_scale, rows_per_group, tm),
                  pl.BlockSpec((d, tn), lambda i, j: (0, j)),
                  pl.BlockSpec((d, n2), lambda i, j: (0, 0))],
        out_specs=[pl.BlockSpec((tm, tn), lambda i, j: (i, j)),
                   pl.BlockSpec((tm, n2), lambda i, j: (i, 0))],
        out_shape=[jax.ShapeDtypeStruct((t, n), F32), jax.ShapeDtypeStruct((t, n2), F32)],
        scratch_shapes=[pltpu.VMEM((tm, d), BF16)],
        compiler_params=_cparams("parallel", "arbitrary"),
        name="in_proj",
    )(x, nw, mod, mod, w, w2)


def _mm_res_kernel(a_ref, w_ref, res_ref, g_ref, o_ref):
    o_ref[...] = res_ref[...] + g_ref[0] * jnp.dot(a_ref[...], w_ref[...], preferred_element_type=F32)


def _mm_res(a, w, res, mod, k_gate, rows_per_group, tm):
    t, kdim = a.shape
    d = w.shape[1]
    return pl.pallas_call(
        _mm_res_kernel,
        grid=(t // tm,),
        in_specs=[pl.BlockSpec((tm, kdim), lambda i: (i, 0)),
                  _resident((kdim, d)),
                  pl.BlockSpec((tm, d), lambda i: (i, 0)),
                  _mod_spec(mod, k_gate, rows_per_group, tm)],
        out_specs=pl.BlockSpec((tm, d), lambda i: (i, 0)),
        out_shape=jax.ShapeDtypeStruct((t, d), F32),
        compiler_params=_cparams("parallel"),
        name="proj_residual",
    )(a, w, res, mod)


def _mlp_kernel(x_ref, nw_ref, sh_ref, sc_ref, g_ref, wu_ref, wd_ref, fw_ref, o_ref, *, final_norm, tf):
    x = x_ref[...]
    hb = _norm_mod(x, nw_ref[...], sh_ref[0], sc_ref[0]).astype(BF16)
    acc = jnp.zeros(x.shape, F32)
    for f in range(wu_ref.shape[1] // tf):
        a = jnp.maximum(jnp.dot(hb, wu_ref[:, f * tf:(f + 1) * tf], preferred_element_type=F32), 0.0)
        acc = acc + jnp.dot((a * a).astype(BF16), wd_ref[f * tf:(f + 1) * tf, :], preferred_element_type=F32)
    y = x + g_ref[0] * acc
    if final_norm:
        ms = jnp.mean(y * y, axis=-1, keepdims=True)
        y = y * lax.rsqrt(ms + EPS) * fw_ref[...]
    o_ref[...] = y


def _mlp(x, nw, mod, w_up, w_down, fw, rows_per_group, tm, final_norm):
    t, d = x.shape
    dff = w_up.shape[1]
    return pl.pallas_call(
        functools.partial(_mlp_kernel, final_norm=final_norm, tf=1024),
        grid=(t // tm,),
        in_specs=[pl.BlockSpec((tm, d), lambda i: (i, 0)),
                  pl.BlockSpec((1, d), lambda i: (0, 0)),
                  _mod_spec(mod, 3, rows_per_group, tm),
                  _mod_spec(mod, 4, rows_per_group, tm),
                  _mod_spec(mod, 5, rows_per_group, tm),
                  _resident((d, dff)),
                  _resident((dff, d)),
                  pl.BlockSpec((1, d), lambda i: (0, 0))],
        out_specs=pl.BlockSpec((tm, d), lambda i: (i, 0)),
        out_shape=jax.ShapeDtypeStruct((t, d), F32),
        compiler_params=_cparams("parallel"),
        name="mlp",
    )(x, nw, mod, mod, mod, w_up, w_down, fw)


def _shift_rows(cur, tail, k):
    rolled = pltpu.roll(cur, k, axis=0)
    row = lax.broadcasted_iota(jnp.int32, tail.shape, 0)
    top = jnp.where(row < k, pltpu.roll(tail, k, axis=0), rolled[:SUBLANE])
    return jnp.concatenate([top, rolled[SUBLANE:]], axis=0)


def _causal_conv(cur, tail, w):
    kk = w.shape[0]
    out = cur * w[kk - 1:kk, :]
    for s in range(1, kk):
        out = out + _shift_rows(cur, tail, s) * w[kk - 1 - s:kk - s, :]
    return out

def _expand_heads(m, h0, n_cols):
    rows = m.shape[0]
    lane = lax.broadcasted_iota(jnp.int32, (rows, LANE), 1)
    cols = []
    for c in range(n_cols):
        lo = jnp.broadcast_to(m[:, h0 + 2 * c:h0 + 2 * c + 1], (rows, LANE))
        hi = jnp.broadcast_to(m[:, h0 + 2 * c + 1:h0 + 2 * c + 2], (rows, LANE))
        cols.append(jnp.where(lane < SSD_HEAD_DIM, lo, hi))
    return jnp.concatenate(cols, axis=1)


def _ssd_prompt_kernel(scb_ref, scc_ref, scx_ref, z0_ref, z1_ref, x0_ref, x1_ref, bc_ref, dt_ref,
                       scw_ref, cw_ref, cb_ref, dtb_ref, alog_ref, dexp_ref, nw_ref,
                       y_ref, scst_ref, cvst_ref, ssm_ref,
                       utail, xtail, s_scr):
    c = pl.program_id(1)
    q = scb_ref.shape[0]
    dsc = scb_ref.shape[1]
    gw = 2 * x0_ref.shape[1] // SSD_GROUPS
    hpg = gw // SSD_HEAD_DIM

    @pl.when(c == 0)
    def _():
        utail[...] = jnp.zeros(utail.shape, F32)
        xtail[...] = jnp.zeros(xtail.shape, F32)
        s_scr[...] = jnp.zeros(s_scr.shape, F32)

    u = scc_ref[...] * scx_ref[...]
    y_sc = scb_ref[...] * _causal_conv(u, utail[...], scw_ref[...])
    utail[...] = u[q - SUBLANE:, :]
    y_ref[:, :dsc] = y_sc.astype(BF16)

    bc = bc_ref[...]
    n_bc = bc.shape[1]
    o_bc = 2 * x0_ref.shape[1]
    bc_cols = slice(o_bc, o_bc + n_bc)
    bc_c = _silu(_causal_conv(bc, xtail[:, bc_cols], cw_ref[:, bc_cols]) + cb_ref[:, bc_cols])
    xtail[:, bc_cols] = bc[q - SUBLANE:, :]
    half = n_bc // 2

    dt = _softplus(dt_ref[...] + dtb_ref[...])
    a = dt * (-jnp.exp(alog_ref[...]))
    ri = lax.broadcasted_iota(jnp.int32, (q, q), 0)
    ci = lax.broadcasted_iota(jnp.int32, (q, q), 1)
    tril = ri >= ci
    a_cs = jnp.dot(tril.astype(F32), a, precision=HIGHEST, preferred_element_type=F32) * math.log2(math.e)
    a_cs_t = a_cs.T
    quad = 4 * SSD_HEAD_DIM
    lane_head = lax.broadcasted_iota(jnp.int32, (q, quad), 1) // SSD_HEAD_DIM

    for g in range(SSD_GROUPS):
        x_ref = x0_ref if g < SSD_GROUPS // 2 else x1_ref
        z_ref = z0_ref if g < SSD_GROUPS // 2 else z1_ref
        lo = (g % (SSD_GROUPS // 2)) * gw
        go = g * gw
        xg = x_ref[:, lo:lo + gw]
        xs = _silu(_causal_conv(xg, xtail[:, go:go + gw], cw_ref[:, go:go + gw]) + cb_ref[:, go:go + gw])
        xtail[:, go:go + gw] = xg[q - SUBLANE:, :]

        n_cols = gw // LANE
        dt_x = _expand_heads(dt, g * hpg, n_cols)
        a_x = _expand_heads(a_cs, g * hpg, n_cols)
        ea = jnp.exp2(a_x)
        dte = jnp.exp2(a_x[q - 1:q, :] - a_x)
        xdt = xs * dt_x
        xdt_b = xdt.astype(BF16)

        bg = bc_c[:, g * SSD_STATE:(g + 1) * SSD_STATE]
        cg_b = bc_c[:, half + g * SSD_STATE:half + (g + 1) * SSD_STATE].astype(BF16)
        cb_mat = lax.dot_general(cg_b, bg.astype(BF16), NT_DIMS, preferred_element_type=F32)
        s_g = s_scr[:, go:go + gw]
        y_off = jnp.dot(cg_b, s_g.astype(BF16), preferred_element_type=F32) * ea

        y_quads = []
        for qd in range(gw // quad):
            xq = xdt[:, qd * quad:(qd + 1) * quad]
            acc = None
            for i in range(4):
                h = g * hpg + qd * 4 + i
                seg = a_cs[:, h:h + 1] - a_cs_t[h:h + 1, :]
                dec = jnp.exp2(jnp.where(tril, seg, -jnp.inf))
                w_h = (cb_mat * dec).astype(BF16)
                xm = jnp.where(lane_head == i, xq, 0.0).astype(BF16)
                d = jnp.dot(w_h, xm, preferred_element_type=F32)
                acc = d if acc is None else acc + d
            y_quads.append(acc)
        y = jnp.concatenate(y_quads, axis=1) + y_off + dexp_ref[:, go:go + gw] * xs

        zg = z_ref[:, lo:lo + gw]
        y = y * _silu(zg)
        ms = jnp.mean(y * y, axis=-1, keepdims=True)
        y = y * lax.rsqrt(ms + EPS) * nw_ref[:, go:go + gw]
        y_ref[:, dsc + go:dsc + go + gw] = y.astype(BF16)

        s_new = s_g * ea[q - 1:q, :] + jnp.dot(bg.T.astype(BF16), (xdt * dte).astype(BF16),
                                               preferred_element_type=F32)
        s_scr[:, go:go + gw] = s_new

    @pl.when(c == pl.num_programs(1) - 1)
    def _():
        scst_ref[0] = utail[...]
        cvst_ref[0] = xtail[...]
        ssm_ref[0] = s_scr[...].T


def _ssd_prompt(proj, dt_raw, scw, cw, cb, dtb, alog, dexp, nw, bp, seq):
    q = min(SSD_CHUNK, seq)
    nc = seq // q
    dsc = scw.shape[1]
    d_inner = dexp.shape[1]
    conv_dim = cw.shape[1]
    bw = dsc

    def col(kb):
        return pl.BlockSpec((q, bw), lambda b, c: (b * nc + c, kb))

    n_z = d_inner // bw
    specs = [col(0), col(1), col(2)]
    specs += [col(3 + i) for i in range(n_z)]
    specs += [col(3 + n_z + i) for i in range(n_z)]
    specs += [col(3 + 2 * n_z)]
    specs += [pl.BlockSpec((q, LANE), lambda b, c: (b * nc + c, 0))]
    for arr in (scw, cw, cb, dtb, alog, dexp, nw):
        specs.append(pl.BlockSpec(arr.shape, lambda b, c: (0, 0)))
    return pl.pallas_call(
        _ssd_prompt_kernel,
        grid=(bp, nc),
        in_specs=specs,
        out_specs=[pl.BlockSpec((q, dsc + d_inner), lambda b, c: (b * nc + c, 0)),
                   pl.BlockSpec((1, SUBLANE, dsc), lambda b, c: (b, 0, 0)),
                   pl.BlockSpec((1, SUBLANE, conv_dim), lambda b, c: (b, 0, 0)),
                   pl.BlockSpec((1, d_inner, SSD_STATE), lambda b, c: (b, 0, 0))],
        out_shape=[jax.ShapeDtypeStruct((bp * seq, dsc + d_inner), BF16),
                   jax.ShapeDtypeStruct((bp, SUBLANE, dsc), F32),
                   jax.ShapeDtypeStruct((bp, SUBLANE, conv_dim), F32),
                   jax.ShapeDtypeStruct((bp, d_inner, SSD_STATE), F32)],
        scratch_shapes=[pltpu.VMEM((SUBLANE, dsc), F32),
                        pltpu.VMEM((SUBLANE, conv_dim), F32),
                        pltpu.VMEM((SSD_STATE, d_inner), F32)],
        compiler_params=_cparams("parallel", "arbitrary"),
        name="ssd_prompt",
    )(*([proj] * (4 + 2 * n_z)), dt_raw, scw, cw, cb, dtb, alog, dexp, nw)


def _even_sample_prep_kernel(proj_ref, dt_ref, scst_ref, cvst_ref, scw_ref, cw_ref, cb_ref, dtb_ref, alog_ref,
                             ysc_ref, scn_ref, cvn_ref, xs_ref, b_ref, c_ref, xdt_t_ref, dec_ref):
    dsc = scw_ref.shape[1]
    conv_dim = cw_ref.shape[1]
    d_inner = xs_ref.shape[1]
    n_heads = d_inner // SSD_HEAD_DIM
    scb = proj_ref[:, :dsc]
    u = proj_ref[:, dsc:2 * dsc] * proj_ref[:, 2 * dsc:3 * dsc]
    xbc = proj_ref[:, 3 * dsc + d_inner:3 * dsc + d_inner + conv_dim]
    b0 = scst_ref[:, :dsc]
    b1 = scst_ref[:, dsc:]
    ysc_ref[...] = scb * (b0 * scw_ref[0:1, :] + b1 * scw_ref[1:2, :] + u * scw_ref[2:3, :])
    scn_ref[:, :dsc] = b1
    scn_ref[:, dsc:] = u

    c0 = cvst_ref[:, :conv_dim]
    c1 = cvst_ref[:, conv_dim:2 * conv_dim]
    c2 = cvst_ref[:, 2 * conv_dim:]
    act = _silu(c0 * cw_ref[0:1, :] + c1 * cw_ref[1:2, :] + c2 * cw_ref[2:3, :] + xbc * cw_ref[3:4, :] + cb_ref[...])
    cvn_ref[:, :conv_dim] = c1
    cvn_ref[:, conv_dim:2 * conv_dim] = c2
    cvn_ref[:, 2 * conv_dim:] = xbc
    xs = act[:, :d_inner]
    n_bc = (conv_dim - d_inner) // 2
    xs_ref[...] = xs
    b_ref[...] = act[:, d_inner:d_inner + n_bc]
    c_ref[...] = act[:, d_inner + n_bc:]

    dt = _softplus(dt_ref[...] + dtb_ref[...])
    dec_ref[...] = jnp.exp(dt * (-jnp.exp(alog_ref[...])))
    xdt_t_ref[...] = (xs * _expand_heads(dt, 0, n_heads // 2)).T


def _even_sample_prep(proj, dt_raw, scst, cvst, scw, cw, cb, dtb, alog, d_inner):
    bs = proj.shape[0]
    dsc = scw.shape[1]
    conv_dim = cw.shape[1]
    n_bc = (conv_dim - d_inner) // 2
    shapes = [(bs, dsc), (bs, 2 * dsc), (bs, 3 * conv_dim), (bs, d_inner), (bs, n_bc), (bs, n_bc),
              (d_inner, bs), (bs, LANE)]
    return pl.pallas_call(
        _even_sample_prep_kernel,
        out_shape=[jax.ShapeDtypeStruct(s, F32) for s in shapes],
        compiler_params=pltpu.CompilerParams(vmem_limit_bytes=VMEM_LIMIT),
        name="even_sample_prep",
    )(proj, dt_raw, scst, cvst, scw, cw, cb, dtb, alog)


def _ssm_sample_kernel(dec_ref, h0_ref, xt_ref, b_ref, c_ref, hn_ref, yt_ref):
    h = pl.program_id(0)
    bs = h0_ref.shape[0]
    p = h0_ref.shape[2]
    n = h0_ref.shape[3]
    lane = lax.broadcasted_iota(jnp.int32, (p, bs), 1)
    xt = xt_ref[...]
    ones = jnp.ones((n, bs), BF16)
    yt = jnp.zeros((p, bs), F32)
    for b in range(bs):
        xcol = jnp.broadcast_to(xt[:, b:b + 1], (p, n))
        hn = h0_ref[b, 0] * dec_ref[b, h] + xcol * b_ref[b:b + 1, :]
        hn_ref[b, 0] = hn
        ysum = jnp.zeros((p, bs), F32)
        rest = hn * c_ref[b:b + 1, :]
        for _ in range(2):
            piece = rest.astype(BF16)
            ysum = ysum + jnp.dot(piece, ones, preferred_element_type=F32)
            rest = rest - piece.astype(F32)
        yt = jnp.where(lane == b, ysum, yt)
    yt_ref[...] = yt


def _ssm_sample(h0, xdt_t, dec, bm, cm):
    bs, n_heads, p, n = h0.shape
    hpg = n_heads // SSD_GROUPS
    return pl.pallas_call(
        _ssm_sample_kernel,
        grid=(n_heads,),
        in_specs=[pl.BlockSpec(memory_space=pltpu.SMEM),
                  pl.BlockSpec((bs, 1, p, n), lambda h: (0, h, 0, 0)),
                  pl.BlockSpec((p, bs), lambda h: (h, 0)),
                  pl.BlockSpec((bs, n), lambda h: (0, h // hpg)),
                  pl.BlockSpec((bs, n), lambda h: (0, h // hpg))],
        out_specs=[pl.BlockSpec((bs, 1, p, n), lambda h: (0, h, 0, 0)),
                   pl.BlockSpec((p, bs), lambda h: (h, 0))],
        out_shape=[jax.ShapeDtypeStruct(h0.shape, F32), jax.ShapeDtypeStruct((n_heads * p, bs), F32)],
        compiler_params=_cparams("parallel"),
        name="ssm_sample",
    )(dec, h0, xdt_t, bm, cm)


def _even_out_sample_kernel(yt_ref, xs_ref, proj_ref, ysc_ref, dexp_ref, nw_ref, w_ref, x_ref, g_ref, o_ref):
    d_inner = xs_ref.shape[1]
    dsc = ysc_ref.shape[1]
    gw = d_inner // SSD_GROUPS
    y = yt_ref[...].T + dexp_ref[...] * xs_ref[...]
    y = y * _silu(proj_ref[:, 3 * dsc:3 * dsc + d_inner])
    parts = []
    for g in range(SSD_GROUPS):
        yg = y[:, g * gw:(g + 1) * gw]
        ms = jnp.mean(yg * yg, axis=-1, keepdims=True)
        parts.append(yg * lax.rsqrt(ms + EPS))
    y_ssd = jnp.concatenate(parts, axis=1) * nw_ref[...]
    cat = jnp.concatenate([ysc_ref[...], y_ssd], axis=1).astype(BF16)
    o_ref[...] = x_ref[...] + g_ref[0] * jnp.dot(cat, w_ref[...], preferred_element_type=F32)


def _even_out_sample(y_t, xs, proj, ysc, dexp, nw, w_out, x, mod):
    bs, d = x.shape
    full = lambda a: pl.BlockSpec(a.shape, lambda i: (0,) * a.ndim)
    return pl.pallas_call(
        _even_out_sample_kernel,
        grid=(1,),
        in_specs=[full(y_t), full(xs), full(proj), full(ysc), full(dexp), full(nw), full(w_out), full(x),
                  pl.BlockSpec((1, bs, d), lambda i: (0, 0, 2))],
        out_specs=pl.BlockSpec((bs, d), lambda i: (0, 0)),
        out_shape=jax.ShapeDtypeStruct((bs, d), F32),
        compiler_params=_cparams("arbitrary"),
        name="even_out_sample",
    )(y_t, xs, proj, ysc, dexp, nw, w_out, x, mod)


def _rope_table_kernel(ca_ref, cb_ref, cc_ref, *, pos0, stride):
    rows = ca_ref.shape[0]
    row = lax.broadcasted_iota(jnp.int32, (rows, LANE), 0) + pl.program_id(0) * rows
    d = lax.broadcasted_iota(jnp.int32, (rows, LANE), 1) % ATT_HEAD_DIM
    half = ROT_DIM // 2
    pos = (pos0 + row * stride).astype(F32)
    inv = jnp.power(jnp.full((rows, LANE), ROPE_THETA, F32), -((d % half).astype(F32) * 2.0 / ROT_DIM))
    ang = pos * inv
    cs = jnp.cos(ang)
    sn = jnp.sin(ang)
    ca_ref[...] = jnp.where(d < ROT_DIM, cs, 1.0)
    cb_ref[...] = jnp.where(d < half, -sn, 0.0)
    cc_ref[...] = jnp.where(d < half, 0.0, jnp.where(d < ROT_DIM, sn, 0.0))


def _rope_tables(rows, pos0, stride):
    tr = min(rows, 512)
    spec = pl.BlockSpec((tr, LANE), lambda i: (i, 0))
    return pl.pallas_call(
        functools.partial(_rope_table_kernel, pos0=pos0, stride=stride),
        grid=(rows // tr,),
        out_specs=[spec, spec, spec],
        out_shape=[jax.ShapeDtypeStruct((rows, LANE), F32)] * 3,
        compiler_params=_cparams("parallel"),
        name="rope_tables",
    )()


def _rope(t, ca, cb, cc):
    cols = []
    for c in range(t.shape[1] // LANE):
        s = t[:, c * LANE:(c + 1) * LANE]
        cols.append(s * ca + pltpu.roll(s, LANE - ROT_DIM // 2, axis=1) * cb + pltpu.roll(s, ROT_DIM // 2, axis=1) * cc)
    return jnp.concatenate(cols, axis=1)


def _sel_keys(k, j):
    lane = lax.broadcasted_iota(jnp.int32, (k.shape[0], LANE), 1)
    onehot = jnp.where(lane == j, 1.0, 0.0)
    in_sel = lane < SEL_LANES
    k_lo = jnp.concatenate([jnp.where(in_sel, onehot, k[:, :LANE]), k[:, LANE:]], axis=1)
    k_hi = jnp.concatenate([k[:, :LANE], jnp.where(in_sel, onehot, k[:, LANE:])], axis=1)
    return k_lo.astype(BF16), k_hi.astype(BF16)


def _qkv_kernel(x_ref, nw_ref, sh_ref, sc_ref, w_ref, ca_ref, cb_ref, cc_ref, q_ref, k_ref, v_ref, *blk_refs,
                blocks_per_seq):
    nq = q_ref.shape[1]
    nk = (w_ref.shape[1] - nq) // 2
    hb = _norm_mod(x_ref[...], nw_ref[...], sh_ref[0], sc_ref[0]).astype(BF16)
    qkv = jnp.dot(hb, w_ref[...], preferred_element_type=F32)
    ca, cb, cc = ca_ref[...], cb_ref[...], cc_ref[...]
    q_ref[...] = _rope(qkv[:, :nq], ca, cb, cc)
    k = _rope(qkv[:, nq:nq + nk], ca, cb, cc)
    v = qkv[:, nq + nk:]
    if blocks_per_seq:
        km_ref, klo_ref, khi_ref, vt_ref = blk_refs
        vt = v.T
        k_ref[0] = k.T
        v_ref[0] = vt
        km_ref[0] = jnp.mean(k, axis=0, keepdims=True)
        klo_ref[...], khi_ref[...] = _sel_keys(k, pl.program_id(0) % blocks_per_seq)
        hd = ATT_HEAD_DIM
        ones_row = jnp.where(lax.broadcasted_iota(jnp.int32, (BF16_SUBLANE, vt.shape[1]), 0) == 0, 1.0, 0.0)
        vt_ref[0] = jnp.concatenate([piece for h in range(nk // hd) for piece in (vt[h * hd:(h + 1) * hd], ones_row)],
                                    axis=0).astype(BF16)
    else:
        k_ref[...] = k
        v_ref[...] = v


def _qkv(x, nw, mod, w, tabs, rows_per_group, tm, per_block):
    t, d = x.shape
    nq = ATT_HEADS * ATT_HEAD_DIM
    nk = ATT_KV_HEADS * ATT_HEAD_DIM
    tab_tiles = tabs[0].shape[0] // tm
    tab = pl.BlockSpec((tm, LANE), lambda i: (i % tab_tiles, 0))
    row = lambda n: pl.BlockSpec((tm, n), lambda i: (i, 0))
    if per_block:
        bps = rows_per_group // tm
        kv_t = pl.BlockSpec((1, nk, tm), lambda i: (i // bps, 0, i % bps))
        v_rows = ATT_KV_HEADS * (ATT_HEAD_DIM + BF16_SUBLANE)
        out_specs = [row(nq), kv_t, kv_t, pl.BlockSpec((1, 1, nk), lambda i: (i, 0, 0)), row(nk), row(nk),
                     pl.BlockSpec((1, v_rows, tm), lambda i: (i, 0, 0))]
        out_shape = [jax.ShapeDtypeStruct((t, nq), F32),
                     jax.ShapeDtypeStruct((t // rows_per_group, nk, rows_per_group), F32),
                     jax.ShapeDtypeStruct((t // rows_per_group, nk, rows_per_group), F32),
                     jax.ShapeDtypeStruct((t // tm, 1, nk), F32), jax.ShapeDtypeStruct((t, nk), BF16),
                     jax.ShapeDtypeStruct((t, nk), BF16), jax.ShapeDtypeStruct((t // tm, v_rows, tm), BF16)]
    else:
        out_specs = [row(nq), row(nk), row(nk)]
        out_shape = [jax.ShapeDtypeStruct((t, nq), F32), jax.ShapeDtypeStruct((t, nk), F32),
                     jax.ShapeDtypeStruct((t, nk), F32)]
    return pl.pallas_call(
        functools.partial(_qkv_kernel, blocks_per_seq=rows_per_group // tm if per_block else 0),
        grid=(t // tm,),
        in_specs=[row(d), pl.BlockSpec((1, d), lambda i: (0, 0)),
                  _mod_spec(mod, 0, rows_per_group, tm), _mod_spec(mod, 1, rows_per_group, tm),
                  _resident(w.shape), tab, tab, tab],
        out_specs=out_specs,
        out_shape=out_shape,
        compiler_params=_cparams("parallel"),
        name="qkv_rope",
    )(x, nw, mod, mod, w, *tabs)


def _select_bias(scores, n_past, n_rounds):
    lane = lax.broadcasted_iota(jnp.int32, scores.shape, 1)
    lanef = lane.astype(F32)
    s = jnp.where(lane < n_past, scores, -jnp.inf)
    bias = jnp.full(scores.shape, NEG_BIG, F32)
    for r in range(n_rounds):
        mx = jnp.max(s, axis=1, keepdims=True)
        idx = jnp.min(jnp.where(s == mx, lanef, float(LANE)), axis=1, keepdims=True)
        bias = jnp.where(lanef == jnp.where(n_past > r, idx, -1.0), 0.0, bias)
        s = jnp.where(lanef == idx, -jnp.inf, s)
    return bias


def _select_bias_t(scores, n_past, n_rounds):
    row = lax.broadcasted_iota(jnp.int32, scores.shape, 0)
    rowf = row.astype(F32)
    s = jnp.where(row < n_past, scores, -jnp.inf)
    bias = jnp.full(scores.shape, NEG_BIG, F32)
    for r in range(n_rounds):
        mx = jnp.max(s, axis=0, keepdims=True)
        idx = jnp.min(jnp.where(s == mx, rowf, float(SEL_LANES)), axis=0, keepdims=True)
        bias = jnp.where(rowf == jnp.where(n_past > r, idx, -1.0), 0.0, bias)
        s = jnp.where(rowf == idx, -jnp.inf, s)
    return jnp.where(row == n_past, 0.0, bias)


def _moba_prompt_kernel(q_ref, klo_ref, khi_ref, vt_ref, km_ref, o_ref, qa_scr, m_scr, over_scr, acc_scr):
    qi = pl.program_id(1)
    blk = q_ref.shape[0]
    kvw = klo_ref.shape[1]
    hd = ATT_HEAD_DIM
    n_q = ATT_GROUP * blk
    kmean = km_ref[0]
    km_hi = kmean.astype(BF16)
    km_lo = (kmean - km_hi.astype(F32)).astype(BF16)
    n_rounds = min(MOBA_TOPK, klo_ref.shape[0] // blk)

    q_t = q_ref[...].T
    for h in range(ATT_KV_HEADS):
        qh_b = jnp.concatenate([q_t[g * kvw + h * hd:g * kvw + (h + 1) * hd, :] for g in range(ATT_GROUP)],
                               axis=1).astype(BF16)
        scores_t = (jnp.dot(km_hi[:, h * hd:(h + 1) * hd], qh_b, preferred_element_type=F32)
                    + jnp.dot(km_lo[:, h * hd:(h + 1) * hd], qh_b, preferred_element_type=F32))
        bias_t = _select_bias_t(scores_t, qi, n_rounds)
        sel_row = LANE if h == 0 else 0
        qa_scr[h] = jnp.zeros((kvw, n_q), BF16)
        qa_scr[h, h * hd:(h + 1) * hd, :] = qh_b
        qa_scr[h, sel_row:sel_row + SEL_LANES, :] = bias_t.astype(BF16)

    v_rows = vt_ref.shape[1] // ATT_KV_HEADS
    key_row = lax.broadcasted_iota(jnp.int32, (blk, n_q), 0)
    q_pos = lax.broadcasted_iota(jnp.int32, (blk, n_q), 1) % blk

    def scores(h, j):
        off = pl.multiple_of(j * blk, blk)
        ka = (khi_ref if h == 0 else klo_ref)[pl.ds(off, blk), :]
        return jnp.dot(ka, qa_scr[h], preferred_element_type=F32)

    def softmax(h, s, first):
        if first:
            s = jnp.where(key_row <= q_pos, s, NEG_BIG)
        mx = jnp.max(s, axis=0, keepdims=True)
        if first:
            m_new, alpha = mx, None
        else:
            m_old = m_scr[h, 0:1, :]
            m_new = jnp.maximum(m_old, mx)
            alpha = jnp.exp2(m_old - m_new)
        m_scr[h] = jnp.broadcast_to(m_new, (SUBLANE, n_q))
        return jnp.exp2((s - m_new).astype(BF16)), alpha

    def values(h, j, p, alpha):
        vt_h = vt_ref[j, h * v_rows:(h + 1) * v_rows, :]
        pv = jnp.dot(vt_h, p, preferred_element_type=F32)
        acc_scr[h] = pv if alpha is None else acc_scr[h] * alpha + pv

    def block_step(j, first):
        s_next = scores(0, j)
        for h in range(ATT_KV_HEADS):
            s = s_next
            if h + 1 < ATT_KV_HEADS:
                s_next = scores(h + 1, j)
            p, alpha = softmax(h, s, first)
            values(h, j, p, alpha)

    def finalize():
        for g in range(ATT_GROUP):
            cols = slice(g * blk, (g + 1) * blk)
            o_t = jnp.concatenate([acc_scr[h, :hd, cols] / acc_scr[h, hd:hd + 1, cols]
                                   for h in range(ATT_KV_HEADS)], axis=0)
            o_ref[:, g * kvw:(g + 1) * kvw] = o_t.T.astype(BF16)

    block_step(qi, True)

    for h in range(ATT_KV_HEADS):
        over_scr[h] = jnp.full((SUBLANE, n_q), NEG_BIG, F32)

    def fast_step(j, nb):
        off = pl.multiple_of(j * blk, blk)

        def span_scores(h):
            ka = (khi_ref if h == 0 else klo_ref)[pl.ds(off, nb * blk), :]
            return jnp.dot(ka, qa_scr[h], preferred_element_type=F32)

        s_next = span_scores(0)
        for h in range(ATT_KV_HEADS):
            s = s_next
            if h + 1 < ATT_KV_HEADS:
                s_next = span_scores(h + 1)
            d = s - m_scr[h, 0:1, :]
            over_scr[h] = jnp.maximum(over_scr[h], jnp.broadcast_to(jnp.max(d, axis=0, keepdims=True), (SUBLANE, n_q)))
            p = jnp.exp2(jnp.minimum(d, EXP2_CLAMP)).astype(BF16)
            rows = slice(h * v_rows, (h + 1) * v_rows)
            vt_h = jnp.concatenate([vt_ref[j + i, rows, :] for i in range(nb)], axis=1)
            acc_scr[h] = acc_scr[h] + jnp.dot(vt_h, p, preferred_element_type=F32)

    def fast_body(t, carry):
        fast_step(2 * t, 2)
        return carry

    lax.fori_loop(0, qi // 2, fast_body, 0)

    @pl.when(qi % 2 == 1)
    def _():
        fast_step(qi - 1, 1)

    finalize()

    @pl.when(jnp.max(over_scr[...]) > EXP2_CLAMP)
    def _():
        block_step(qi, True)

        def body(j, carry):
            block_step(j, False)
            return carry

        lax.fori_loop(0, qi, body, 0)
        finalize()


def _moba_prompt(q, k_lo, k_hi, v_t, kmean, bp, seq):
    blk = MOBA_BLOCK
    nq = seq // blk
    kvw = k_lo.shape[1]
    n_q = ATT_GROUP * blk
    v_rows = v_t.shape[1]
    return pl.pallas_call(
        _moba_prompt_kernel,
        grid=(bp, nq),
        in_specs=[pl.BlockSpec((blk, q.shape[1]), lambda b, i: (b * nq + i, 0)),
                  pl.BlockSpec((seq, kvw), lambda b, i: (b, 0)),
                  pl.BlockSpec((seq, kvw), lambda b, i: (b, 0)),
                  pl.BlockSpec((nq, v_rows, blk), lambda b, i: (b, 0, 0)),
                  pl.BlockSpec((1, SEL_LANES, kvw), lambda b, i: (b, 0, 0))],
        out_specs=pl.BlockSpec((blk, q.shape[1]), lambda b, i: (b * nq + i, 0)),
        out_shape=jax.ShapeDtypeStruct(q.shape, BF16),
        scratch_shapes=[pltpu.VMEM((ATT_KV_HEADS, kvw, n_q), BF16),
                        pltpu.VMEM((ATT_KV_HEADS, SUBLANE, n_q), F32),
                        pltpu.VMEM((ATT_KV_HEADS, SUBLANE, n_q), F32),
                        pltpu.VMEM((ATT_KV_HEADS, v_rows // ATT_KV_HEADS, n_q), F32)],
        compiler_params=_cparams("parallel", "arbitrary"),
        name="moba_prompt",
    )(q, k_lo, k_hi, v_t, kmean)


def _moba_sample_kernel(pt_ref, q_ref, kn_ref, vn_ref, avg_ref, *rest, n_pages):
    k_pages = rest[:n_pages]
    v_pages = rest[n_pages:2 * n_pages]
    o_ref = rest[2 * n_pages]
    page = k_pages[0].shape[3]
    n_past = n_pages * page // MOBA_BLOCK
    q_rows = q_ref[0]
    kn_rows = kn_ref[0]
    vn_rows = vn_ref[0]
    n_rows = q_rows.shape[0]
    q_b = q_rows.astype(BF16)

    heads = range(ATT_KV_HEADS)
    rows = [slice(h * SUBLANE, (h + 1) * SUBLANE) for h in heads]
    kt_bs = [jnp.concatenate([r[0, h] for r in k_pages], axis=1).astype(BF16) for h in heads]
    logits = jnp.concatenate([jnp.dot(q_b[rows[h]], kt_bs[h], preferred_element_type=F32) for h in heads], axis=0)
    kmts = [jnp.dot(kt_bs[h], avg_ref[...], preferred_element_type=F32) for h in heads]
    vts =[jnp.concatenate([r[0, h] for r in v_pages], axis=1).astype(BF16) for h in heads]

    blocks = range(n_past)
    keys = [slice(j * MOBA_BLOCK, (j + 1) * MOBA_BLOCK) for j in blocks]
    mjs = [jnp.max(logits[:, keys[j]], axis=1, keepdims=True) for j in blocks]
    km_his = [km.astype(BF16) for km in kmts]
    km_los = [(kmts[h] - km_his[h].astype(F32)).astype(BF16) for h in heads]
    scores = jnp.concatenate([jnp.dot(q_b[rows[h]], km_his[h], preferred_element_type=F32)
                              + jnp.dot(q_b[rows[h]], km_los[h], preferred_element_type=F32) for h in heads], axis=0)
    pjs = [jnp.exp2(logits[:, keys[j]] - mjs[j]) for j in blocks]
    sjs = [jnp.sum(pjs[j], axis=1, keepdims=True) for j in blocks]
    pj_bs = [pjs[j].astype(BF16) for j in blocks]
    blk_out = [jnp.concatenate([lax.dot_general(pj_bs[j][rows[h]], vts[h][:, keys[j]], NT_DIMS,
                                                preferred_element_type=F32) for h in heads], axis=0)
               for j in blocks]
    bias = _select_bias(scores, n_past, min(MOBA_TOPK, n_past + 1))
    lane = lax.broadcasted_iota(jnp.int32, (n_rows, LANE), 1)
    blk_max = jnp.full((n_rows, LANE), NEG_BIG, F32)
    blk_sum = jnp.zeros((n_rows, LANE), F32)
    for j in blocks:
        blk_max = jnp.where(lane == j, mjs[j], blk_max)
        blk_sum = jnp.where(lane == j, sjs[j], blk_sum)
    l_new = jnp.sum(q_rows * kn_rows, axis=1, keepdims=True)
    chosen = blk_max + bias
    m = jnp.maximum(jnp.max(chosen, axis=1, keepdims=True), l_new)
    w = jnp.exp2(chosen - m)
    pn = jnp.exp2(l_new - m)
    denom = jnp.sum(w * blk_sum, axis=1, keepdims=True) + pn
    terms = [w[:, j:j + 1] * blk_out[j] for j in blocks] + [pn * vn_rows]
    while len(terms) > 1:
        terms = [terms[i] + terms[i + 1] for i in range(0, len(terms) - 1, 2)] + terms[len(terms) & ~1:]
    o_ref[0] = terms[0] / denom


def _moba_sample(q, kn, vn, cache_kt, cache_vt, page_table):
    bs = q.shape[0]
    n_pages = page_table.shape[1]
    _, n_kv, hd, page = cache_kt.shape
    past = n_pages * page
    n_rows = n_kv * SUBLANE
    assert ATT_GROUP <= SUBLANE

    def head_rows(t):
        return jnp.pad(t, ((0, 0), (0, 0), (0, SUBLANE - t.shape[2]), (0, 0))).reshape(bs, n_rows, hd)

    q_rows = head_rows(q.reshape(bs, ATT_GROUP, n_kv, hd).transpose(0, 2, 1, 3))
    kn_rows = jnp.broadcast_to(kn.reshape(bs, n_kv, 1, hd), (bs, n_kv, SUBLANE, hd)).reshape(bs, n_rows, hd)
    vn_rows = jnp.broadcast_to(vn.reshape(bs, n_kv, 1, hd), (bs, n_kv, SUBLANE, hd)).reshape(bs, n_rows, hd)
    one = pl.BlockSpec((1, n_rows, hd), lambda b, pt: (b, 0, 0))
    avg = jnp.where(jnp.arange(past)[:, None] // MOBA_BLOCK == jnp.arange(LANE)[None, :], 1.0 / MOBA_BLOCK, 0.0).astype(BF16)

    def page_spec(p):
        return pl.BlockSpec((1, n_kv, hd, page), lambda b, pt: (pt[b, p], 0, 0, 0))

    grid_spec = pltpu.PrefetchScalarGridSpec(
        num_scalar_prefetch=1,
        grid=(bs,),
        in_specs=[one, one, one, pl.BlockSpec((past, LANE), lambda b, pt: (0, 0))]
        + [page_spec(p) for p in range(n_pages)] * 2,
        out_specs=one,
    )
    o_rows = pl.pallas_call(
        functools.partial(_moba_sample_kernel, n_pages=n_pages),
        grid_spec=grid_spec,
        out_shape=jax.ShapeDtypeStruct((bs, n_rows, hd), F32),
        compiler_params=_cparams("parallel"),
        name="moba_sample",
    )(page_table, q_rows, kn_rows, vn_rows, avg, *([cache_kt] * n_pages), *([cache_vt] * n_pages))
    return o_rows.reshape(bs, n_kv, SUBLANE, hd)[:, :, :ATT_GROUP].transpose(0, 2, 1, 3).reshape(bs, ATT_GROUP * n_kv * hd)


def _pad_lanes(v, fill=0.0):
    return jnp.pad(v.reshape(1, -1), ((0, 0), (0, LANE - v.shape[0])), constant_values=fill)


def kernel(x_prompt, x_sample, state_sc_conv, state_ssd_conv, state_ssm, cache_k, cache_v, page_table,
           c_prompt, c_sample, norm_mix_w, norm_mlp_w, norm_final_w, w_ada, b_ada, w_in_e, sc_conv_w,
           ssd_conv_w, ssd_conv_b, ssd_dt_bias, ssd_a_log, ssd_d, ssd_norm_w, w_out_e, w_qkv, w_o, w_up, w_down):
    bp, seq, d = x_prompt.shape
    bs, dec_seq, _ = x_sample.shape
    assert dec_seq == 1 and seq % MOBA_BLOCK == 0 and seq % SSD_CHUNK == 0 and seq // MOBA_BLOCK <= SEL_LANES
    n_pages, page = page_table.shape[1], cache_k.shape[2]
    past_len = n_pages * page
    assert past_len % MOBA_BLOCK == 0 and MOBA_BLOCK % page == 0
    dsc = sc_conv_w.shape[2]
    conv_dim = ssd_conv_w.shape[2]
    n_heads = ssd_d.shape[1]
    d_inner = n_heads * SSD_HEAD_DIM
    main = 3 * dsc + d_inner + conv_dim
    kvw = ATT_KV_HEADS * ATT_HEAD_DIM
    nq = ATT_HEADS * ATT_HEAD_DIM

    xp = x_prompt.reshape(bp * seq, d)
    xs = x_sample.reshape(bs, d)

    rows = -(-(bs + bp) // SUBLANE) * SUBLANE
    c_all = jnp.pad(jnp.concatenate([c_sample, c_prompt], axis=0), ((0, rows - bs - bp), (0, 0)))
    ada = _ada(c_all, w_ada, b_ada)

    w_in = w_in_e[0][:, :main].astype(BF16)
    w_dt = jnp.pad(w_in_e[0][:, main:], ((0, 0), (0, LANE - n_heads))).astype(BF16)
    w_out = w_out_e[0].astype(BF16)
    wq = w_qkv[0][:, :nq].reshape(d, ATT_KV_HEADS, ATT_GROUP, ATT_HEAD_DIM).transpose(0, 2, 1, 3).reshape(d, nq)
    w_qkv_p = jnp.concatenate([wq * (ATT_HEAD_DIM ** -0.5 * math.log2(math.e)), w_qkv[0][:, nq:]], axis=1).astype(BF16)
    w_o_p = w_o[0].reshape(ATT_KV_HEADS, ATT_GROUP, ATT_HEAD_DIM, d).transpose(1, 0, 2, 3).reshape(nq, d).astype(BF16)
    w_up_b = w_up.astype(BF16)
    w_down_b = w_down.astype(BF16)

    scw = sc_conv_w[0]
    cw = ssd_conv_w[0]
    cb = ssd_conv_b[0].reshape(1, conv_dim)
    dtb = _pad_lanes(ssd_dt_bias[0])
    alog = _pad_lanes(ssd_a_log[0])
    dexp = jnp.repeat(ssd_d[0], SSD_HEAD_DIM).reshape(1, d_inner)
    ssd_nw = ssd_norm_w[0].reshape(1, d_inner)

    mod_p = ada[0, bs:bs + bp].reshape(bp, 1, 6 * d)
    mod_s = ada[0, :bs].reshape(1, bs, 6 * d)
    nw_mix = norm_mix_w[0].reshape(1, d)
    nw_mlp = norm_mlp_w[0].reshape(1, d)
    fw = norm_final_w.reshape(1, d)

    proj_p, dt_p = _nm_matmul(xp, nw_mix, mod_p, 0, 1, w_in, w_dt, seq, min(seq, 1024), 2048)
    ycat_p, scst_p, cvst_p, ssm_p = _ssd_prompt(proj_p, dt_p, scw, cw, cb, dtb, alog, dexp, ssd_nw, bp, seq)
    xp = _mm_res(ycat_p, w_out, xp, mod_p, 2, seq, 512)
    xp = _mlp(xp, nw_mlp, mod_p, w_up_b[0], w_down_b[0], fw, seq, 512, False)

    proj_s, dt_s = _nm_matmul(xs, nw_mix, mod_s, 0, 1, w_in, w_dt, bs, bs, 1024)
    ysc_s, scn_s, cvn_s, xs_s, b_s, c_s, xdt_t, dec_s = _even_sample_prep(
        proj_s, dt_s, state_sc_conv[0].reshape(bs, -1), state_ssd_conv[0].reshape(bs, -1),
        scw, cw, cb, dtb, alog, d_inner)
    ssm_s, y_t = _ssm_sample(state_ssm[0], xdt_t, dec_s, b_s, c_s)
    xs = _even_out_sample(y_t, xs_s, proj_s, ysc_s, dexp, ssd_nw, w_out, xs, mod_s)
    xs = _mlp(xs, nw_mlp, mod_s, w_up_b[0], w_down_b[0], fw, bs, bs, False)

    mod_p = ada[1, bs:bs + bp].reshape(bp, 1, 6 * d)
    mod_s = ada[1, :bs].reshape(1, bs, 6 * d)
    nw_mix = norm_mix_w[1].reshape(1, d)
    nw_mlp = norm_mlp_w[1].reshape(1, d)

    tabs_p = _rope_tables(seq, 0, 1)
    q_p, kt_p, vt_p, km_p, klo_p, khi_p, vtb_p = _qkv(xp, nw_mix, mod_p, w_qkv_p, tabs_p, seq, MOBA_BLOCK, True)
    n_blk = seq // MOBA_BLOCK
    kmean_p = jnp.pad(km_p.reshape(bp, n_blk, kvw), ((0, 0), (0, SEL_LANES - n_blk), (0, 0)))
    o_p = _moba_prompt(q_p, klo_p, khi_p, vtb_p, kmean_p, bp, seq)
    xp = _mm_res(o_p, w_o_p, xp, mod_p, 2, seq, 512)
    y_prompt = _mlp(xp, nw_mlp, mod_p, w_up_b[1], w_down_b[1], fw, seq, 512, True)

    tabs_s = _rope_tables(bs, past_len, 0)
    q_s, k_s, v_s = _qkv(xs, nw_mix, mod_s, w_qkv_p, tabs_s, bs, bs, False)
    o_s = _moba_sample(q_s, k_s, v_s, cache_k[0].transpose(0, 2, 3, 1), cache_v[0].transpose(0, 2, 3, 1), page_table)
    xs = _mm_res(o_s.reshape(bs, nq).astype(BF16), w_o_p, xs, mod_s, 2, bs, bs)
    y_sample = _mlp(xs, nw_mlp, mod_s, w_up_b[1], w_down_b[1], fw, bs, bs, True)

    def kv_out(t):
        return t.reshape(bp, ATT_KV_HEADS, ATT_HEAD_DIM, seq).transpose(0, 3, 1, 2)[None]

    return (y_prompt.reshape(bp, seq, d), y_sample.reshape(bs, 1, d),
            scst_p[:, SUBLANE - 2:, :][None], scn_s.reshape(1, bs, 2, dsc),
            cvst_p[:, SUBLANE - 3:, :][None], cvn_s.reshape(1, bs, 3, conv_dim),
            ssm_p.reshape(1, bp, n_heads, SSD_HEAD_DIM, SSD_STATE), ssm_s[None],
            kv_out(kt_p), k_s.reshape(1, bs, 1, ATT_KV_HEADS, ATT_HEAD_DIM),
            kv_out(vt_p), v_s.reshape(1, bs, 1, ATT_KV_HEADS, ATT_HEAD_DIM))
```

```python
import functools
import math

import jax
import jax.numpy as jnp
from jax import lax
from jax.experimental import pallas as pl
from jax.experimental.pallas import tpu as pltpu

F32 = jnp.float32
BF16 = jnp.bfloat16
HIGHEST = lax.Precision.HIGHEST

EPS = 1e-6
LANE = 128
SUBLANE = 8
BF16_SUBLANE = 16
SSD_HEAD_DIM = 64
SSD_GROUPS = 4
SSD_STATE = 128
SSD_CHUNK = 256
ATT_HEADS = 16
ATT_KV_HEADS = 4
ATT_GROUP = ATT_HEADS // ATT_KV_HEADS
ATT_HEAD_DIM = 64
ROT_DIM = ATT_HEAD_DIM // 4
ROPE_THETA = 500000.0
MOBA_BLOCK = 256
MOBA_TOPK = 3
SEL_LANES = 16
NEG_BIG = -1e30
FAST_SPAN = 4
EXP2_CLAMP = 60.0
VMEM_LIMIT = 56 * 1024 * 1024

NT_DIMS = (((1,), (1,)), ((), ()))


def _cparams(*sem):
    return pltpu.CompilerParams(dimension_semantics=sem, vmem_limit_bytes=VMEM_LIMIT)


def _silu(x):
    return x * jax.nn.sigmoid(x)


def _softplus(x):
    return jnp.maximum(x, 0.0) + jnp.log1p(jnp.exp(-jnp.abs(x)))


def _norm_mod(x, nw, shift, scale):
    ms = jnp.mean(x * x, axis=-1, keepdims=True)
    xn = x * lax.rsqrt(ms + EPS) * nw
    return xn * (1.0 + scale) + shift


def _resident(shape):
    nd = len(shape)
    return pl.BlockSpec(shape, lambda *_: (0,) * nd, pipeline_mode=pl.Buffered(1))


def _ada_kernel(c_ref, w_ref, b_ref, o_ref):
    s = _silu(c_ref[...]).astype(BF16)
    o_ref[0] = jnp.dot(s, w_ref[0].astype(BF16), preferred_element_type=F32) + b_ref[0]


def _ada(c_all, w_ada, b_ada):
    rows, d = c_all.shape
    n_layers, _, n6 = w_ada.shape
    tn = 1024
    return pl.pallas_call(
        _ada_kernel,
        grid=(n_layers, n6 // tn),
        in_specs=[pl.BlockSpec((rows, d), lambda l, j: (0, 0)),
                  pl.BlockSpec((1, d, tn), lambda l, j: (l, 0, j)),
                  pl.BlockSpec((1, 1, tn), lambda l, j: (l, 0, j))],
        out_specs=pl.BlockSpec((1, rows, tn), lambda l, j: (l, 0, j)),
        out_shape=jax.ShapeDtypeStruct((n_layers, rows, n6), F32),
        compiler_params=_cparams("parallel", "parallel"),
        name="adaln",
    )(c_all, w_ada, b_ada.reshape(n_layers, 1, n6))


def _mod_spec(mod, k, rows_per_group, tm):
    _, r, n6 = mod.shape
    d = n6 // 6
    tiles_per_group = max(rows_per_group // tm, 1)
    return pl.BlockSpec((1, r, d), lambda i, *_: (i // tiles_per_group, 0, k))


def _nm_matmul_kernel(x_ref, nw_ref, sh_ref, sc_ref, w_ref, w2_ref, o_ref, o2_ref, h_scr):
    @pl.when(pl.program_id(1) == 0)
    def _():
        hb = _norm_mod(x_ref[...], nw_ref[...], sh_ref[0], sc_ref[0]).astype(BF16)
        h_scr[...] = hb
        o2_ref[...] = jnp.dot(hb, w2_ref[...], preferred_element_type=F32)

    o_ref[...] = jnp.dot(h_scr[...], w_ref[...], preferred_element_type=F32)


def _nm_matmul(x, nw, mod, k_shift, k_scale, w, w2, rows_per_group, tm, tn):
    t, d = x.shape
    n = w.shape[1]
    n2 = w2.shape[1]
    return pl.pallas_call(
        _nm_matmul_kernel,
        grid=(t // tm, n // tn),
        in_specs=[pl.BlockSpec((tm, d), lambda i, j: (i, 0)),
                  pl.BlockSpec((1, d), lambda i, j: (0, 0)),
                  _mod_spec(mod, k_shift, rows_per_group, tm),
                  _mod_spec(mod, k_scale, rows_per_group, tm),
                  pl.BlockSpec((d, tn), lambda i, j: (0, j)),
                  pl.BlockSpec((d, n2), lambda i, j: (0, 0))],
        out_specs=[pl.BlockSpec((tm, tn), lambda i, j: (i, j)),
                   pl.BlockSpec((tm, n2), lambda i, j: (i, 0))],
        out_shape=[jax.ShapeDtypeStruct((t, n), F32), jax.ShapeDtypeStruct((t, n2), F32)],
        scratch_shapes=[pltpu.VMEM((tm, d), BF16)],
        compiler_params=_cparams("parallel", "arbitrary"),
        name="in_proj",
    )(x, nw, mod, mod, w, w2)


def _mm_res_kernel(a_ref, w_ref, res_ref, g_ref, o_ref):
    o_ref[...] = res_ref[...] + g_ref[0] * jnp.dot(a_ref[...], w_ref[...], preferred_element_type=F32)


def _mm_res(a, w, res, mod, k_gate, rows_per_group, tm):
    t, kdim = a.shape
    d = w.shape[1]
    return pl.pallas_call(
        _mm_res_kernel,
        grid=(t // tm,),
        in_specs=[pl.BlockSpec((tm, kdim), lambda i: (i, 0)),
                  _resident((kdim, d)),
                  pl.BlockSpec((tm, d), lambda i: (i, 0)),
                  _mod_spec(mod, k_gate, rows_per_group, tm)],
        out_specs=pl.BlockSpec((tm, d), lambda i: (i, 0)),
        out_shape=jax.ShapeDtypeStruct((t, d), F32),
        compiler_params=_cparams("parallel"),
        name="proj_residual",
    )(a, w, res, mod)


def _mlp_kernel(x_ref, nw_ref, sh_ref, sc_ref, g_ref, wu_ref, wd_ref, fw_ref, o_ref, *, final_norm, tf):
    x = x_ref[...]
    hb = _norm_mod(x, nw_ref[...], sh_ref[0], sc_ref[0]).astype(BF16)
    acc = jnp.zeros(x.shape, F32)
    for f in range(wu_ref.shape[1] // tf):
        a = jnp.maximum(jnp.dot(hb, wu_ref[:, f * tf:(f + 1) * tf], preferred_element_type=F32), 0.0)
        acc = acc + jnp.dot((a * a).astype(BF16), wd_ref[f * tf:(f + 1) * tf, :], preferred_element_type=F32)
    y = x + g_ref[0] * acc
    if final_norm:
        ms = jnp.mean(y * y, axis=-1, keepdims=True)
        y = y * lax.rsqrt(ms + EPS) * fw_ref[...]
    o_ref[...] = y


def _mlp(x, nw, mod, w_up, w_down, fw, rows_per_group, tm, final_norm):
    t, d = x.shape
    dff = w_up.shape[1]
    return pl.pallas_call(
        functools.partial(_mlp_kernel, final_norm=final_norm, tf=1024),
        grid=(t // tm,),
        in_specs=[pl.BlockSpec((tm, d), lambda i: (i, 0)),
                  pl.BlockSpec((1, d), lambda i: (0, 0)),
                  _mod_spec(mod, 3, rows_per_group, tm),
                  _mod_spec(mod, 4, rows_per_group, tm),
                  _mod_spec(mod, 5, rows_per_group, tm),
                  _resident((d, dff)),
                  _resident((dff, d)),
                  pl.BlockSpec((1, d), lambda i: (0, 0))],
        out_specs=pl.BlockSpec((tm, d), lambda i: (i, 0)),
        out_shape=jax.ShapeDtypeStruct((t, d), F32),
        compiler_params=_cparams("parallel"),
        name="mlp",
    )(x, nw, mod, mod, mod, w_up, w_down, fw)


def _shift_rows(cur, tail, k):
    rolled = pltpu.roll(cur, k, axis=0)
    row = lax.broadcasted_iota(jnp.int32, tail.shape, 0)
    top = jnp.where(row < k, pltpu.roll(tail, k, axis=0), rolled[:SUBLANE])
    return jnp.concatenate([top, rolled[SUBLANE:]], axis=0)


def _causal_conv(cur, tail, w):
    kk = w.shape[0]
    out = cur * w[kk - 1:kk, :]
    for s in range(1, kk):
        out = out + _shift_rows(cur, tail, s) * w[kk - 1 - s:kk - s, :]
    return out

def _expand_heads(m, h0, n_cols):
    rows = m.shape[0]
    lane = lax.broadcasted_iota(jnp.int32, (rows, LANE), 1)
    cols = []
    for c in range(n_cols):
        lo = jnp.broadcast_to(m[:, h0 + 2 * c:h0 + 2 * c + 1], (rows, LANE))
        hi = jnp.broadcast_to(m[:, h0 + 2 * c + 1:h0 + 2 * c + 2], (rows, LANE))
        cols.append(jnp.where(lane < SSD_HEAD_DIM, lo, hi))
    return jnp.concatenate(cols, axis=1)


def _ssd_prompt_kernel(scb_ref, scc_ref, scx_ref, z0_ref, z1_ref, x0_ref, x1_ref, bc_ref, dt_ref,
                       scw_ref, cw_ref, cb_ref, dtb_ref, alog_ref, dexp_ref, nw_ref,
                       y_ref, scst_ref, cvst_ref, ssm_ref,
                       utail, xtail, s_scr):
    c = pl.program_id(1)
    q = scb_ref.shape[0]
    dsc = scb_ref.shape[1]
    gw = 2 * x0_ref.shape[1] // SSD_GROUPS
    hpg = gw // SSD_HEAD_DIM

    @pl.when(c == 0)
    def _():
        utail[...] = jnp.zeros(utail.shape, F32)
        xtail[...] = jnp.zeros(xtail.shape, F32)
        s_scr[...] = jnp.zeros(s_scr.shape, F32)

    u = scc_ref[...] * scx_ref[...]
    y_sc = scb_ref[...] * _causal_conv(u, utail[...], scw_ref[...])
    utail[...] = u[q - SUBLANE:, :]
    y_ref[:, :dsc] = y_sc.astype(BF16)

    bc = bc_ref[...]
    n_bc = bc.shape[1]
    o_bc = 2 * x0_ref.shape[1]
    bc_cols = slice(o_bc, o_bc + n_bc)
    bc_c = _silu(_causal_conv(bc, xtail[:, bc_cols], cw_ref[:, bc_cols]) + cb_ref[:, bc_cols])
    xtail[:, bc_cols] = bc[q - SUBLANE:, :]
    half = n_bc // 2

    dt = _softplus(dt_ref[...] + dtb_ref[...])
    a = dt * (-jnp.exp(alog_ref[...]))
    ri = lax.broadcasted_iota(jnp.int32, (q, q), 0)
    ci = lax.broadcasted_iota(jnp.int32, (q, q), 1)
    tril = ri >= ci
    a_cs = jnp.dot(tril.astype(F32), a, precision=HIGHEST, preferred_element_type=F32) * math.log2(math.e)
    a_cs_t = a_cs.T
    quad = 4 * SSD_HEAD_DIM
    lane_head = lax.broadcasted_iota(jnp.int32, (q, quad), 1) // SSD_HEAD_DIM

    for g in range(SSD_GROUPS):
        x_ref = x0_ref if g < SSD_GROUPS // 2 else x1_ref
        z_ref = z0_ref if g < SSD_GROUPS // 2 else z1_ref
        lo = (g % (SSD_GROUPS // 2)) * gw
        go = g * gw
        xg = x_ref[:, lo:lo + gw]
        xs = _silu(_causal_conv(xg, xtail[:, go:go + gw], cw_ref[:, go:go + gw]) + cb_ref[:, go:go + gw])
        xtail[:, go:go + gw] = xg[q - SUBLANE:, :]

        n_cols = gw // LANE
        dt_x = _expand_heads(dt, g * hpg, n_cols)
        a_x = _expand_heads(a_cs, g * hpg, n_cols)
        ea = jnp.exp2(a_x)
        dte = jnp.exp2(a_x[q - 1:q, :] - a_x)
        xdt = xs * dt_x
        xdt_b = xdt.astype(BF16)

        bg = bc_c[:, g * SSD_STATE:(g + 1) * SSD_STATE]
        cg_b = bc_c[:, half + g * SSD_STATE:half + (g + 1) * SSD_STATE].astype(BF16)
        cb_mat = lax.dot_general(cg_b, bg.astype(BF16), NT_DIMS, preferred_element_type=F32)
        s_g = s_scr[:, go:go + gw]
        y_off = jnp.dot(cg_b, s_g.astype(BF16), preferred_element_type=F32) * ea

        y_quads = []
        for qd in range(gw // quad):
            xq = xdt[:, qd * quad:(qd + 1) * quad]
            acc = None
            for i in range(4):
                h = g * hpg + qd * 4 + i
                seg = a_cs[:, h:h + 1] - a_cs_t[h:h + 1, :]
                dec = jnp.exp2(jnp.where(tril, seg, -jnp.inf))
                w_h = (cb_mat * dec).astype(BF16)
                xm = jnp.where(lane_head == i, xq, 0.0).astype(BF16)
                d = jnp.dot(w_h, xm, preferred_element_type=F32)
                acc = d if acc is None else acc + d
            y_quads.append(acc)
        y = jnp.concatenate(y_quads, axis=1) + y_off + dexp_ref[:, go:go + gw] * xs

        zg = z_ref[:, lo:lo + gw]
        y = y * _silu(zg)
        ms = jnp.mean(y * y, axis=-1, keepdims=True)
        y = y * lax.rsqrt(ms + EPS) * nw_ref[:, go:go + gw]
        y_ref[:, dsc + go:dsc + go + gw] = y.astype(BF16)

        s_new = s_g * ea[q - 1:q, :] + jnp.dot(bg.T.astype(BF16), (xdt * dte).astype(BF16),
                                               preferred_element_type=F32)
        s_scr[:, go:go + gw] = s_new

    @pl.when(c == pl.num_programs(1) - 1)
    def _():
        scst_ref[0] = utail[...]
        cvst_ref[0] = xtail[...]
        ssm_ref[0] = s_scr[...].T


def _ssd_prompt(proj, dt_raw, scw, cw, cb, dtb, alog, dexp, nw, bp, seq):
    q = min(SSD_CHUNK, seq)
    nc = seq // q
    dsc = scw.shape[1]
    d_inner = dexp.shape[1]
    conv_dim = cw.shape[1]
    bw = dsc

    def col(kb):
        return pl.BlockSpec((q, bw), lambda b, c: (b * nc + c, kb))

    n_z = d_inner // bw
    specs = [col(0), col(1), col(2)]
    specs += [col(3 + i) for i in range(n_z)]
    specs += [col(3 + n_z + i) for i in range(n_z)]
    specs += [col(3 + 2 * n_z)]
    specs += [pl.BlockSpec((q, LANE), lambda b, c: (b * nc + c, 0))]
    for arr in (scw, cw, cb, dtb, alog, dexp, nw):
        specs.append(pl.BlockSpec(arr.shape, lambda b, c: (0, 0)))
    return pl.pallas_call(
        _ssd_prompt_kernel,
        grid=(bp, nc),
        in_specs=specs,
        out_specs=[pl.BlockSpec((q, dsc + d_inner), lambda b, c: (b * nc + c, 0)),
                   pl.BlockSpec((1, SUBLANE, dsc), lambda b, c: (b, 0, 0)),
                   pl.BlockSpec((1, SUBLANE, conv_dim), lambda b, c: (b, 0, 0)),
                   pl.BlockSpec((1, d_inner, SSD_STATE), lambda b, c: (b, 0, 0))],
        out_shape=[jax.ShapeDtypeStruct((bp * seq, dsc + d_inner), BF16),
                   jax.ShapeDtypeStruct((bp, SUBLANE, dsc), F32),
                   jax.ShapeDtypeStruct((bp, SUBLANE, conv_dim), F32),
                   jax.ShapeDtypeStruct((bp, d_inner, SSD_STATE), F32)],
        scratch_shapes=[pltpu.VMEM((SUBLANE, dsc), F32),
                        pltpu.VMEM((SUBLANE, conv_dim), F32),
                        pltpu.VMEM((SSD_STATE, d_inner), F32)],
        compiler_params=_cparams("parallel", "arbitrary"),
        name="ssd_prompt",
    )(*([proj] * (4 + 2 * n_z)), dt_raw, scw, cw, cb, dtb, alog, dexp, nw)


def _even_sample_prep_kernel(proj_ref, dt_ref, scst_ref, cvst_ref, scw_ref, cw_ref, cb_ref, dtb_ref, alog_ref,
                             ysc_ref, scn_ref, cvn_ref, xs_ref, b_ref, c_ref, xdt_t_ref, dec_ref):
    dsc = scw_ref.shape[1]
    conv_dim = cw_ref.shape[1]
    d_inner = xs_ref.shape[1]
    n_heads = d_inner // SSD_HEAD_DIM
    scb = proj_ref[:, :dsc]
    u = proj_ref[:, dsc:2 * dsc] * proj_ref[:, 2 * dsc:3 * dsc]
    xbc = proj_ref[:, 3 * dsc + d_inner:3 * dsc + d_inner + conv_dim]
    b0 = scst_ref[:, :dsc]
    b1 = scst_ref[:, dsc:]
    ysc_ref[...] = scb * (b0 * scw_ref[0:1, :] + b1 * scw_ref[1:2, :] + u * scw_ref[2:3, :])
    scn_ref[:, :dsc] = b1
    scn_ref[:, dsc:] = u

    c0 = cvst_ref[:, :conv_dim]
    c1 = cvst_ref[:, conv_dim:2 * conv_dim]
    c2 = cvst_ref[:, 2 * conv_dim:]
    act = _silu(c0 * cw_ref[0:1, :] + c1 * cw_ref[1:2, :] + c2 * cw_ref[2:3, :] + xbc * cw_ref[3:4, :] + cb_ref[...])
    cvn_ref[:, :conv_dim] = c1
    cvn_ref[:, conv_dim:2 * conv_dim] = c2
    cvn_ref[:, 2 * conv_dim:] = xbc
    xs = act[:, :d_inner]
    n_bc = (conv_dim - d_inner) // 2
    xs_ref[...] = xs
    b_ref[...] = act[:, d_inner:d_inner + n_bc]
    c_ref[...] = act[:, d_inner + n_bc:]

    dt = _softplus(dt_ref[...] + dtb_ref[...])
    dec_ref[...] = jnp.exp(dt * (-jnp.exp(alog_ref[...])))
    xdt_t_ref[...] = (xs * _expand_heads(dt, 0, n_heads // 2)).T


def _even_sample_prep(proj, dt_raw, scst, cvst, scw, cw, cb, dtb, alog, d_inner):
    bs = proj.shape[0]
    dsc = scw.shape[1]
    conv_dim = cw.shape[1]
    n_bc = (conv_dim - d_inner) // 2
    shapes = [(bs, dsc), (bs, 2 * dsc), (bs, 3 * conv_dim), (bs, d_inner), (bs, n_bc), (bs, n_bc),
              (d_inner, bs), (bs, LANE)]
    return pl.pallas_call(
        _even_sample_prep_kernel,
        out_shape=[jax.ShapeDtypeStruct(s, F32) for s in shapes],
        compiler_params=pltpu.CompilerParams(vmem_limit_bytes=VMEM_LIMIT),
        name="even_sample_prep",
    )(proj, dt_raw, scst, cvst, scw, cw, cb, dtb, alog)


def _ssm_sample_kernel(dec_ref, h0_ref, xt_ref, b_ref, c_ref, hn_ref, yt_ref):
    h = pl.program_id(0)
    bs = h0_ref.shape[0]
    p = h0_ref.shape[2]
    n = h0_ref.shape[3]
    lane = lax.broadcasted_iota(jnp.int32, (p, bs), 1)
    xt = xt_ref[...]
    ones = jnp.ones((n, bs), BF16)
    yt = jnp.zeros((p, bs), F32)
    for b in range(bs):
        xcol = jnp.broadcast_to(xt[:, b:b + 1], (p, n))
        hn = h0_ref[b, 0] * dec_ref[b, h] + xcol * b_ref[b:b + 1, :]
        hn_ref[b, 0] = hn
        ysum = jnp.zeros((p, bs), F32)
        rest = hn * c_ref[b:b + 1, :]
        for _ in range(2):
            piece = rest.astype(BF16)
            ysum = ysum + jnp.dot(piece, ones, preferred_element_type=F32)
            rest = rest - piece.astype(F32)
        yt = jnp.where(lane == b, ysum, yt)
    yt_ref[...] = yt


def _ssm_sample(h0, xdt_t, dec, bm, cm):
    bs, n_heads, p, n = h0.shape
    hpg = n_heads // SSD_GROUPS
    return pl.pallas_call(
        _ssm_sample_kernel,
        grid=(n_heads,),
        in_specs=[pl.BlockSpec(memory_space=pltpu.SMEM),
                  pl.BlockSpec((bs, 1, p, n), lambda h: (0, h, 0, 0)),
                  pl.BlockSpec((p, bs), lambda h: (h, 0)),
                  pl.BlockSpec((bs, n), lambda h: (0, h // hpg)),
                  pl.BlockSpec((bs, n), lambda h: (0, h // hpg))],
        out_specs=[pl.BlockSpec((bs, 1, p, n), lambda h: (0, h, 0, 0)),
                   pl.BlockSpec((p, bs), lambda h: (h, 0))],
        out_shape=[jax.ShapeDtypeStruct(h0.shape, F32), jax.ShapeDtypeStruct((n_heads * p, bs), F32)],
        compiler_params=_cparams("parallel"),
        name="ssm_sample",
    )(dec, h0, xdt_t, bm, cm)


def _even_out_sample_kernel(yt_ref, xs_ref, proj_ref, ysc_ref, dexp_ref, nw_ref, w_ref, x_ref, g_ref, o_ref):
    d_inner = xs_ref.shape[1]
    dsc = ysc_ref.shape[1]
    gw = d_inner // SSD_GROUPS
    y = yt_ref[...].T + dexp_ref[...] * xs_ref[...]
    y = y * _silu(proj_ref[:, 3 * dsc:3 * dsc + d_inner])
    parts = []
    for g in range(SSD_GROUPS):
        yg = y[:, g * gw:(g + 1) * gw]
        ms = jnp.mean(yg * yg, axis=-1, keepdims=True)
        parts.append(yg * lax.rsqrt(ms + EPS))
    y_ssd = jnp.concatenate(parts, axis=1) * nw_ref[...]
    cat = jnp.concatenate([ysc_ref[...], y_ssd], axis=1).astype(BF16)
    o_ref[...] = x_ref[...] + g_ref[0] * jnp.dot(cat, w_ref[...], preferred_element_type=F32)


def _even_out_sample(y_t, xs, proj, ysc, dexp, nw, w_out, x, mod):
    bs, d = x.shape
    full = lambda a: pl.BlockSpec(a.shape, lambda i: (0,) * a.ndim)
    return pl.pallas_call(
        _even_out_sample_kernel,
        grid=(1,),
        in_specs=[full(y_t), full(xs), full(proj), full(ysc), full(dexp), full(nw), full(w_out), full(x),
                  pl.BlockSpec((1, bs, d), lambda i: (0, 0, 2))],
        out_specs=pl.BlockSpec((bs, d), lambda i: (0, 0)),
        out_shape=jax.ShapeDtypeStruct((bs, d), F32),
        compiler_params=_cparams("arbitrary"),
        name="even_out_sample",
    )(y_t, xs, proj, ysc, dexp, nw, w_out, x, mod)


def _rope_table_kernel(ca_ref, cb_ref, cc_ref, *, pos0, stride):
    rows = ca_ref.shape[0]
    row = lax.broadcasted_iota(jnp.int32, (rows, LANE), 0) + pl.program_id(0) * rows
    d = lax.broadcasted_iota(jnp.int32, (rows, LANE), 1) % ATT_HEAD_DIM
    half = ROT_DIM // 2
    pos = (pos0 + row * stride).astype(F32)
    inv = jnp.power(jnp.full((rows, LANE), ROPE_THETA, F32), -((d % half).astype(F32) * 2.0 / ROT_DIM))
    ang = pos * inv
    cs = jnp.cos(ang)
    sn = jnp.sin(ang)
    ca_ref[...] = jnp.where(d < ROT_DIM, cs, 1.0)
    cb_ref[...] = jnp.where(d < half, -sn, 0.0)
    cc_ref[...] = jnp.where(d < half, 0.0, jnp.where(d < ROT_DIM, sn, 0.0))


def _rope_tables(rows, pos0, stride):
    tr = min(rows, 512)
    spec = pl.BlockSpec((tr, LANE), lambda i: (i, 0))
    return pl.pallas_call(
        functools.partial(_rope_table_kernel, pos0=pos0, stride=stride),
        grid=(rows // tr,),
        out_specs=[spec, spec, spec],
        out_shape=[jax.ShapeDtypeStruct((rows, LANE), F32)] * 3,
        compiler_params=_cparams("parallel"),
        name="rope_tables",
    )()


def _rope(t, ca, cb, cc):
    cols = []
    for c in range(t.shape[1] // LANE):
        s = t[:, c * LANE:(c + 1) * LANE]
        cols.append(s * ca + pltpu.roll(s, LANE - ROT_DIM // 2, axis=1) * cb + pltpu.roll(s, ROT_DIM // 2, axis=1) * cc)
    return jnp.concatenate(cols, axis=1)


def _sel_keys(k, j):
    lane = lax.broadcasted_iota(jnp.int32, (k.shape[0], LANE), 1)
    onehot = jnp.where(lane == j, 1.0, 0.0)
    in_sel = lane < SEL_LANES
    k_lo = jnp.concatenate([jnp.where(in_sel, onehot, k[:, :LANE]), k[:, LANE:]], axis=1)
    k_hi = jnp.concatenate([k[:, :LANE], jnp.where(in_sel, onehot, k[:, LANE:])], axis=1)
    return k_lo.astype(BF16), k_hi.astype(BF16)


def _qkv_kernel(x_ref, nw_ref, sh_ref, sc_ref, w_ref, ca_ref, cb_ref, cc_ref, q_ref, k_ref, v_ref, *blk_refs,
                blocks_per_seq):
    nq = q_ref.shape[1]
    nk = (w_ref.shape[1] - nq) // 2
    hb = _norm_mod(x_ref[...], nw_ref[...], sh_ref[0], sc_ref[0]).astype(BF16)
    qkv = jnp.dot(hb, w_ref[...], preferred_element_type=F32)
    ca, cb, cc = ca_ref[...], cb_ref[...], cc_ref[...]
    q_ref[...] = _rope(qkv[:, :nq], ca, cb, cc)
    k = _rope(qkv[:, nq:nq + nk], ca, cb, cc)
    v = qkv[:, nq + nk:]
    if blocks_per_seq:
        km_ref, klo_ref, khi_ref, vt_ref = blk_refs
        vt = v.T
        k_ref[0] = k.T
        v_ref[0] = vt
        km_ref[0] = jnp.mean(k, axis=0, keepdims=True)
        klo_ref[...], khi_ref[...] = _sel_keys(k, pl.program_id(0) % blocks_per_seq)
        hd = ATT_HEAD_DIM
        ones_row = jnp.where(lax.broadcasted_iota(jnp.int32, (BF16_SUBLANE, vt.shape[1]), 0) == 0, 1.0, 0.0)
        vt_ref[0] = jnp.concatenate([piece for h in range(nk // hd) for piece in (vt[h * hd:(h + 1) * hd], ones_row)],
                                    axis=0).astype(BF16)
    else:
        k_ref[...] = k
        v_ref[...] = v


def _qkv(x, nw, mod, w, tabs, rows_per_group, tm, per_block):
    t, d = x.shape
    nq = ATT_HEADS * ATT_HEAD_DIM
    nk = ATT_KV_HEADS * ATT_HEAD_DIM
    tab_tiles = tabs[0].shape[0] // tm
    tab = pl.BlockSpec((tm, LANE), lambda i: (i % tab_tiles, 0))
    row = lambda n: pl.BlockSpec((tm, n), lambda i: (i, 0))
    if per_block:
        bps = rows_per_group // tm
        kv_t = pl.BlockSpec((1, nk, tm), lambda i: (i // bps, 0, i % bps))
        v_rows = ATT_KV_HEADS * (ATT_HEAD_DIM + BF16_SUBLANE)
        out_specs = [row(nq), kv_t, kv_t, pl.BlockSpec((1, 1, nk), lambda i: (i, 0, 0)), row(nk), row(nk),
                     pl.BlockSpec((1, v_rows, tm), lambda i: (i, 0, 0))]
        out_shape = [jax.ShapeDtypeStruct((t, nq), F32),
                     jax.ShapeDtypeStruct((t // rows_per_group, nk, rows_per_group), F32),
                     jax.ShapeDtypeStruct((t // rows_per_group, nk, rows_per_group), F32),
                     jax.ShapeDtypeStruct((t // tm, 1, nk), F32), jax.ShapeDtypeStruct((t, nk), BF16),
                     jax.ShapeDtypeStruct((t, nk), BF16), jax.ShapeDtypeStruct((t // tm, v_rows, tm), BF16)]
    else:
        out_specs = [row(nq), row(nk), row(nk)]
        out_shape = [jax.ShapeDtypeStruct((t, nq), F32), jax.ShapeDtypeStruct((t, nk), F32),
                     jax.ShapeDtypeStruct((t, nk), F32)]
    return pl.pallas_call(
        functools.partial(_qkv_kernel, blocks_per_seq=rows_per_group // tm if per_block else 0),
        grid=(t // tm,),
        in_specs=[row(d), pl.BlockSpec((1, d), lambda i: (0, 0)),
                  _mod_spec(mod, 0, rows_per_group, tm), _mod_spec(mod, 1, rows_per_group, tm),
                  _resident(w.shape), tab, tab, tab],
        out_specs=out_specs,
        out_shape=out_shape,
        compiler_params=_cparams("parallel"),
        name="qkv_rope",
    )(x, nw, mod, mod, w, *tabs)


def _select_bias(scores, n_past, n_rounds):
    lane = lax.broadcasted_iota(jnp.int32, scores.shape, 1)
    lanef = lane.astype(F32)
    s = jnp.where(lane < n_past, scores, -jnp.inf)
    bias = jnp.full(scores.shape, NEG_BIG, F32)
    for r in range(n_rounds):
        mx = jnp.max(s, axis=1, keepdims=True)
        idx = jnp.min(jnp.where(s == mx, lanef, float(LANE)), axis=1, keepdims=True)
        bias = jnp.where(lanef == jnp.where(n_past > r, idx, -1.0), 0.0, bias)
        s = jnp.where(lanef == idx, -jnp.inf, s)
    return bias


def _select_bias_t(scores, n_past, n_rounds):
    row = lax.broadcasted_iota(jnp.int32, scores.shape, 0)
    rowf = row.astype(F32)
    s = jnp.where(row < n_past, scores, -jnp.inf)
    bias = jnp.full(scores.shape, NEG_BIG, F32)
    for r in range(n_rounds):
        mx = jnp.max(s, axis=0, keepdims=True)
        idx = jnp.min(jnp.where(s == mx, rowf, float(SEL_LANES)), axis=0, keepdims=True)
        bias = jnp.where(rowf == jnp.where(n_past > r, idx, -1.0), 0.0, bias)
        s = jnp.where(rowf == idx, -jnp.inf, s)
    return jnp.where(row == n_past, 0.0, bias)


def _moba_prompt_kernel(q_ref, klo_ref, khi_ref, vt_ref, km_ref, o_ref, qa_scr, m_scr, over_scr, acc_scr):
    qi = pl.program_id(1)
    blk = q_ref.shape[0]
    kvw = klo_ref.shape[1]
    hd = ATT_HEAD_DIM
    n_q = ATT_GROUP * blk
    kmean = km_ref[0]
    km_hi = kmean.astype(BF16)
    km_lo = (kmean - km_hi.astype(F32)).astype(BF16)
    n_rounds = min(MOBA_TOPK, klo_ref.shape[0] // blk)

    q_t = q_ref[...].T
    for h in range(ATT_KV_HEADS):
        qh_b = jnp.concatenate([q_t[g * kvw + h * hd:g * kvw + (h + 1) * hd, :] for g in range(ATT_GROUP)],
                               axis=1).astype(BF16)
        scores_t = (jnp.dot(km_hi[:, h * hd:(h + 1) * hd], qh_b, preferred_element_type=F32)
                    + jnp.dot(km_lo[:, h * hd:(h + 1) * hd], qh_b, preferred_element_type=F32))
        bias_t = _select_bias_t(scores_t, qi, n_rounds)
        sel_row = LANE if h == 0 else 0
        qa_scr[h] = jnp.zeros((kvw, n_q), BF16)
        qa_scr[h, h * hd:(h + 1) * hd, :] = qh_b
        qa_scr[h, sel_row:sel_row + SEL_LANES, :] = bias_t.astype(BF16)

    v_rows = vt_ref.shape[1] // ATT_KV_HEADS
    key_row = lax.broadcasted_iota(jnp.int32, (blk, n_q), 0)
    q_pos = lax.broadcasted_iota(jnp.int32, (blk, n_q), 1) % blk

    def scores(h, j):
        off = pl.multiple_of(j * blk, blk)
        ka = (khi_ref if h == 0 else klo_ref)[pl.ds(off, blk), :]
        return jnp.dot(ka, qa_scr[h], preferred_element_type=F32)

    def softmax(h, s, first):
        if first:
            s = jnp.where(key_row <= q_pos, s, NEG_BIG)
        mx = jnp.max(s, axis=0, keepdims=True)
        if first:
            m_new, alpha = mx, None
        else:
            m_old = m_scr[h, 0:1, :]
            m_new = jnp.maximum(m_old, mx)
            alpha = jnp.exp2(m_old - m_new)
        m_scr[h] = jnp.broadcast_to(m_new, (SUBLANE, n_q))
        return jnp.exp2((s - m_new).astype(BF16)), alpha

    def values(h, j, p, alpha):
        vt_h = vt_ref[j, h * v_rows:(h + 1) * v_rows, :]
        pv = jnp.dot(vt_h, p, preferred_element_type=F32)
        acc_scr[h] = pv if alpha is None else acc_scr[h] * alpha + pv

    def block_step(j, first):
        s_next = scores(0, j)
        for h in range(ATT_KV_HEADS):
            s = s_next
            if h + 1 < ATT_KV_HEADS:
                s_next = scores(h + 1, j)
            p, alpha = softmax(h, s, first)
            values(h, j, p, alpha)

    def finalize():
        for g in range(ATT_GROUP):
            cols = slice(g * blk, (g + 1) * blk)
            o_t = jnp.concatenate([acc_scr[h, :hd, cols] / acc_scr[h, hd:hd + 1, cols]
                                   for h in range(ATT_KV_HEADS)], axis=0)
            o_ref[:, g * kvw:(g + 1) * kvw] = o_t.T.astype(BF16)

    block_step(qi, True)

    for h in range(ATT_KV_HEADS):
        over_scr[h] = jnp.full((SUBLANE, n_q), NEG_BIG, F32)

    def fast_step(j, nb):
        off = pl.multiple_of(j * blk, blk)

        def span_scores(h):
            ka = (khi_ref if h == 0 else klo_ref)[pl.ds(off, nb * blk), :]
            return jnp.dot(ka, qa_scr[h], preferred_element_type=F32)

        s_next = span_scores(0)
        for h in range(ATT_KV_HEADS):
            s = s_next
            if h + 1 < ATT_KV_HEADS:
                s_next = span_scores(h + 1)
            d = s - m_scr[h, 0:1, :]
            over_scr[h] = jnp.maximum(over_scr[h], jnp.broadcast_to(jnp.max(d, axis=0, keepdims=True), (SUBLANE, n_q)))
            p = jnp.exp2(jnp.minimum(d, EXP2_CLAMP)).astype(BF16)
            rows = slice(h * v_rows, (h + 1) * v_rows)
            vt_h = jnp.concatenate([vt_ref[j + i, rows, :] for i in range(nb)], axis=1)
            acc_scr[h] = acc_scr[h] + jnp.dot(vt_h, p, preferred_element_type=F32)

    def fast_body(t, carry):
        fast_step(FAST_SPAN * t, FAST_SPAN)
        return carry

    lax.fori_loop(0, qi // FAST_SPAN, fast_body, 0)
    span = FAST_SPAN // 2
    while span:
        @pl.when(qi % (2 * span) >= span)
        def _(span=span):
            fast_step(qi - qi % (2 * span), span)
        span //= 2

    finalize()

    @pl.when(jnp.max(over_scr[...]) > EXP2_CLAMP)
    def _():
        block_step(qi, True)

        def body(j, carry):
            block_step(j, False)
            return carry

        lax.fori_loop(0, qi, body, 0)
        finalize()


def _moba_prompt(q, k_lo, k_hi, v_t, kmean, bp, seq):
    blk = MOBA_BLOCK
    nq = seq // blk
    kvw = k_lo.shape[1]
    n_q = ATT_GROUP * blk
    v_rows = v_t.shape[1]
    return pl.pallas_call(
        _moba_prompt_kernel,
        grid=(bp, nq),
        in_specs=[pl.BlockSpec((blk, q.shape[1]), lambda b, i: (b * nq + i, 0)),
                  pl.BlockSpec((seq, kvw), lambda b, i: (b, 0)),
                  pl.BlockSpec((seq, kvw), lambda b, i: (b, 0)),
                  pl.BlockSpec((nq, v_rows, blk), lambda b, i: (b, 0, 0)),
                  pl.BlockSpec((1, SEL_LANES, kvw), lambda b, i: (b, 0, 0))],
        out_specs=pl.BlockSpec((blk, q.shape[1]), lambda b, i: (b * nq + i, 0)),
        out_shape=jax.ShapeDtypeStruct(q.shape, BF16),
        scratch_shapes=[pltpu.VMEM((ATT_KV_HEADS, kvw, n_q), BF16),
                        pltpu.VMEM((ATT_KV_HEADS, SUBLANE, n_q), F32),
                        pltpu.VMEM((ATT_KV_HEADS, SUBLANE, n_q), F32),
                        pltpu.VMEM((ATT_KV_HEADS, v_rows // ATT_KV_HEADS, n_q), F32)],
        compiler_params=_cparams("parallel", "arbitrary"),
        name="moba_prompt",
    )(q, k_lo, k_hi, v_t, kmean)


def _moba_sample_kernel(pt_ref, q_ref, kn_ref, vn_ref, avg_ref, *rest, n_pages):
    k_pages = rest[:n_pages]
    v_pages = rest[n_pages:2 * n_pages]
    o_ref = rest[2 * n_pages]
    page = k_pages[0].shape[3]
    n_past = n_pages * page // MOBA_BLOCK
    q_rows = q_ref[0]
    kn_rows = kn_ref[0]
    vn_rows = vn_ref[0]
    n_rows = q_rows.shape[0]
    q_b = q_rows.astype(BF16)

    heads = range(ATT_KV_HEADS)
    rows = [slice(h * SUBLANE, (h + 1) * SUBLANE) for h in heads]
    kt_bs = [jnp.concatenate([r[0, h] for r in k_pages], axis=1).astype(BF16) for h in heads]
    logits = jnp.concatenate([jnp.dot(q_b[rows[h]], kt_bs[h], preferred_element_type=F32) for h in heads], axis=0)
    kmts = [jnp.dot(kt_bs[h], avg_ref[...], preferred_element_type=F32) for h in heads]
    vts =[jnp.concatenate([r[0, h] for r in v_pages], axis=1).astype(BF16) for h in heads]

    blocks = range(n_past)
    keys = [slice(j * MOBA_BLOCK, (j + 1) * MOBA_BLOCK) for j in blocks]
    mjs = [jnp.max(logits[:, keys[j]], axis=1, keepdims=True) for j in blocks]
    km_his = [km.astype(BF16) for km in kmts]
    km_los = [(kmts[h] - km_his[h].astype(F32)).astype(BF16) for h in heads]
    scores = jnp.concatenate([jnp.dot(q_b[rows[h]], km_his[h], preferred_element_type=F32)
                              + jnp.dot(q_b[rows[h]], km_los[h], preferred_element_type=F32) for h in heads], axis=0)
    pjs = [jnp.exp2(logits[:, keys[j]] - mjs[j]) for j in blocks]
    sjs = [jnp.sum(pjs[j], axis=1, keepdims=True) for j in blocks]
    pj_bs = [pjs[j].astype(BF16) for j in blocks]
    blk_out = [jnp.concatenate([lax.dot_general(pj_bs[j][rows[h]], vts[h][:, keys[j]], NT_DIMS,
                                                preferred_element_type=F32) for h in heads], axis=0)
               for j in blocks]
    bias = _select_bias(scores, n_past, min(MOBA_TOPK, n_past + 1))
    lane = lax.broadcasted_iota(jnp.int32, (n_rows, LANE), 1)
    blk_max = jnp.full((n_rows, LANE), NEG_BIG, F32)
    blk_sum = jnp.zeros((n_rows, LANE), F32)
    for j in blocks:
        blk_max = jnp.where(lane == j, mjs[j], blk_max)
        blk_sum = jnp.where(lane == j, sjs[j], blk_sum)
    l_new = jnp.sum(q_rows * kn_rows, axis=1, keepdims=True)
    chosen = blk_max + bias
    m = jnp.maximum(jnp.max(chosen, axis=1, keepdims=True), l_new)
    w = jnp.exp2(chosen - m)
    pn = jnp.exp2(l_new - m)
    denom = jnp.sum(w * blk_sum, axis=1, keepdims=True) + pn
    terms = [w[:, j:j + 1] * blk_out[j] for j in blocks] + [pn * vn_rows]
    while len(terms) > 1:
        terms = [terms[i] + terms[i + 1] for i in range(0, len(terms) - 1, 2)] + terms[len(terms) & ~1:]
    o_ref[0] = terms[0] / denom


def _moba_sample(q, kn, vn, cache_kt, cache_vt, page_table):
    bs = q.shape[0]
    n_pages = page_table.shape[1]
    _, n_kv, hd, page = cache_kt.shape
    past = n_pages * page
    n_rows = n_kv * SUBLANE
    assert ATT_GROUP <= SUBLANE

    def head_rows(t):
        return jnp.pad(t, ((0, 0), (0, 0), (0, SUBLANE - t.shape[2]), (0, 0))).reshape(bs, n_rows, hd)

    q_rows = head_rows(q.reshape(bs, ATT_GROUP, n_kv, hd).transpose(0, 2, 1, 3))
    kn_rows = jnp.broadcast_to(kn.reshape(bs, n_kv, 1, hd), (bs, n_kv, SUBLANE, hd)).reshape(bs, n_rows, hd)
    vn_rows = jnp.broadcast_to(vn.reshape(bs, n_kv, 1, hd), (bs, n_kv, SUBLANE, hd)).reshape(bs, n_rows, hd)
    one = pl.BlockSpec((1, n_rows, hd), lambda b, pt: (b, 0, 0))
    avg = jnp.where(jnp.arange(past)[:, None] // MOBA_BLOCK == jnp.arange(LANE)[None, :], 1.0 / MOBA_BLOCK, 0.0).astype(BF16)

    def page_spec(p):
        return pl.BlockSpec((1, n_kv, hd, page), lambda b, pt: (pt[b, p], 0, 0, 0))

    grid_spec = pltpu.PrefetchScalarGridSpec(
        num_scalar_prefetch=1,
        grid=(bs,),
        in_specs=[one, one, one, pl.BlockSpec((past, LANE), lambda b, pt: (0, 0))]
        + [page_spec(p) for p in range(n_pages)] * 2,
        out_specs=one,
    )
    o_rows = pl.pallas_call(
        functools.partial(_moba_sample_kernel, n_pages=n_pages),
        grid_spec=grid_spec,
        out_shape=jax.ShapeDtypeStruct((bs, n_rows, hd), F32),
        compiler_params=_cparams("parallel"),
        name="moba_sample",
    )(page_table, q_rows, kn_rows, vn_rows, avg, *([cache_kt] * n_pages), *([cache_vt] * n_pages))
    return o_rows.reshape(bs, n_kv, SUBLANE, hd)[:, :, :ATT_GROUP].transpose(0, 2, 1, 3).reshape(bs, ATT_GROUP * n_kv * hd)


def _pad_lanes(v, fill=0.0):
    return jnp.pad(v.reshape(1, -1), ((0, 0), (0, LANE - v.shape[0])), constant_values=fill)


def kernel(x_prompt, x_sample, state_sc_conv, state_ssd_conv, state_ssm, cache_k, cache_v, page_table,
           c_prompt, c_sample, norm_mix_w, norm_mlp_w, norm_final_w, w_ada, b_ada, w_in_e, sc_conv_w,
           ssd_conv_w, ssd_conv_b, ssd_dt_bias, ssd_a_log, ssd_d, ssd_norm_w, w_out_e, w_qkv, w_o, w_up, w_down):
    bp, seq, d = x_prompt.shape
    bs, dec_seq, _ = x_sample.shape
    assert dec_seq == 1 and seq % MOBA_BLOCK == 0 and seq % SSD_CHUNK == 0 and seq // MOBA_BLOCK <= SEL_LANES
    n_pages, page = page_table.shape[1], cache_k.shape[2]
    past_len = n_pages * page
    assert past_len % MOBA_BLOCK == 0 and MOBA_BLOCK % page == 0
    dsc = sc_conv_w.shape[2]
    conv_dim = ssd_conv_w.shape[2]
    n_heads = ssd_d.shape[1]
    d_inner = n_heads * SSD_HEAD_DIM
    main = 3 * dsc + d_inner + conv_dim
    kvw = ATT_KV_HEADS * ATT_HEAD_DIM
    nq = ATT_HEADS * ATT_HEAD_DIM

    xp = x_prompt.reshape(bp * seq, d)
    xs = x_sample.reshape(bs, d)

    rows = -(-(bs + bp) // SUBLANE) * SUBLANE
    c_all = jnp.pad(jnp.concatenate([c_sample, c_prompt], axis=0), ((0, rows - bs - bp), (0, 0)))
    ada = _ada(c_all, w_ada, b_ada)

    w_in = w_in_e[0][:, :main].astype(BF16)
    w_dt = jnp.pad(w_in_e[0][:, main:], ((0, 0), (0, LANE - n_heads))).astype(BF16)
    w_out = w_out_e[0].astype(BF16)
    wq = w_qkv[0][:, :nq].reshape(d, ATT_KV_HEADS, ATT_GROUP, ATT_HEAD_DIM).transpose(0, 2, 1, 3).reshape(d, nq)
    w_qkv_p = jnp.concatenate([wq * (ATT_HEAD_DIM ** -0.5 * math.log2(math.e)), w_qkv[0][:, nq:]], axis=1).astype(BF16)
    w_o_p = w_o[0].reshape(ATT_KV_HEADS, ATT_GROUP, ATT_HEAD_DIM, d).transpose(1, 0, 2, 3).reshape(nq, d).astype(BF16)
    w_up_b = w_up.astype(BF16)
    w_down_b = w_down.astype(BF16)

    scw = sc_conv_w[0]
    cw = ssd_conv_w[0]
    cb = ssd_conv_b[0].reshape(1, conv_dim)
    dtb = _pad_lanes(ssd_dt_bias[0])
    alog = _pad_lanes(ssd_a_log[0])
    dexp = jnp.repeat(ssd_d[0], SSD_HEAD_DIM).reshape(1, d_inner)
    ssd_nw = ssd_norm_w[0].reshape(1, d_inner)

    mod_p = ada[0, bs:bs + bp].reshape(bp, 1, 6 * d)
    mod_s = ada[0, :bs].reshape(1, bs, 6 * d)
    nw_mix = norm_mix_w[0].reshape(1, d)
    nw_mlp = norm_mlp_w[0].reshape(1, d)
    fw = norm_final_w.reshape(1, d)

    proj_p, dt_p = _nm_matmul(xp, nw_mix, mod_p, 0, 1, w_in, w_dt, seq, min(seq, 1024), 2048)
    ycat_p, scst_p, cvst_p, ssm_p = _ssd_prompt(proj_p, dt_p, scw, cw, cb, dtb, alog, dexp, ssd_nw, bp, seq)
    tm_p = min(seq, 1024)
    xp = _mm_res(ycat_p, w_out, xp, mod_p, 2, seq, tm_p)
    xp = _mlp(xp, nw_mlp, mod_p, w_up_b[0], w_down_b[0], fw, seq, tm_p, False)

    proj_s, dt_s = _nm_matmul(xs, nw_mix, mod_s, 0, 1, w_in, w_dt, bs, bs, 1024)
    ysc_s, scn_s, cvn_s, xs_s, b_s, c_s, xdt_t, dec_s = _even_sample_prep(
        proj_s, dt_s, state_sc_conv[0].reshape(bs, -1), state_ssd_conv[0].reshape(bs, -1),
        scw, cw, cb, dtb, alog, d_inner)
    ssm_s, y_t = _ssm_sample(state_ssm[0], xdt_t, dec_s, b_s, c_s)
    xs = _even_out_sample(y_t, xs_s, proj_s, ysc_s, dexp, ssd_nw, w_out, xs, mod_s)
    xs = _mlp(xs, nw_mlp, mod_s, w_up_b[0], w_down_b[0], fw, bs, bs, False)

    mod_p = ada[1, bs:bs + bp].reshape(bp, 1, 6 * d)
    mod_s = ada[1, :bs].reshape(1, bs, 6 * d)
    nw_mix = norm_mix_w[1].reshape(1, d)
    nw_mlp = norm_mlp_w[1].reshape(1, d)

    tabs_p = _rope_tables(seq, 0, 1)
    q_p, kt_p, vt_p, km_p, klo_p, khi_p, vtb_p = _qkv(xp, nw_mix, mod_p, w_qkv_p, tabs_p, seq, MOBA_BLOCK, True)
    n_blk = seq // MOBA_BLOCK
    kmean_p = jnp.pad(km_p.reshape(bp, n_blk, kvw), ((0, 0), (0, SEL_LANES - n_blk), (0, 0)))
    o_p = _moba_prompt(q_p, klo_p, khi_p, vtb_p, kmean_p, bp, seq)
    xp = _mm_res(o_p, w_o_p, xp, mod_p, 2, seq, tm_p)
    y_prompt = _mlp(xp, nw_mlp, mod_p, w_up_b[1], w_down_b[1], fw, seq, tm_p, True)

    tabs_s = _rope_tables(bs, past_len, 0)
    q_s, k_s, v_s = _qkv(xs, nw_mix, mod_s, w_qkv_p, tabs_s, bs, bs, False)
    o_s = _moba_sample(q_s, k_s, v_s, cache_k[0].transpose(0, 2, 3, 1), cache_v[0].transpose(0, 2, 3, 1), page_table)
    xs = _mm_res(o_s.reshape(bs, nq).astype(BF16), w_o_p, xs, mod_s, 2, bs, bs)
    y_sample = _mlp(xs, nw_mlp, mod_s, w_up_b[1], w_down_b[1], fw, bs, bs, True)

    def kv_out(t):
        return t.reshape(bp, ATT_KV_HEADS, ATT_HEAD_DIM, seq).transpose(0, 3, 1, 2)[None]

    return (y_prompt.reshape(bp, seq, d), y_sample.reshape(bs, 1, d),
            scst_p[:, SUBLANE - 2:, :][None], scn_s.reshape(1, bs, 2, dsc),
            cvst_p[:, SUBLANE - 3:, :][None], cvn_s.reshape(1, bs, 3, conv_dim),
            ssm_p.reshape(1, bp, n_heads, SSD_HEAD_DIM, SSD_STATE), ssm_s[None],
            kv_out(kt_p), k_s.reshape(1, bs, 1, ATT_KV_HEADS, ATT_HEAD_DIM),
            kv_out(vt_p), v_s.reshape(1, bs, 1, ATT_KV_HEADS, ATT_HEAD_DIM))
```

```python
import functools
import math

import jax
import jax.numpy as jnp
from jax import lax
from jax.experimental import pallas as pl
from jax.experimental.pallas import tpu as pltpu

F32 = jnp.float32
BF16 = jnp.bfloat16
HIGHEST = lax.Precision.HIGHEST

EPS = 1e-6
LANE = 128
SUBLANE = 8
BF16_SUBLANE = 16
SSD_HEAD_DIM = 64
SSD_GROUPS = 4
SSD_STATE = 128
SSD_CHUNK = 256
ATT_HEADS = 16
ATT_KV_HEADS = 4
ATT_GROUP = ATT_HEADS // ATT_KV_HEADS
ATT_HEAD_DIM = 64
ROT_DIM = ATT_HEAD_DIM // 4
ROPE_THETA = 500000.0
MOBA_BLOCK = 256
MOBA_TOPK = 3
SEL_LANES = 16
NEG_BIG = -1e30
FAST_SPAN = 8
EXP2_CLAMP = 60.0
VMEM_LIMIT = 56 * 1024 * 1024

NT_DIMS = (((1,), (1,)), ((), ()))


def _cparams(*sem):
    return pltpu.CompilerParams(dimension_semantics=sem, vmem_limit_bytes=VMEM_LIMIT)


def _silu(x):
    return x * jax.nn.sigmoid(x)


def _softplus(x):
    return jnp.maximum(x, 0.0) + jnp.log1p(jnp.exp(-jnp.abs(x)))


def _norm_mod(x, nw, shift, scale):
    ms = jnp.mean(x * x, axis=-1, keepdims=True)
    xn = x * lax.rsqrt(ms + EPS) * nw
    return xn * (1.0 + scale) + shift


def _resident(shape):
    nd = len(shape)
    return pl.BlockSpec(shape, lambda *_: (0,) * nd, pipeline_mode=pl.Buffered(1))


def _ada_kernel(c_ref, w_ref, b_ref, o_ref):
    s = _silu(c_ref[...]).astype(BF16)
    o_ref[0] = jnp.dot(s, w_ref[0].astype(BF16), preferred_element_type=F32) + b_ref[0]


def _ada(c_all, w_ada, b_ada):
    rows, d = c_all.shape
    n_layers, _, n6 = w_ada.shape
    tn = 1024
    return pl.pallas_call(
        _ada_kernel,
        grid=(n_layers, n6 // tn),
        in_specs=[pl.BlockSpec((rows, d), lambda l, j: (0, 0)),
                  pl.BlockSpec((1, d, tn), lambda l, j: (l, 0, j)),
                  pl.BlockSpec((1, 1, tn), lambda l, j: (l, 0, j))],
        out_specs=pl.BlockSpec((1, rows, tn), lambda l, j: (l, 0, j)),
        out_shape=jax.ShapeDtypeStruct((n_layers, rows, n6), F32),
        compiler_params=_cparams("parallel", "parallel"),
        name="adaln",
    )(c_all, w_ada, b_ada.reshape(n_layers, 1, n6))


def _mod_spec(mod, k, rows_per_group, tm):
    _, r, n6 = mod.shape
    d = n6 // 6
    tiles_per_group = max(rows_per_group // tm, 1)
    return pl.BlockSpec((1, r, d), lambda i, *_: (i // tiles_per_group, 0, k))


def _nm_matmul_kernel(x_ref, nw_ref, sh_ref, sc_ref, w_ref, w2_ref, o_ref, o2_ref, h_scr):
    @pl.when(pl.program_id(1) == 0)
    def _():
        hb = _norm_mod(x_ref[...], nw_ref[...], sh_ref[0], sc_ref[0]).astype(BF16)
        h_scr[...] = hb
        o2_ref[...] = jnp.dot(hb, w2_ref[...], preferred_element_type=F32)

    o_ref[...] = jnp.dot(h_scr[...], w_ref[...], preferred_element_type=F32)


def _nm_matmul(x, nw, mod, k_shift, k_scale, w, n, w2, rows_per_group, tm, tn):
    t, d = x.shape
    n2 = w2.shape[1]
    assert n % tn == 0 and n <= w.shape[1]
    return pl.pallas_call(
        _nm_matmul_kernel,
        grid=(t // tm, n // tn),
        in_specs=[pl.BlockSpec((tm, d), lambda i, j: (i, 0)),
                  pl.BlockSpec((1, d), lambda i, j: (0, 0)),
                  _mod_spec(mod, k_shift, rows_per_group, tm),
                  _mod_spec(mod, k_scale, rows_per_group, tm),
                  pl.BlockSpec((d, tn), lambda i, j: (0, j)),
                  pl.BlockSpec((d, n2), lambda i, j: (0, 0))],
        out_specs=[pl.BlockSpec((tm, tn), lambda i, j: (i, j)),
                   pl.BlockSpec((tm, n2), lambda i, j: (i, 0))],
        out_shape=[jax.ShapeDtypeStruct((t, n), F32), jax.ShapeDtypeStruct((t, n2), F32)],
        scratch_shapes=[pltpu.VMEM((tm, d), BF16)],
        compiler_params=_cparams("parallel", "arbitrary"),
        name="in_proj",
    )(x, nw, mod, mod, w, w2)


def _mm_res_kernel(a_ref, w_ref, res_ref, g_ref, o_ref):
    o_ref[...] = res_ref[...] + g_ref[0] * jnp.dot(a_ref[...], w_ref[...], preferred_element_type=F32)


def _mm_res(a, w, res, mod, k_gate, rows_per_group, tm):
    t, kdim = a.shape
    d = w.shape[1]
    return pl.pallas_call(
        _mm_res_kernel,
        grid=(t // tm,),
        in_specs=[pl.BlockSpec((tm, kdim), lambda i: (i, 0)),
                  _resident((kdim, d)),
                  pl.BlockSpec((tm, d), lambda i: (i, 0)),
                  _mod_spec(mod, k_gate, rows_per_group, tm)],
        out_specs=pl.BlockSpec((tm, d), lambda i: (i, 0)),
        out_shape=jax.ShapeDtypeStruct((t, d), F32),
        compiler_params=_cparams("parallel"),
        name="proj_residual",
    )(a, w, res, mod)


def _mlp_kernel(x_ref, nw_ref, sh_ref, sc_ref, g_ref, wu_ref, wd_ref, fw_ref, o_ref, *, final_norm, tf):
    x = x_ref[...]
    hb = _norm_mod(x, nw_ref[...], sh_ref[0], sc_ref[0]).astype(BF16)
    acc = jnp.zeros(x.shape, F32)
    for f in range(wu_ref.shape[1] // tf):
        a = jnp.maximum(jnp.dot(hb, wu_ref[:, f * tf:(f + 1) * tf], preferred_element_type=F32), 0.0)
        acc = acc + jnp.dot((a * a).astype(BF16), wd_ref[f * tf:(f + 1) * tf, :], preferred_element_type=F32)
    y = x + g_ref[0] * acc
    if final_norm:
        ms = jnp.mean(y * y, axis=-1, keepdims=True)
        y = y * lax.rsqrt(ms + EPS) * fw_ref[...]
    o_ref[...] = y


def _mlp(x, nw, mod, w_up, w_down, fw, rows_per_group, tm, final_norm):
    t, d = x.shape
    dff = w_up.shape[1]
    return pl.pallas_call(
        functools.partial(_mlp_kernel, final_norm=final_norm, tf=1024),
        grid=(t // tm,),
        in_specs=[pl.BlockSpec((tm, d), lambda i: (i, 0)),
                  pl.BlockSpec((1, d), lambda i: (0, 0)),
                  _mod_spec(mod, 3, rows_per_group, tm),
                  _mod_spec(mod, 4, rows_per_group, tm),
                  _mod_spec(mod, 5, rows_per_group, tm),
                  _resident((d, dff)),
                  _resident((dff, d)),
                  pl.BlockSpec((1, d), lambda i: (0, 0))],
        out_specs=pl.BlockSpec((tm, d), lambda i: (i, 0)),
        out_shape=jax.ShapeDtypeStruct((t, d), F32),
        compiler_params=_cparams("parallel"),
        name="mlp",
    )(x, nw, mod, mod, mod, w_up, w_down, fw)


def _shift_rows(cur, tail, k):
    rolled = pltpu.roll(cur, k, axis=0)
    row = lax.broadcasted_iota(jnp.int32, tail.shape, 0)
    top = jnp.where(row < k, pltpu.roll(tail, k, axis=0), rolled[:SUBLANE])
    return jnp.concatenate([top, rolled[SUBLANE:]], axis=0)


def _causal_conv(cur, tail, w):
    kk = w.shape[0]
    out = cur * w[kk - 1:kk, :]
    for s in range(1, kk):
        out = out + _shift_rows(cur, tail, s) * w[kk - 1 - s:kk - s, :]
    return out

def _expand_heads(m, h0, n_cols):
    rows = m.shape[0]
    lane = lax.broadcasted_iota(jnp.int32, (rows, LANE), 1)
    cols = []
    for c in range(n_cols):
        lo = jnp.broadcast_to(m[:, h0 + 2 * c:h0 + 2 * c + 1], (rows, LANE))
        hi = jnp.broadcast_to(m[:, h0 + 2 * c + 1:h0 + 2 * c + 2], (rows, LANE))
        cols.append(jnp.where(lane < SSD_HEAD_DIM, lo, hi))
    return jnp.concatenate(cols, axis=1)


def _ssd_prompt_kernel(scb_ref, scc_ref, scx_ref, z0_ref, z1_ref, x0_ref, x1_ref, bc_ref, dt_ref,
                       scw_ref, cw_ref, cb_ref, dtb_ref, alog_ref, dexp_ref, nw_ref,
                       y_ref, scst_ref, cvst_ref, ssm_ref,
                       utail, xtail, s_scr):
    c = pl.program_id(1)
    q = scb_ref.shape[0]
    dsc = scb_ref.shape[1]
    gw = 2 * x0_ref.shape[1] // SSD_GROUPS
    hpg = gw // SSD_HEAD_DIM

    @pl.when(c == 0)
    def _():
        utail[...] = jnp.zeros(utail.shape, F32)
        xtail[...] = jnp.zeros(xtail.shape, F32)
        s_scr[...] = jnp.zeros(s_scr.shape, F32)

    u = scc_ref[...] * scx_ref[...]
    y_sc = scb_ref[...] * _causal_conv(u, utail[...], scw_ref[...])
    utail[...] = u[q - SUBLANE:, :]
    y_ref[:, :dsc] = y_sc.astype(BF16)

    bc = bc_ref[...]
    n_bc = bc.shape[1]
    o_bc = 2 * x0_ref.shape[1]
    bc_cols = slice(o_bc, o_bc + n_bc)
    bc_c = _silu(_causal_conv(bc, xtail[:, bc_cols], cw_ref[:, bc_cols]) + cb_ref[:, bc_cols])
    xtail[:, bc_cols] = bc[q - SUBLANE:, :]
    half = n_bc // 2

    dt = _softplus(dt_ref[...] + dtb_ref[...])
    a = dt * (-jnp.exp(alog_ref[...]))
    ri = lax.broadcasted_iota(jnp.int32, (q, q), 0)
    ci = lax.broadcasted_iota(jnp.int32, (q, q), 1)
    tril = ri >= ci
    a_cs = jnp.dot(tril.astype(F32), a, precision=HIGHEST, preferred_element_type=F32) * math.log2(math.e)
    a_cs_t = a_cs.T
    quad = 4 * SSD_HEAD_DIM
    lane_head = lax.broadcasted_iota(jnp.int32, (q, quad), 1) // SSD_HEAD_DIM

    for g in range(SSD_GROUPS):
        x_ref = x0_ref if g < SSD_GROUPS // 2 else x1_ref
        z_ref = z0_ref if g < SSD_GROUPS // 2 else z1_ref
        lo = (g % (SSD_GROUPS // 2)) * gw
        go = g * gw
        xg = x_ref[:, lo:lo + gw]
        xs = _silu(_causal_conv(xg, xtail[:, go:go + gw], cw_ref[:, go:go + gw]) + cb_ref[:, go:go + gw])
        xtail[:, go:go + gw] = xg[q - SUBLANE:, :]

        n_cols = gw // LANE
        dt_x = _expand_heads(dt, g * hpg, n_cols)
        a_x = _expand_heads(a_cs, g * hpg, n_cols)
        ea = jnp.exp2(a_x)
        dte = jnp.exp2(a_x[q - 1:q, :] - a_x)
        xdt = xs * dt_x
        xdt_b = xdt.astype(BF16)

        bg = bc_c[:, g * SSD_STATE:(g + 1) * SSD_STATE]
        cg_b = bc_c[:, half + g * SSD_STATE:half + (g + 1) * SSD_STATE].astype(BF16)
        cb_mat = lax.dot_general(cg_b, bg.astype(BF16), NT_DIMS, preferred_element_type=F32)
        s_g = s_scr[:, go:go + gw]
        y_off = jnp.dot(cg_b, s_g.astype(BF16), preferred_element_type=F32) * ea

        y_quads = []
        for qd in range(gw // quad):
            xq = xdt[:, qd * quad:(qd + 1) * quad]
            acc = None
            for i in range(4):
                h = g * hpg + qd * 4 + i
                seg = a_cs[:, h:h + 1] - a_cs_t[h:h + 1, :]
                dec = jnp.exp2(jnp.where(tril, seg, -jnp.inf))
                w_h = (cb_mat * dec).astype(BF16)
                xm = jnp.where(lane_head == i, xq, 0.0).astype(BF16)
                d = jnp.dot(w_h, xm, preferred_element_type=F32)
                acc = d if acc is None else acc + d
            y_quads.append(acc)
        y = jnp.concatenate(y_quads, axis=1) + y_off + dexp_ref[:, go:go + gw] * xs

        zg = z_ref[:, lo:lo + gw]
        y = y * _silu(zg)
        ms = jnp.mean(y * y, axis=-1, keepdims=True)
        y = y * lax.rsqrt(ms + EPS) * nw_ref[:, go:go + gw]
        y_ref[:, dsc + go:dsc + go + gw] = y.astype(BF16)

        s_new = s_g * ea[q - 1:q, :] + jnp.dot(bg.T.astype(BF16), (xdt * dte).astype(BF16),
                                               preferred_element_type=F32)
        s_scr[:, go:go + gw] = s_new

    @pl.when(c == pl.num_programs(1) - 1)
    def _():
        scst_ref[0] = utail[...]
        cvst_ref[0] = xtail[...]
        ssm_ref[0] = s_scr[...].T


def _ssd_prompt(proj, dt_raw, scw, cw, cb, dtb, alog, dexp, nw, bp, seq):
    q = min(SSD_CHUNK, seq)
    nc = seq // q
    dsc = scw.shape[1]
    d_inner = dexp.shape[1]
    conv_dim = cw.shape[1]
    bw = dsc

    def col(kb):
        return pl.BlockSpec((q, bw), lambda b, c: (b * nc + c, kb))

    n_z = d_inner // bw
    specs = [col(0), col(1), col(2)]
    specs += [col(3 + i) for i in range(n_z)]
    specs += [col(3 + n_z + i) for i in range(n_z)]
    specs += [col(3 + 2 * n_z)]
    specs += [pl.BlockSpec((q, LANE), lambda b, c: (b * nc + c, 0))]
    for arr in (scw, cw, cb, dtb, alog, dexp, nw):
        specs.append(pl.BlockSpec(arr.shape, lambda b, c: (0, 0)))
    return pl.pallas_call(
        _ssd_prompt_kernel,
        grid=(bp, nc),
        in_specs=specs,
        out_specs=[pl.BlockSpec((q, dsc + d_inner), lambda b, c: (b * nc + c, 0)),
                   pl.BlockSpec((1, SUBLANE, dsc), lambda b, c: (b, 0, 0)),
                   pl.BlockSpec((1, SUBLANE, conv_dim), lambda b, c: (b, 0, 0)),
                   pl.BlockSpec((1, d_inner, SSD_STATE), lambda b, c: (b, 0, 0))],
        out_shape=[jax.ShapeDtypeStruct((bp * seq, dsc + d_inner), BF16),
                   jax.ShapeDtypeStruct((bp, SUBLANE, dsc), F32),
                   jax.ShapeDtypeStruct((bp, SUBLANE, conv_dim), F32),
                   jax.ShapeDtypeStruct((bp, d_inner, SSD_STATE), F32)],
        scratch_shapes=[pltpu.VMEM((SUBLANE, dsc), F32),
                        pltpu.VMEM((SUBLANE, conv_dim), F32),
                        pltpu.VMEM((SSD_STATE, d_inner), F32)],
        compiler_params=_cparams("parallel", "arbitrary"),
        name="ssd_prompt",
    )(*([proj] * (4 + 2 * n_z)), dt_raw, scw, cw, cb, dtb, alog, dexp, nw)


def _even_sample_prep_kernel(proj_ref, dt_ref, scst_ref, cvst_ref, scw_ref, cw_ref, cb_ref, dtb_ref, alog_ref,
                             ysc_ref, scn_ref, cvn_ref, xs_ref, b_ref, c_ref, xdt_t_ref, dec_ref):
    dsc = scw_ref.shape[1]
    conv_dim = cw_ref.shape[1]
    d_inner = xs_ref.shape[1]
    n_heads = d_inner // SSD_HEAD_DIM
    scb = proj_ref[:, :dsc]
    u = proj_ref[:, dsc:2 * dsc] * proj_ref[:, 2 * dsc:3 * dsc]
    xbc = proj_ref[:, 3 * dsc + d_inner:3 * dsc + d_inner + conv_dim]
    b0 = scst_ref[:, :dsc]
    b1 = scst_ref[:, dsc:]
    ysc_ref[...] = scb * (b0 * scw_ref[0:1, :] + b1 * scw_ref[1:2, :] + u * scw_ref[2:3, :])
    scn_ref[:, :dsc] = b1
    scn_ref[:, dsc:] = u

    c0 = cvst_ref[:, :conv_dim]
    c1 = cvst_ref[:, conv_dim:2 * conv_dim]
    c2 = cvst_ref[:, 2 * conv_dim:]
    act = _silu(c0 * cw_ref[0:1, :] + c1 * cw_ref[1:2, :] + c2 * cw_ref[2:3, :] + xbc * cw_ref[3:4, :] + cb_ref[...])
    cvn_ref[:, :conv_dim] = c1
    cvn_ref[:, conv_dim:2 * conv_dim] = c2
    cvn_ref[:, 2 * conv_dim:] = xbc
    xs = act[:, :d_inner]
    n_bc = (conv_dim - d_inner) // 2
    xs_ref[...] = xs
    b_ref[...] = act[:, d_inner:d_inner + n_bc]
    c_ref[...] = act[:, d_inner + n_bc:]

    dt = _softplus(dt_ref[...] + dtb_ref[...])
    dec_ref[...] = jnp.exp(dt * (-jnp.exp(alog_ref[...])))
    xdt_t_ref[...] = (xs * _expand_heads(dt, 0, n_heads // 2)).T


def _even_sample_prep(proj, dt_raw, scst, cvst, scw, cw, cb, dtb, alog, d_inner):
    bs = proj.shape[0]
    dsc = scw.shape[1]
    conv_dim = cw.shape[1]
    n_bc = (conv_dim - d_inner) // 2
    shapes = [(bs, dsc), (bs, 2 * dsc), (bs, 3 * conv_dim), (bs, d_inner), (bs, n_bc), (bs, n_bc),
              (d_inner, bs), (bs, LANE)]
    return pl.pallas_call(
        _even_sample_prep_kernel,
        out_shape=[jax.ShapeDtypeStruct(s, F32) for s in shapes],
        compiler_params=pltpu.CompilerParams(vmem_limit_bytes=VMEM_LIMIT),
        name="even_sample_prep",
    )(proj, dt_raw, scst, cvst, scw, cw, cb, dtb, alog)


def _ssm_sample_kernel(dec_ref, h0_ref, xt_ref, b_ref, c_ref, hn_ref, yt_ref):
    h = pl.program_id(0)
    bs = h0_ref.shape[0]
    p = h0_ref.shape[2]
    n = h0_ref.shape[3]
    lane = lax.broadcasted_iota(jnp.int32, (p, bs), 1)
    xt = xt_ref[...]
    ones = jnp.ones((n, bs), BF16)
    yt = jnp.zeros((p, bs), F32)
    for b in range(bs):
        xcol = jnp.broadcast_to(xt[:, b:b + 1], (p, n))
        hn = h0_ref[b, 0] * dec_ref[b, h] + xcol * b_ref[b:b + 1, :]
        hn_ref[b, 0] = hn
        ysum = jnp.zeros((p, bs), F32)
        rest = hn * c_ref[b:b + 1, :]
        for _ in range(2):
            piece = rest.astype(BF16)
            ysum = ysum + jnp.dot(piece, ones, preferred_element_type=F32)
            rest = rest - piece.astype(F32)
        yt = jnp.where(lane == b, ysum, yt)
    yt_ref[...] = yt


def _ssm_sample(h0, xdt_t, dec, bm, cm):
    bs, n_heads, p, n = h0.shape
    hpg = n_heads // SSD_GROUPS
    return pl.pallas_call(
        _ssm_sample_kernel,
        grid=(n_heads,),
        in_specs=[pl.BlockSpec(memory_space=pltpu.SMEM),
                  pl.BlockSpec((bs, 1, p, n), lambda h: (0, h, 0, 0)),
                  pl.BlockSpec((p, bs), lambda h: (h, 0)),
                  pl.BlockSpec((bs, n), lambda h: (0, h // hpg)),
                  pl.BlockSpec((bs, n), lambda h: (0, h // hpg))],
        out_specs=[pl.BlockSpec((bs, 1, p, n), lambda h: (0, h, 0, 0)),
                   pl.BlockSpec((p, bs), lambda h: (h, 0))],
        out_shape=[jax.ShapeDtypeStruct(h0.shape, F32), jax.ShapeDtypeStruct((n_heads * p, bs), F32)],
        compiler_params=_cparams("parallel"),
        name="ssm_sample",
    )(dec, h0, xdt_t, bm, cm)


def _even_out_sample_kernel(yt_ref, xs_ref, proj_ref, ysc_ref, dexp_ref, nw_ref, w_ref, x_ref, g_ref, o_ref):
    d_inner = xs_ref.shape[1]
    dsc = ysc_ref.shape[1]
    gw = d_inner // SSD_GROUPS
    y = yt_ref[...].T + dexp_ref[...] * xs_ref[...]
    y = y * _silu(proj_ref[:, 3 * dsc:3 * dsc + d_inner])
    parts = []
    for g in range(SSD_GROUPS):
        yg = y[:, g * gw:(g + 1) * gw]
        ms = jnp.mean(yg * yg, axis=-1, keepdims=True)
        parts.append(yg * lax.rsqrt(ms + EPS))
    y_ssd = jnp.concatenate(parts, axis=1) * nw_ref[...]
    cat = jnp.concatenate([ysc_ref[...], y_ssd], axis=1).astype(BF16)
    o_ref[...] = x_ref[...] + g_ref[0] * jnp.dot(cat, w_ref[...], preferred_element_type=F32)


def _even_out_sample(y_t, xs, proj, ysc, dexp, nw, w_out, x, mod):
    bs, d = x.shape
    full = lambda a: pl.BlockSpec(a.shape, lambda i: (0,) * a.ndim)
    return pl.pallas_call(
        _even_out_sample_kernel,
        grid=(1,),
        in_specs=[full(y_t), full(xs), full(proj), full(ysc), full(dexp), full(nw), full(w_out), full(x),
                  pl.BlockSpec((1, bs, d), lambda i: (0, 0, 2))],
        out_specs=pl.BlockSpec((bs, d), lambda i: (0, 0)),
        out_shape=jax.ShapeDtypeStruct((bs, d), F32),
        compiler_params=_cparams("arbitrary"),
        name="even_out_sample",
    )(y_t, xs, proj, ysc, dexp, nw, w_out, x, mod)


def _rope_table_kernel(ca_ref, cb_ref, cc_ref, *, pos0, stride):
    rows = ca_ref.shape[0]
    row = lax.broadcasted_iota(jnp.int32, (rows, LANE), 0) + pl.program_id(0) * rows
    d = lax.broadcasted_iota(jnp.int32, (rows, LANE), 1) % ATT_HEAD_DIM
    half = ROT_DIM // 2
    pos = (pos0 + row * stride).astype(F32)
    inv = jnp.power(jnp.full((rows, LANE), ROPE_THETA, F32), -((d % half).astype(F32) * 2.0 / ROT_DIM))
    ang = pos * inv
    cs = jnp.cos(ang)
    sn = jnp.sin(ang)
    ca_ref[...] = jnp.where(d < ROT_DIM, cs, 1.0)
    cb_ref[...] = jnp.where(d < half, -sn, 0.0)
    cc_ref[...] = jnp.where(d < half, 0.0, jnp.where(d < ROT_DIM, sn, 0.0))


def _rope_tables(rows, pos0, stride):
    tr = min(rows, 512)
    spec = pl.BlockSpec((tr, LANE), lambda i: (i, 0))
    return pl.pallas_call(
        functools.partial(_rope_table_kernel, pos0=pos0, stride=stride),
        grid=(rows // tr,),
        out_specs=[spec, spec, spec],
        out_shape=[jax.ShapeDtypeStruct((rows, LANE), F32)] * 3,
        compiler_params=_cparams("parallel"),
        name="rope_tables",
    )()


def _rope(t, ca, cb, cc):
    cols = []
    for c in range(t.shape[1] // LANE):
        s = t[:, c * LANE:(c + 1) * LANE]
        cols.append(s * ca + pltpu.roll(s, LANE - ROT_DIM // 2, axis=1) * cb + pltpu.roll(s, ROT_DIM // 2, axis=1) * cc)
    return jnp.concatenate(cols, axis=1)


def _sel_keys(k, j):
    lane = lax.broadcasted_iota(jnp.int32, (k.shape[0], LANE), 1)
    onehot = jnp.where(lane == j, 1.0, 0.0)
    in_sel = lane < SEL_LANES
    k_lo = jnp.concatenate([jnp.where(in_sel, onehot, k[:, :LANE]), k[:, LANE:]], axis=1)
    k_hi = jnp.concatenate([k[:, :LANE], jnp.where(in_sel, onehot, k[:, LANE:])], axis=1)
    return k_lo.astype(BF16), k_hi.astype(BF16)


def _qkv_kernel(x_ref, nw_ref, sh_ref, sc_ref, w_ref, ca_ref, cb_ref, cc_ref, q_ref, k_ref, v_ref, *blk_refs,
                blocks_per_seq):
    nq = q_ref.shape[1]
    nk = (w_ref.shape[1] - nq) // 2
    hb = _norm_mod(x_ref[...], nw_ref[...], sh_ref[0], sc_ref[0]).astype(BF16)
    qkv = jnp.dot(hb, w_ref[...], preferred_element_type=F32)
    ca, cb, cc = ca_ref[...], cb_ref[...], cc_ref[...]
    q_ref[...] = _rope(qkv[:, :nq], ca, cb, cc)
    k = _rope(qkv[:, nq:nq + nk], ca, cb, cc)
    v = qkv[:, nq + nk:]
    if blocks_per_seq:
        km_ref, klo_ref, khi_ref, vt_ref = blk_refs
        vt = v.T
        k_ref[0] = k.T
        v_ref[0] = vt
        km_ref[0] = jnp.mean(k, axis=0, keepdims=True)
        klo_ref[...], khi_ref[...] = _sel_keys(k, pl.program_id(0) % blocks_per_seq)
        hd = ATT_HEAD_DIM
        ones_row = jnp.where(lax.broadcasted_iota(jnp.int32, (BF16_SUBLANE, vt.shape[1]), 0) == 0, 1.0, 0.0)
        vt_ref[0] = jnp.concatenate([piece for h in range(nk // hd) for piece in (vt[h * hd:(h + 1) * hd], ones_row)],
                                    axis=0).astype(BF16)
    else:
        k_ref[...] = k
        v_ref[...] = v


def _qkv(x, nw, mod, w, tabs, rows_per_group, tm, per_block):
    t, d = x.shape
    nq = ATT_HEADS * ATT_HEAD_DIM
    nk = ATT_KV_HEADS * ATT_HEAD_DIM
    tab_tiles = tabs[0].shape[0] // tm
    tab = pl.BlockSpec((tm, LANE), lambda i: (i % tab_tiles, 0))
    row = lambda n: pl.BlockSpec((tm, n), lambda i: (i, 0))
    if per_block:
        bps = rows_per_group // tm
        kv_t = pl.BlockSpec((1, nk, tm), lambda i: (i // bps, 0, i % bps))
        v_rows = ATT_KV_HEADS * (ATT_HEAD_DIM + BF16_SUBLANE)
        out_specs = [row(nq), kv_t, kv_t, pl.BlockSpec((1, 1, nk), lambda i: (i, 0, 0)), row(nk), row(nk),
                     pl.BlockSpec((1, v_rows, tm), lambda i: (i, 0, 0))]
        out_shape = [jax.ShapeDtypeStruct((t, nq), F32),
                     jax.ShapeDtypeStruct((t // rows_per_group, nk, rows_per_group), F32),
                     jax.ShapeDtypeStruct((t // rows_per_group, nk, rows_per_group), F32),
                     jax.ShapeDtypeStruct((t // tm, 1, nk), F32), jax.ShapeDtypeStruct((t, nk), BF16),
                     jax.ShapeDtypeStruct((t, nk), BF16), jax.ShapeDtypeStruct((t // tm, v_rows, tm), BF16)]
    else:
        out_specs = [row(nq), row(nk), row(nk)]
        out_shape = [jax.ShapeDtypeStruct((t, nq), F32), jax.ShapeDtypeStruct((t, nk), F32),
                     jax.ShapeDtypeStruct((t, nk), F32)]
    return pl.pallas_call(
        functools.partial(_qkv_kernel, blocks_per_seq=rows_per_group // tm if per_block else 0),
        grid=(t // tm,),
        in_specs=[row(d), pl.BlockSpec((1, d), lambda i: (0, 0)),
                  _mod_spec(mod, 0, rows_per_group, tm), _mod_spec(mod, 1, rows_per_group, tm),
                  _resident(w.shape), tab, tab, tab],
        out_specs=out_specs,
        out_shape=out_shape,
        compiler_params=_cparams("parallel"),
        name="qkv_rope",
    )(x, nw, mod, mod, w, *tabs)


def _select_bias(scores, n_past, n_rounds):
    lane = lax.broadcasted_iota(jnp.int32, scores.shape, 1)
    lanef = lane.astype(F32)
    s = jnp.where(lane < n_past, scores, -jnp.inf)
    bias = jnp.full(scores.shape, NEG_BIG, F32)
    for r in range(n_rounds):
        mx = jnp.max(s, axis=1, keepdims=True)
        idx = jnp.min(jnp.where(s == mx, lanef, float(LANE)), axis=1, keepdims=True)
        bias = jnp.where(lanef == jnp.where(n_past > r, idx, -1.0), 0.0, bias)
        s = jnp.where(lanef == idx, -jnp.inf, s)
    return bias


def _select_bias_t(scores, n_past, n_rounds):
    row = lax.broadcasted_iota(jnp.int32, scores.shape, 0)
    rowf = row.astype(F32)
    s = jnp.where(row < n_past, scores, -jnp.inf)
    bias = jnp.full(scores.shape, NEG_BIG, F32)
    for r in range(n_rounds):
        mx = jnp.max(s, axis=0, keepdims=True)
        idx = jnp.min(jnp.where(s == mx, rowf, float(SEL_LANES)), axis=0, keepdims=True)
        bias = jnp.where(rowf == jnp.where(n_past > r, idx, -1.0), 0.0, bias)
        s = jnp.where(rowf == idx, -jnp.inf, s)
    return jnp.where(row == n_past, 0.0, bias)


def _moba_prompt_kernel(q_ref, klo_ref, khi_ref, vt_ref, km_ref, o_ref, qa_scr, m_scr, over_scr, acc_scr):
    qi = pl.program_id(1)
    blk = q_ref.shape[0]
    kvw = klo_ref.shape[1]
    hd = ATT_HEAD_DIM
    n_q = ATT_GROUP * blk
    kmean = km_ref[0]
    km_hi = kmean.astype(BF16)
    km_lo = (kmean - km_hi.astype(F32)).astype(BF16)
    n_rounds = min(MOBA_TOPK, klo_ref.shape[0] // blk)

    q_t = q_ref[...].T
    for h in range(ATT_KV_HEADS):
        qh_b = jnp.concatenate([q_t[g * kvw + h * hd:g * kvw + (h + 1) * hd, :] for g in range(ATT_GROUP)],
                               axis=1).astype(BF16)
        scores_t = (jnp.dot(km_hi[:, h * hd:(h + 1) * hd], qh_b, preferred_element_type=F32)
                    + jnp.dot(km_lo[:, h * hd:(h + 1) * hd], qh_b, preferred_element_type=F32))
        bias_t = _select_bias_t(scores_t, qi, n_rounds)
        sel_row = LANE if h == 0 else 0
        qa_scr[h] = jnp.zeros((kvw, n_q), BF16)
        qa_scr[h, h * hd:(h + 1) * hd, :] = qh_b
        qa_scr[h, sel_row:sel_row + SEL_LANES, :] = bias_t.astype(BF16)

    v_rows = vt_ref.shape[1] // ATT_KV_HEADS
    key_row = lax.broadcasted_iota(jnp.int32, (blk, n_q), 0)
    q_pos = lax.broadcasted_iota(jnp.int32, (blk, n_q), 1) % blk

    def scores(h, j):
        off = pl.multiple_of(j * blk, blk)
        ka = (khi_ref if h == 0 else klo_ref)[pl.ds(off, blk), :]
        return jnp.dot(ka, qa_scr[h], preferred_element_type=F32)

    def softmax(h, s, first):
        if first:
            s = jnp.where(key_row <= q_pos, s, NEG_BIG)
        mx = jnp.max(s, axis=0, keepdims=True)
        if first:
            m_new, alpha = mx, None
        else:
            m_old = m_scr[h, 0:1, :]
            m_new = jnp.maximum(m_old, mx)
            alpha = jnp.exp2(m_old - m_new)
        m_scr[h] = jnp.broadcast_to(m_new, (SUBLANE, n_q))
        return jnp.exp2((s - m_new).astype(BF16)), alpha

    def values(h, j, p, alpha):
        vt_h = vt_ref[j, h * v_rows:(h + 1) * v_rows, :]
        pv = jnp.dot(vt_h, p, preferred_element_type=F32)
        acc_scr[h] = pv if alpha is None else acc_scr[h] * alpha + pv

    def block_step(j, first):
        s_next = scores(0, j)
        for h in range(ATT_KV_HEADS):
            s = s_next
            if h + 1 < ATT_KV_HEADS:
                s_next = scores(h + 1, j)
            p, alpha = softmax(h, s, first)
            values(h, j, p, alpha)

    def finalize():
        for g in range(ATT_GROUP):
            cols = slice(g * blk, (g + 1) * blk)
            o_t = jnp.concatenate([acc_scr[h, :hd, cols] / acc_scr[h, hd:hd + 1, cols]
                                   for h in range(ATT_KV_HEADS)], axis=0)
            o_ref[:, g * kvw:(g + 1) * kvw] = o_t.T.astype(BF16)

    block_step(qi, True)

    for h in range(ATT_KV_HEADS):
        over_scr[h] = jnp.full((SUBLANE, n_q), NEG_BIG, F32)

    def fast_step(j, nb):
        off = pl.multiple_of(j * blk, blk)

        def span_scores(h):
            ka = (khi_ref if h == 0 else klo_ref)[pl.ds(off, nb * blk), :]
            return jnp.dot(ka, qa_scr[h], preferred_element_type=F32)

        s_next = span_scores(0)
        for h in range(ATT_KV_HEADS):
            s = s_next
            if h + 1 < ATT_KV_HEADS:
                s_next = span_scores(h + 1)
            d = s - m_scr[h, 0:1, :]
            over_scr[h] = jnp.maximum(over_scr[h], jnp.broadcast_to(jnp.max(d, axis=0, keepdims=True), (SUBLANE, n_q)))
            p = jnp.exp2(jnp.minimum(d, EXP2_CLAMP)).astype(BF16)
            rows = slice(h * v_rows, (h + 1) * v_rows)
            vt_h = jnp.concatenate([vt_ref[j + i, rows, :] for i in range(nb)], axis=1)
            acc_scr[h] = acc_scr[h] + jnp.dot(vt_h, p, preferred_element_type=F32)

    def fast_body(t, carry):
        fast_step(FAST_SPAN * t, FAST_SPAN)
        return carry

    lax.fori_loop(0, qi // FAST_SPAN, fast_body, 0)
    span = FAST_SPAN // 2
    while span:
        @pl.when(qi % (2 * span) >= span)
        def _(span=span):
            fast_step(qi - qi % (2 * span), span)
        span //= 2

    finalize()

    @pl.when(jnp.max(over_scr[...]) > EXP2_CLAMP)
    def _():
        block_step(qi, True)

        def body(j, carry):
            block_step(j, False)
            return carry

        lax.fori_loop(0, qi, body, 0)
        finalize()


def _moba_prompt(q, k_lo, k_hi, v_t, kmean, bp, seq):
    blk = MOBA_BLOCK
    nq = seq // blk
    kvw = k_lo.shape[1]
    n_q = ATT_GROUP * blk
    v_rows = v_t.shape[1]
    return pl.pallas_call(
        _moba_prompt_kernel,
        grid=(bp, nq),
        in_specs=[pl.BlockSpec((blk, q.shape[1]), lambda b, i: (b * nq + i, 0)),
                  pl.BlockSpec((seq, kvw), lambda b, i: (b, 0)),
                  pl.BlockSpec((seq, kvw), lambda b, i: (b, 0)),
                  pl.BlockSpec((nq, v_rows, blk), lambda b, i: (b, 0, 0)),
                  pl.BlockSpec((1, SEL_LANES, kvw), lambda b, i: (b, 0, 0))],
        out_specs=pl.BlockSpec((blk, q.shape[1]), lambda b, i: (b * nq + i, 0)),
        out_shape=jax.ShapeDtypeStruct(q.shape, BF16),
        scratch_shapes=[pltpu.VMEM((ATT_KV_HEADS, kvw, n_q), BF16),
                        pltpu.VMEM((ATT_KV_HEADS, SUBLANE, n_q), F32),
                        pltpu.VMEM((ATT_KV_HEADS, SUBLANE, n_q), F32),
                        pltpu.VMEM((ATT_KV_HEADS, v_rows // ATT_KV_HEADS, n_q), F32)],
        compiler_params=_cparams("parallel", "arbitrary"),
        name="moba_prompt",
    )(q, k_lo, k_hi, v_t, kmean)


def _moba_sample_kernel(pt_ref, q_ref, kn_ref, vn_ref, avg_ref, *rest, n_pages):
    k_pages = rest[:n_pages]
    v_pages = rest[n_pages:2 * n_pages]
    o_ref = rest[2 * n_pages]
    page = k_pages[0].shape[3]
    n_past = n_pages * page // MOBA_BLOCK
    q_rows = q_ref[0]
    kn_rows = kn_ref[0]
    vn_rows = vn_ref[0]
    n_rows = q_rows.shape[0]
    q_b = q_rows.astype(BF16)

    heads = range(ATT_KV_HEADS)
    rows = [slice(h * SUBLANE, (h + 1) * SUBLANE) for h in heads]
    kt_bs = [jnp.concatenate([r[0, h] for r in k_pages], axis=1).astype(BF16) for h in heads]
    logits = jnp.concatenate([jnp.dot(q_b[rows[h]], kt_bs[h], preferred_element_type=F32) for h in heads], axis=0)
    kmts = [jnp.dot(kt_bs[h], avg_ref[...], preferred_element_type=F32) for h in heads]
    vts =[jnp.concatenate([r[0, h] for r in v_pages], axis=1).astype(BF16) for h in heads]

    blocks = range(n_past)
    keys = [slice(j * MOBA_BLOCK, (j + 1) * MOBA_BLOCK) for j in blocks]
    mjs = [jnp.max(logits[:, keys[j]], axis=1, keepdims=True) for j in blocks]
    km_his = [km.astype(BF16) for km in kmts]
    km_los = [(kmts[h] - km_his[h].astype(F32)).astype(BF16) for h in heads]
    scores = jnp.concatenate([jnp.dot(q_b[rows[h]], km_his[h], preferred_element_type=F32)
                              + jnp.dot(q_b[rows[h]], km_los[h], preferred_element_type=F32) for h in heads], axis=0)
    pjs = [jnp.exp2(logits[:, keys[j]] - mjs[j]) for j in blocks]
    sjs = [jnp.sum(pjs[j], axis=1, keepdims=True) for j in blocks]
    pj_bs = [pjs[j].astype(BF16) for j in blocks]
    blk_out = [jnp.concatenate([lax.dot_general(pj_bs[j][rows[h]], vts[h][:, keys[j]], NT_DIMS,
                                                preferred_element_type=F32) for h in heads], axis=0)
               for j in blocks]
    bias = _select_bias(scores, n_past, min(MOBA_TOPK, n_past + 1))
    lane = lax.broadcasted_iota(jnp.int32, (n_rows, LANE), 1)
    blk_max = jnp.full((n_rows, LANE), NEG_BIG, F32)
    blk_sum = jnp.zeros((n_rows, LANE), F32)
    for j in blocks:
        blk_max = jnp.where(lane == j, mjs[j], blk_max)
        blk_sum = jnp.where(lane == j, sjs[j], blk_sum)
    l_new = jnp.sum(q_rows * kn_rows, axis=1, keepdims=True)
    chosen = blk_max + bias
    m = jnp.maximum(jnp.max(chosen, axis=1, keepdims=True), l_new)
    w = jnp.exp2(chosen - m)
    pn = jnp.exp2(l_new - m)
    denom = jnp.sum(w * blk_sum, axis=1, keepdims=True) + pn
    terms = [w[:, j:j + 1] * blk_out[j] for j in blocks] + [pn * vn_rows]
    while len(terms) > 1:
        terms = [terms[i] + terms[i + 1] for i in range(0, len(terms) - 1, 2)] + terms[len(terms) & ~1:]
    o_ref[0] = terms[0] / denom


def _moba_sample(q, kn, vn, cache_kt, cache_vt, page_table):
    bs = q.shape[0]
    n_pages = page_table.shape[1]
    _, n_kv, hd, page = cache_kt.shape
    past = n_pages * page
    n_rows = n_kv * SUBLANE
    assert ATT_GROUP <= SUBLANE

    def head_rows(t):
        return jnp.pad(t, ((0, 0), (0, 0), (0, SUBLANE - t.shape[2]), (0, 0))).reshape(bs, n_rows, hd)

    q_rows = head_rows(q.reshape(bs, ATT_GROUP, n_kv, hd).transpose(0, 2, 1, 3))
    kn_rows = jnp.broadcast_to(kn.reshape(bs, n_kv, 1, hd), (bs, n_kv, SUBLANE, hd)).reshape(bs, n_rows, hd)
    vn_rows = jnp.broadcast_to(vn.reshape(bs, n_kv, 1, hd), (bs, n_kv, SUBLANE, hd)).reshape(bs, n_rows, hd)
    one = pl.BlockSpec((1, n_rows, hd), lambda b, pt: (b, 0, 0))
    avg = jnp.where(jnp.arange(past)[:, None] // MOBA_BLOCK == jnp.arange(LANE)[None, :], 1.0 / MOBA_BLOCK, 0.0).astype(BF16)

    def page_spec(p):
        return pl.BlockSpec((1, n_kv, hd, page), lambda b, pt: (pt[b, p], 0, 0, 0))

    grid_spec = pltpu.PrefetchScalarGridSpec(
        num_scalar_prefetch=1,
        grid=(bs,),
        in_specs=[one, one, one, pl.BlockSpec((past, LANE), lambda b, pt: (0, 0))]
        + [page_spec(p) for p in range(n_pages)] * 2,
        out_specs=one,
    )
    o_rows = pl.pallas_call(
        functools.partial(_moba_sample_kernel, n_pages=n_pages),
        grid_spec=grid_spec,
        out_shape=jax.ShapeDtypeStruct((bs, n_rows, hd), F32),
        compiler_params=_cparams("parallel"),
        name="moba_sample",
    )(page_table, q_rows, kn_rows, vn_rows, avg, *([cache_kt] * n_pages), *([cache_vt] * n_pages))
    return o_rows.reshape(bs, n_kv, SUBLANE, hd)[:, :, :ATT_GROUP].transpose(0, 2, 1, 3).reshape(bs, ATT_GROUP * n_kv * hd)


def _pad_lanes(v, fill=0.0):
    return jnp.pad(v.reshape(1, -1), ((0, 0), (0, LANE - v.shape[0])), constant_values=fill)


def kernel(x_prompt, x_sample, state_sc_conv, state_ssd_conv, state_ssm, cache_k, cache_v, page_table,
           c_prompt, c_sample, norm_mix_w, norm_mlp_w, norm_final_w, w_ada, b_ada, w_in_e, sc_conv_w,
           ssd_conv_w, ssd_conv_b, ssd_dt_bias, ssd_a_log, ssd_d, ssd_norm_w, w_out_e, w_qkv, w_o, w_up, w_down):
    bp, seq, d = x_prompt.shape
    bs, dec_seq, _ = x_sample.shape
    assert dec_seq == 1 and seq % MOBA_BLOCK == 0 and seq % SSD_CHUNK == 0 and seq // MOBA_BLOCK <= SEL_LANES
    n_pages, page = page_table.shape[1], cache_k.shape[2]
    past_len = n_pages * page
    assert past_len % MOBA_BLOCK == 0 and MOBA_BLOCK % page == 0
    dsc = sc_conv_w.shape[2]
    conv_dim = ssd_conv_w.shape[2]
    n_heads = ssd_d.shape[1]
    d_inner = n_heads * SSD_HEAD_DIM
    main = 3 * dsc + d_inner + conv_dim
    kvw = ATT_KV_HEADS * ATT_HEAD_DIM
    nq = ATT_HEADS * ATT_HEAD_DIM

    xp = x_prompt.reshape(bp * seq, d)
    xs = x_sample.reshape(bs, d)

    rows = -(-(bs + bp) // SUBLANE) * SUBLANE
    c_all = jnp.pad(jnp.concatenate([c_sample, c_prompt], axis=0), ((0, rows - bs - bp), (0, 0)))
    ada = _ada(c_all, w_ada, b_ada)

    w_in = w_in_e[0].astype(BF16)
    w_dt = jnp.pad(w_in[:, main:], ((0, 0), (0, LANE - n_heads)))
    w_out = w_out_e[0].astype(BF16)
    wq = w_qkv[0][:, :nq].reshape(d, ATT_KV_HEADS, ATT_GROUP, ATT_HEAD_DIM).transpose(0, 2, 1, 3).reshape(d, nq)
    w_qkv_p = jnp.concatenate([wq * (ATT_HEAD_DIM ** -0.5 * math.log2(math.e)), w_qkv[0][:, nq:]], axis=1).astype(BF16)
    w_o_p = w_o[0].reshape(ATT_KV_HEADS, ATT_GROUP, ATT_HEAD_DIM, d).transpose(1, 0, 2, 3).reshape(nq, d).astype(BF16)
    w_up_b = w_up.astype(BF16)
    w_down_b = w_down.astype(BF16)

    scw = sc_conv_w[0]
    cw = ssd_conv_w[0]
    cb = ssd_conv_b[0].reshape(1, conv_dim)
    dtb = _pad_lanes(ssd_dt_bias[0])
    alog = _pad_lanes(ssd_a_log[0])
    dexp = jnp.repeat(ssd_d[0], SSD_HEAD_DIM).reshape(1, d_inner)
    ssd_nw = ssd_norm_w[0].reshape(1, d_inner)

    mod_p = ada[0, bs:bs + bp].reshape(bp, 1, 6 * d)
    mod_s = ada[0, :bs].reshape(1, bs, 6 * d)
    nw_mix = norm_mix_w[0].reshape(1, d)
    nw_mlp = norm_mlp_w[0].reshape(1, d)
    fw = norm_final_w.reshape(1, d)

    proj_p, dt_p = _nm_matmul(xp, nw_mix, mod_p, 0, 1, w_in, main, w_dt, seq, min(seq, 1024), 2048)
    ycat_p, scst_p, cvst_p, ssm_p = _ssd_prompt(proj_p, dt_p, scw, cw, cb, dtb, alog, dexp, ssd_nw, bp, seq)
    tm_p = min(seq, 1024)
    xp = _mm_res(ycat_p, w_out, xp, mod_p, 2, seq, tm_p)
    xp = _mlp(xp, nw_mlp, mod_p, w_up_b[0], w_down_b[0], fw, seq, tm_p, False)

    proj_s, dt_s = _nm_matmul(xs, nw_mix, mod_s, 0, 1, w_in, main, w_dt, bs, bs, 1024)
    ysc_s, scn_s, cvn_s, xs_s, b_s, c_s, xdt_t, dec_s = _even_sample_prep(
        proj_s, dt_s, state_sc_conv[0].reshape(bs, -1), state_ssd_conv[0].reshape(bs, -1),
        scw, cw, cb, dtb, alog, d_inner)
    ssm_s, y_t = _ssm_sample(state_ssm[0], xdt_t, dec_s, b_s, c_s)
    xs = _even_out_sample(y_t, xs_s, proj_s, ysc_s, dexp, ssd_nw, w_out, xs, mod_s)
    xs = _mlp(xs, nw_mlp, mod_s, w_up_b[0], w_down_b[0], fw, bs, bs, False)

    mod_p = ada[1, bs:bs + bp].reshape(bp, 1, 6 * d)
    mod_s = ada[1, :bs].reshape(1, bs, 6 * d)
    nw_mix = norm_mix_w[1].reshape(1, d)
    nw_mlp = norm_mlp_w[1].reshape(1, d)

    tabs_p = _rope_tables(seq, 0, 1)
    q_p, kt_p, vt_p, km_p, klo_p, khi_p, vtb_p = _qkv(xp, nw_mix, mod_p, w_qkv_p, tabs_p, seq, MOBA_BLOCK, True)
    n_blk = seq // MOBA_BLOCK
    kmean_p = jnp.pad(km_p.reshape(bp, n_blk, kvw), ((0, 0), (0, SEL_LANES - n_blk), (0, 0)))
    o_p = _moba_prompt(q_p, klo_p, khi_p, vtb_p, kmean_p, bp, seq)
    xp = _mm_res(o_p, w_o_p, xp, mod_p, 2, seq, tm_p)
    y_prompt = _mlp(xp, nw_mlp, mod_p, w_up_b[1], w_down_b[1], fw, seq, tm_p, True)

    tabs_s = _rope_tables(bs, past_len, 0)
    q_s, k_s, v_s = _qkv(xs, nw_mix, mod_s, w_qkv_p, tabs_s, bs, bs, False)
    o_s = _moba_sample(q_s, k_s, v_s, cache_k[0].transpose(0, 2, 3, 1), cache_v[0].transpose(0, 2, 3, 1), page_table)
    xs = _mm_res(o_s.reshape(bs, nq).astype(BF16), w_o_p, xs, mod_s, 2, bs, bs)
    y_sample = _mlp(xs, nw_mlp, mod_s, w_up_b[1], w_down_b[1], fw, bs, bs, True)

    def kv_out(t):
        return t.reshape(bp, ATT_KV_HEADS, ATT_HEAD_DIM, seq).transpose(0, 3, 1, 2)[None]

    return (y_prompt.reshape(bp, seq, d), y_sample.reshape(bs, 1, d),
            scst_p[:, SUBLANE - 2:, :][None], scn_s.reshape(1, bs, 2, dsc),
            cvst_p[:, SUBLANE - 3:, :][None], cvn_s.reshape(1, bs, 3, conv_dim),
            ssm_p.reshape(1, bp, n_heads, SSD_HEAD_DIM, SSD_STATE), ssm_s[None],
            kv_out(kt_p), k_s.reshape(1, bs, 1, ATT_KV_HEADS, ATT_HEAD_DIM),
            kv_out(vt_p), v_s.reshape(1, bs, 1, ATT_KV_HEADS, ATT_HEAD_DIM))
```
